```python
import math
import jax
import jax.numpy as jnp
from jax import lax
import numpy as np

D_MODEL = 2048
BATCH = 16
SEQ = 256
DEPTH = 2
DEC_BATCH = 2
DEC_SEQ = 2048
PAST_LEN = 512

GRID_W = 64
ROPE_BASE = 10000.0
EPS = 1e-6
Q_BLOCK = 128
N_EVEN = (DEPTH + 1) // 2
N_ODD = DEPTH // 2
D_FF = 5632
N_MOD = 9
A_HEADS = 8
A_DH = 64
A_DV = 128
B_HEADS = 4
B_DK = 128
B_DV = 256
B_RANK = 16
B_TAU = 16.0
B_CHUNK = 32
A_QK_W = A_HEADS * 2 * A_DH
A_V_W = A_HEADS * A_DV
B_QK_W = B_HEADS * B_DK
B_V_W = B_HEADS * B_DV
AB_IN = 2 * A_QK_W + A_V_W + 2 * B_QK_W + 2 * B_V_W + 2 * B_RANK
AB_OUT = A_V_W + B_V_W
AB_SPLITS = (A_QK_W,
             2 * A_QK_W,
             2 * A_QK_W + A_V_W,
             2 * A_QK_W + A_V_W + B_QK_W,
             2 * A_QK_W + A_V_W + 2 * B_QK_W,
             2 * A_QK_W + A_V_W + 2 * B_QK_W + B_V_W,
             2 * A_QK_W + A_V_W + 2 * B_QK_W + 2 * B_V_W)
C_HEADS = 16
C_KV_HEADS = 4
C_DH = 128
C_WINDOW = 128
C_BLOCK = 128
C_IN = (C_HEADS + 2 * C_KV_HEADS) * C_DH
C_OUT = C_HEADS * C_DH
C_SPLITS = (C_HEADS * C_DH, (C_HEADS + C_KV_HEADS) * C_DH)

kernel_name = 'hybrid_prefix_diffusion_trunk_step'


def rms_norm(x, g):
    xf = x.astype(jnp.float32)
    y = xf * lax.rsqrt(jnp.mean(xf * xf, axis=-1, keepdims=True) + EPS)
    return (y * g.astype(jnp.float32)).astype(x.dtype)


def adaln(cond, w, b):
    return (jax.nn.silu(cond) @ w + b).reshape(cond.shape[0], N_MOD, D_MODEL)


def modulate(h, shift, scale):
    return h * (1.0 + scale[:, None, :]) + shift[:, None, :]


def half_ffn(x, mod, i, g, w_in, w_out):
    h = modulate(rms_norm(x, g), mod[:, i], mod[:, i + 1])
    gate, up = jnp.split(h @ w_in, 2, axis=-1)
    return x + 0.5 * mod[:, i + 2][:, None, :] * ((jax.nn.silu(gate) * up) @ w_out)


def heads(t, n_heads):
    b, n, _ = t.shape
    return t.reshape(b, n, n_heads, -1).transpose(0, 2, 1, 3)


def merge_heads(t):
    b, h, n, d = t.shape
    return t.transpose(0, 2, 1, 3).reshape(b, n, h * d)


def axial_angles(n, head_dim):
    rows = n // GRID_W
    row = jnp.repeat(jnp.arange(rows, dtype=jnp.float32), GRID_W)
    col = jnp.tile(jnp.arange(GRID_W, dtype=jnp.float32), rows)
    d_axis = head_dim // 2
    inv = ROPE_BASE ** (-jnp.arange(0, d_axis, 2, dtype=jnp.float32) / d_axis)
    return row[:, None] * inv[None, :], col[:, None] * inv[None, :]


def rotate(x, ang):
    x1, x2 = jnp.split(x, 2, axis=-1)
    cos, sin = jnp.cos(ang), jnp.sin(ang)
    return jnp.concatenate([x1 * cos - x2 * sin, x1 * sin + x2 * cos], axis=-1)


def axial_rope(x, angles):
    ang_r, ang_c = angles
    xr, xc = jnp.split(x.astype(jnp.float32), 2, axis=-1)
    return jnp.concatenate([rotate(xr, ang_r), rotate(xc, ang_c)], axis=-1).astype(x.dtype)


def sweep_query_blocks(fn, *qs):
    b, h, n, _ = qs[0].shape
    nb = n // Q_BLOCK
    qb = tuple(jnp.moveaxis(q.reshape(b, h, nb, Q_BLOCK, q.shape[-1]), 2, 0) for q in qs)
    out = lax.map(lambda blk: fn(*blk), qb)
    return jnp.moveaxis(out, 0, 2).reshape(b, h, n, out.shape[-1])


def diff_lambda(lam_p, layer):
    lam_init = 0.8 - 0.6 * math.exp(-0.3 * layer)
    lp = lam_p.astype(jnp.float32)
    lam = jnp.exp(jnp.sum(lp[0] * lp[1])) - jnp.exp(jnp.sum(lp[2] * lp[3])) + lam_init
    return lam, lam_init


def diff_attention(q1, q2, k1, k2, v, lam):
    scale = A_DH ** -0.5

    def blk(a, b):
        s1 = jnp.einsum('bhqd,bhkd->bhqk', a, k1).astype(jnp.float32) * scale
        s2 = jnp.einsum('bhqd,bhkd->bhqk', b, k2).astype(jnp.float32) * scale
        p = jax.nn.softmax(s1, axis=-1) - lam * jax.nn.softmax(s2, axis=-1)
        return jnp.einsum('bhqk,bhkd->bhqd', p.astype(v.dtype), v)

    return sweep_query_blocks(blk, q1, q2)


def gla_chunk_scan(q, k, v, log_a, s0):
    b, h, n, _ = q.shape
    dv = v.shape[-1]
    nc = n // B_CHUNK

    def chunks(t):
        return jnp.moveaxis(t.astype(jnp.float32).reshape(b, h, nc, B_CHUNK, t.shape[-1]), 2, 0)

    causal = jnp.tril(jnp.ones((B_CHUNK, B_CHUNK), dtype=bool))[:, :, None]

    def step(state, inp):
        qc, kc, vc, gc = inp
        cum = jnp.cumsum(gc, axis=2)
        cum_last = cum[:, :, -1:, :]
        inter = jnp.einsum('bhtd,bhde->bhte', qc * jnp.exp(cum), state)
        rel = jnp.where(causal, cum[:, :, :, None, :] - cum[:, :, None, :, :], -jnp.inf)
        scores = jnp.einsum('bhtd,bhsd,bhtsd->bhts', qc, kc, jnp.exp(rel))
        intra = jnp.einsum('bhts,bhse->bhte', scores, vc)
        new_state = (jnp.exp(cum_last[:, :, 0, :])[..., None] * state
                     + jnp.einsum('bhsd,bhse->bhde', kc * jnp.exp(cum_last - cum), vc))
        return new_state, inter + intra

    s_fin, o = lax.scan(step, s0.astype(jnp.float32),
                        (chunks(q), chunks(k), chunks(v), chunks(log_a)))
    o = jnp.moveaxis(o, 0, 2).reshape(b, h, n, dv)
    return o.astype(v.dtype), s_fin


def gla_bidirectional(q, k, v, la_f, la_b, s_f, s_b):
    o_f, s_f_new = gla_chunk_scan(q, k, v, la_f, s_f)
    flip = lambda t: jnp.flip(t, axis=2)
    o_b, s_b_new = gla_chunk_scan(flip(q), flip(k), flip(v), flip(la_b), s_b)
    return o_f + flip(o_b), s_f_new, s_b_new


def ab_project(h, w_in, alpha_w, alpha_b):
    a_q, a_k, a_v, b_q, b_k, b_v, b_r, b_low = jnp.split(h @ w_in, AB_SPLITS, axis=-1)
    a_q = heads(a_q, A_HEADS)
    a_k = heads(a_k, A_HEADS)
    attn = (a_q[..., :A_DH], a_q[..., A_DH:], a_k[..., :A_DH], a_k[..., A_DH:], heads(a_v, A_HEADS))
    low_f, low_b = jnp.split(b_low, 2, axis=-1)
    la_f = jax.nn.log_sigmoid((low_f @ alpha_w[0] + alpha_b[0]).astype(jnp.float32)) / B_TAU
    la_b = jax.nn.log_sigmoid((low_b @ alpha_w[1] + alpha_b[1]).astype(jnp.float32)) / B_TAU
    rec = (heads(b_q, B_HEADS) * (B_DK ** -0.5), heads(b_k, B_HEADS), heads(b_v, B_HEADS), b_r,
           heads(la_f, B_HEADS), heads(la_b, B_HEADS))
    return attn, rec


def ab_merge(a_o, b_o, b_r, lam_init, subln_g, bnorm_g, w_out):
    a_out = merge_heads(rms_norm(a_o, subln_g) * (1.0 - lam_init))
    b_out = merge_heads(rms_norm(b_o, bnorm_g)) * jax.nn.silu(b_r)
    return jnp.concatenate([a_out, b_out], axis=-1) @ w_out


def ab_mixer_context(h, lam, lam_init, w_in, w_out, subln_g, alpha_w, alpha_b, bnorm_g):
    (q1, q2, k1, k2, v), (bq, bk, bv, br, la_f, la_b) = ab_project(h, w_in, alpha_w, alpha_b)
    a_o = diff_attention(q1, q2, k1, k2, v, lam)
    zero = jnp.zeros((h.shape[0], B_HEADS, B_DK, B_DV), jnp.float32)
    b_o, s_f, s_b = gla_bidirectional(bq, bk, bv, la_f, la_b, zero, zero)
    out = ab_merge(a_o, b_o, br, lam_init, subln_g, bnorm_g, w_out)
    return out, (jnp.concatenate([k1, k2], axis=-1), v, s_f, s_b)


def ab_mixer_latent(h, ctx_k, ctx_v, s_f, s_b, ang, lam, lam_init, w_in, w_out, subln_g,
                    alpha_w, alpha_b, bnorm_g):
    (q1, q2, k1, k2, v), (bq, bk, bv, br, la_f, la_b) = ab_project(h, w_in, alpha_w, alpha_b)
    q1, q2, k1, k2 = (axial_rope(q1, ang), axial_rope(q2, ang), axial_rope(k1, ang), axial_rope(k2, ang))
    ck1, ck2 = jnp.split(ctx_k, 2, axis=-1)
    a_o = diff_attention(q1, q2,
                         jnp.concatenate([ck1, k1], axis=2),
                         jnp.concatenate([ck2, k2], axis=2),
                         jnp.concatenate([ctx_v, v], axis=2), lam)
    b_o, _, _ = gla_bidirectional(bq, bk, bv, la_f, la_b, s_f, s_b)
    return ab_merge(a_o, b_o, br, lam_init, subln_g, bnorm_g, w_out)


def c_project(h, w_in):
    q, k, v = jnp.split(h @ w_in, C_SPLITS, axis=-1)
    return heads(q, C_HEADS), heads(k, C_KV_HEADS), heads(v, C_KV_HEADS)


def sink_dense_attention(q, k, v, sink):
    b, hq, _, dh = q.shape
    hkv = k.shape[1]
    g = hq // hkv
    scale = dh ** -0.5
    sink_g = sink.astype(jnp.float32).reshape(hkv, g)

    def blk(qb):
        qg = qb.reshape(b, hkv, g, Q_BLOCK, dh)
        s = jnp.einsum('bhgqd,bhkd->bhgqk', qg, k).astype(jnp.float32) * scale
        sk = jnp.broadcast_to(sink_g[None, :, :, None, None], s.shape[:-1] + (1,))
        p = jax.nn.softmax(jnp.concatenate([sk, s], axis=-1), axis=-1)[..., 1:]
        o = jnp.einsum('bhgqk,bhkd->bhgqd', p.astype(v.dtype), v)
        return o.reshape(b, hq, Q_BLOCK, dh)

    return sweep_query_blocks(blk, q)


def window_sink_attention(q, k, v, kc, vc, sink):
    b, hq, n, dh = q.shape
    hkv = k.shape[1]
    g = hq // hkv
    nb = n // C_BLOCK
    nc = kc.shape[2]
    scale = dh ** -0.5
    qg = q.reshape(b, hkv, g, nb, C_BLOCK, dh)
    pad = ((0, 0), (0, 0), (C_BLOCK, C_BLOCK), (0, 0))

    def band(t):
        return jnp.concatenate(
            [t[:, :, j * C_BLOCK: j * C_BLOCK + n].reshape(b, hkv, nb, C_BLOCK, dh) for j in range(3)],
            axis=3)

    kb, vb = band(jnp.pad(k, pad)), band(jnp.pad(v, pad))
    blk_idx = jnp.arange(nb)[:, None] * C_BLOCK
    qpos = blk_idx + jnp.arange(C_BLOCK)[None, :]
    kpos = blk_idx - C_BLOCK + jnp.arange(3 * C_BLOCK)[None, :]
    valid = ((kpos[:, None, :] >= 0) & (kpos[:, None, :] < n)
             & (jnp.abs(qpos[:, :, None] - kpos[:, None, :]) <= C_WINDOW))
    s_band = jnp.einsum('bhgnqd,bhnkd->bhgnqk', qg, kb).astype(jnp.float32) * scale
    s_band = jnp.where(valid, s_band, -jnp.inf)
    s_ctx = jnp.einsum('bhgnqd,bhkd->bhgnqk', qg, kc).astype(jnp.float32) * scale
    sink_g = sink.astype(jnp.float32).reshape(hkv, g)
    sk = jnp.broadcast_to(sink_g[None, :, :, None, None, None], s_ctx.shape[:-1] + (1,))
    p = jax.nn.softmax(jnp.concatenate([sk, s_ctx, s_band], axis=-1), axis=-1)
    p_ctx = p[..., 1:1 + nc].astype(v.dtype)
    p_band = p[..., 1 + nc:].astype(v.dtype)
    o = (jnp.einsum('bhgnqk,bhkd->bhgnqd', p_ctx, vc)
         + jnp.einsum('bhgnqk,bhnkd->bhgnqd', p_band, vb))
    return o.reshape(b, hq, n, dh)


def c_mixer_context(h, w_in, w_out, sink):
    q, k, v = c_project(h, w_in)
    o = sink_dense_attention(q, k, v, sink)
    return merge_heads(o) @ w_out, (k, v)


def c_mixer_latent(h, ctx_k, ctx_v, ang, w_in, w_out, sink):
    q, k, v = c_project(h, w_in)
    q, k = axial_rope(q, ang), axial_rope(k, ang)
    o = window_sink_attention(q, k, v, ctx_k, ctx_v, sink)
    return merge_heads(o) @ w_out


def setup_inputs(seed: int = 0) -> dict:
    key = jax.random.key(seed)
    ks = jax.random.split(key, 26)

    def nrm(k, shape, s=1.0):
        return s * jax.random.normal(k, shape, dtype=jnp.float32)

    return {
        'x_prompt': nrm(ks[0], (BATCH, SEQ, D_MODEL)),
        'x_sample': nrm(ks[1], (DEC_BATCH, DEC_SEQ, D_MODEL)),
        'cache_a_k': nrm(ks[2], (DEC_BATCH, N_EVEN, A_HEADS, PAST_LEN, 2 * A_DH)),
        'cache_a_v': nrm(ks[3], (DEC_BATCH, N_EVEN, A_HEADS, PAST_LEN, A_DV)),
        'state_b_fwd': nrm(ks[4], (DEC_BATCH, N_EVEN, B_HEADS, B_DK, B_DV), 0.5),
        'state_b_bwd': nrm(ks[5], (DEC_BATCH, N_EVEN, B_HEADS, B_DK, B_DV), 0.5),
        'cache_c_k': nrm(ks[6], (DEC_BATCH, N_ODD, C_KV_HEADS, PAST_LEN, C_DH)),
        'cache_c_v': nrm(ks[7], (DEC_BATCH, N_ODD, C_KV_HEADS, PAST_LEN, C_DH)),
        'c': nrm(ks[8], (DEC_BATCH, D_MODEL)),
        'c_ctx': nrm(ks[9], (D_MODEL,)),
        'ada_w': nrm(ks[10], (DEPTH, D_MODEL, N_MOD * D_MODEL), D_MODEL ** -0.5),
        'ada_b': nrm(ks[11], (DEPTH, N_MOD * D_MODEL), 0.02),
        'norm_g': 1.0 + nrm(ks[12], (DEPTH, 3, D_MODEL), 0.02),
        'ffn_w_in': nrm(ks[13], (DEPTH, 2, D_MODEL, 2 * D_FF), D_MODEL ** -0.5),
        'ffn_w_out': nrm(ks[14], (DEPTH, 2, D_FF, D_MODEL), D_FF ** -0.5),
        'ab_w_in': nrm(ks[15], (N_EVEN, D_MODEL, AB_IN), D_MODEL ** -0.5),
        'ab_w_out': nrm(ks[16], (N_EVEN, AB_OUT, D_MODEL), AB_OUT ** -0.5),
        'a_lambda': nrm(ks[17], (N_EVEN, 4, A_DH), 0.1),
        'a_subln_g': 1.0 + nrm(ks[18], (N_EVEN, A_DV), 0.02),
        'b_alpha_w': nrm(ks[19], (N_EVEN, 2, B_RANK, B_QK_W), B_RANK ** -0.5),
        'b_alpha_b': nrm(ks[20], (N_EVEN, 2, B_QK_W), 0.02),
        'b_norm_g': 1.0 + nrm(ks[21], (N_EVEN, B_DV), 0.02),
        'c_w_in': nrm(ks[22], (N_ODD, D_MODEL, C_IN), D_MODEL ** -0.5),
        'c_w_out': nrm(ks[23], (N_ODD, C_OUT, D_MODEL), C_OUT ** -0.5),
        'c_sink': nrm(ks[24], (N_ODD, C_HEADS), 0.5),
        'final_g': 1.0 + nrm(ks[25], (D_MODEL,), 0.02),
    }


def reference(x_prompt, x_sample, cache_a_k, cache_a_v, state_b_fwd, state_b_bwd, cache_c_k,
              cache_c_v, c, c_ctx, ada_w, ada_b, norm_g, ffn_w_in, ffn_w_out, ab_w_in, ab_w_out,
              a_lambda, a_subln_g, b_alpha_w, b_alpha_b, b_norm_g, c_w_in, c_w_out, c_sink, final_g):
    n_lat = x_sample.shape[1]
    ang_a = axial_angles(n_lat, A_DH)
    ang_c = axial_angles(n_lat, C_DH)
    xp, xs = x_prompt, x_sample
    a_k_list, a_v_list, b_f_list, b_b_list, c_k_list, c_v_list = [], [], [], [], [], []
    for l in range(DEPTH):
        mod_p = adaln(c_ctx[None, :], ada_w[l], ada_b[l])
        mod_s = adaln(c, ada_w[l], ada_b[l])
        xp = half_ffn(xp, mod_p, 0, norm_g[l, 0], ffn_w_in[l, 0], ffn_w_out[l, 0])
        xs = half_ffn(xs, mod_s, 0, norm_g[l, 0], ffn_w_in[l, 0], ffn_w_out[l, 0])
        hp = modulate(rms_norm(xp, norm_g[l, 1]), mod_p[:, 3], mod_p[:, 4])
        hs = modulate(rms_norm(xs, norm_g[l, 1]), mod_s[:, 3], mod_s[:, 4])
        if l % 2 == 0:
            e = l // 2
            lam, lam_init = diff_lambda(a_lambda[e], l)
            mix_p, (ak, av, sf, sb) = ab_mixer_context(
                hp, lam, lam_init, ab_w_in[e], ab_w_out[e], a_subln_g[e], b_alpha_w[e],
                b_alpha_b[e], b_norm_g[e])
            mix_s = ab_mixer_latent(
                hs, cache_a_k[:, e], cache_a_v[:, e], state_b_fwd[:, e], state_b_bwd[:, e], ang_a,
                lam, lam_init, ab_w_in[e], ab_w_out[e], a_subln_g[e], b_alpha_w[e], b_alpha_b[e],
                b_norm_g[e])
            a_k_list.append(ak)
            a_v_list.append(av)
            b_f_list.append(sf)
            b_b_list.append(sb)
        else:
            o = l // 2
            mix_p, (ck, cv) = c_mixer_context(hp, c_w_in[o], c_w_out[o], c_sink[o])
            mix_s = c_mixer_latent(hs, cache_c_k[:, o], cache_c_v[:, o], ang_c, c_w_in[o],
                                   c_w_out[o], c_sink[o])
            c_k_list.append(ck)
            c_v_list.append(cv)
        xp = xp + mod_p[:, 5][:, None, :] * mix_p
        xs = xs + mod_s[:, 5][:, None, :] * mix_s
        xp = half_ffn(xp, mod_p, 6, norm_g[l, 2], ffn_w_in[l, 1], ffn_w_out[l, 1])
        xs = half_ffn(xs, mod_s, 6, norm_g[l, 2], ffn_w_in[l, 1], ffn_w_out[l, 1])
    y_prompt = rms_norm(xp, final_g)
    y_sample = rms_norm(xs, final_g)
    new_a_k = jnp.stack(a_k_list, axis=1)
    new_a_v = jnp.stack(a_v_list, axis=1)
    new_b_fwd = jnp.stack(b_f_list, axis=1)
    new_b_bwd = jnp.stack(b_b_list, axis=1)
    new_c_k = jnp.stack(c_k_list, axis=1)
    new_c_v = jnp.stack(c_v_list, axis=1)
    return (y_prompt, y_sample, new_a_k, new_a_v, new_b_fwd, new_b_bwd, new_c_k, new_c_v)
```

```python
import functools
import math

import jax
import jax.numpy as jnp
from jax import lax
from jax.experimental import pallas as pl
from jax.experimental.pallas import tpu as pltpu

F32 = jnp.float32
BF16 = jnp.bfloat16

EPS = 1e-6
GRID_W = 64
ROPE_BASE = 10000.0
N_MOD = 9
A_HEADS, A_DH, A_DV = 8, 64, 128
B_HEADS, B_DK, B_DV, B_RANK, B_TAU = 4, 128, 256, 16, 16.0
C_HEADS, C_KV_HEADS, C_DH, C_WINDOW = 16, 4, 128, 128
A_QK_W = A_HEADS * 2 * A_DH
A_V_W = A_HEADS * A_DV
B_QK_W = B_HEADS * B_DK
B_V_W = B_HEADS * B_DV
AB_MAIN = 2 * A_QK_W + A_V_W + 2 * B_QK_W + 2 * B_V_W

V7X_VMEM_BYTES = 64 * 1024 * 1024
VMEM_LIMIT_BYTES = V7X_VMEM_BYTES - 8 * 1024 * 1024
COND_ROWS = 8
ROW_CHUNK = 128


def _tile(n, pref, align=128):
    if n <= pref:
        return n
    t = (pref // align) * align
    while n % t:
        t -= align
    assert t > 0, (n, pref)
    return t


def _cparams(*sem):
    return pltpu.CompilerParams(dimension_semantics=sem, vmem_limit_bytes=VMEM_LIMIT_BYTES)


def _bdot(a, b):
    return jnp.dot(a.astype(BF16), b.astype(BF16), preferred_element_type=F32)


def _rms(x):
    return x * lax.rsqrt(jnp.mean(x * x, axis=-1, keepdims=True) + EPS)


def _adaln_kernel(cond_ref, w_ref, b_ref, o_ref):
    c = cond_ref[...]
    o_ref[0] = _bdot(c * jax.nn.sigmoid(c), w_ref[0]) + b_ref[0]


def adaln(cond, ada_w, ada_b):
    depth, d, n = ada_w.shape
    tn = _tile(n, 1024)
    return pl.pallas_call(
        _adaln_kernel,
        grid=(depth, n // tn),
        in_specs=[pl.BlockSpec((COND_ROWS, d), lambda l, j: (0, 0)),
                  pl.BlockSpec((1, d, tn), lambda l, j: (l, 0, j)),
                  pl.BlockSpec((1, 1, tn), lambda l, j: (l, 0, j))],
        out_specs=pl.BlockSpec((1, COND_ROWS, tn), lambda l, j: (l, 0, j)),
        out_shape=jax.ShapeDtypeStruct((depth, COND_ROWS, n), F32),
        compiler_params=_cparams("arbitrary", "arbitrary"),
        name="adaln",
    )(cond, ada_w, ada_b.reshape(depth, 1, n))


def _norm_modulate(x_ref, g_ref, mod_ref, h_ref):
    g = g_ref[...]
    shift = mod_ref[0, 0:1, :]
    scale1 = 1.0 + mod_ref[0, 1:2, :]

    def body(r, carry):
        rows = pl.ds(pl.multiple_of(r * ROW_CHUNK, ROW_CHUNK), ROW_CHUNK)
        h_ref[rows, :] = ((_rms(x_ref[rows, :]) * g) * scale1 + shift).astype(BF16)
        return carry

    lax.fori_loop(0, x_ref.shape[0] // ROW_CHUNK, body, 0)


def _row_tile(rows_info):
    n_prompt_rows, lat_rows = rows_info
    tm = _tile(math.gcd(n_prompt_rows, lat_rows), 1024, align=ROW_CHUNK)
    return tm


def _cond_row(i, tm, n_prompt_rows, lat_rows):
    return jnp.maximum((i * tm - n_prompt_rows) // lat_rows + 1, 0)


def _ffn_kernel(x_ref, mod_ref, g_ref, wg_ref, wu_ref, wo_ref, *rest, n_chunk, final):
    if final:
        fg_ref, o_ref, h_ref = rest
    else:
        o_ref, h_ref = rest
    j = pl.program_id(1)

    @pl.when(j == 0)
    def _():
        _norm_modulate(x_ref, g_ref, mod_ref, h_ref)

    h = h_ref[...]
    gate = jnp.dot(h, wg_ref[...].astype(BF16), preferred_element_type=F32)
    up = jnp.dot(h, wu_ref[...].astype(BF16), preferred_element_type=F32)
    act = (gate * jax.nn.sigmoid(gate) * up).astype(BF16)
    d = o_ref.shape[1]
    for c in range(d // n_chunk):
        cols = slice(c * n_chunk, (c + 1) * n_chunk)
        part = jnp.dot(act, wo_ref[:, cols].astype(BF16), preferred_element_type=F32)

        @pl.when(j == 0)
        def _():
            o_ref[:, cols] = part

        @pl.when(j > 0)
        def _():
            o_ref[:, cols] += part

    @pl.when(j == pl.num_programs(1) - 1)
    def _():
        half_gate = 0.5 * mod_ref[0, 2:3, :]

        def body(r, carry):
            rows = pl.ds(pl.multiple_of(r * ROW_CHUNK, ROW_CHUNK), ROW_CHUNK)
            xn = x_ref[rows, :] + half_gate * o_ref[rows, :]
            if final:
                xn = _rms(xn) * fg_ref[...]
            o_ref[rows, :] = xn
            return carry

        lax.fori_loop(0, x_ref.shape[0] // ROW_CHUNK, body, 0)


def ffn_half_step(x, mod3, g, w_in, w_out, rows_info, final_g=None):
    m, d = x.shape
    f = w_out.shape[0]
    n_prompt_rows, lat_rows = rows_info
    tm, tf, n_chunk = _row_tile(rows_info), _tile(f, 256), _tile(d, 512)
    nf = f // tf
    final = final_g is not None
    row = lambda i, j: (_cond_row(i, tm, n_prompt_rows, lat_rows), 0, 0)
    in_specs = [pl.BlockSpec((tm, d), lambda i, j: (i, 0)),
                pl.BlockSpec((1, 3, d), row),
                pl.BlockSpec((1, d), lambda i, j: (0, 0)),
                pl.BlockSpec((d, tf), lambda i, j: (0, j)),
                pl.BlockSpec((d, tf), lambda i, j: (0, nf + j)),
                pl.BlockSpec((tf, d), lambda i, j: (j, 0))]
    args = [x, mod3, g.reshape(1, d), w_in, w_in, w_out]
    if final:
        in_specs.append(pl.BlockSpec((1, d), lambda i, j: (0, 0)))
        args.append(final_g.reshape(1, d))
    return pl.pallas_call(
        functools.partial(_ffn_kernel, n_chunk=n_chunk, final=final),
        grid=(m // tm, nf),
        in_specs=in_specs,
        out_specs=pl.BlockSpec((tm, d), lambda i, j: (i, 0)),
        out_shape=jax.ShapeDtypeStruct((m, d), F32),
        scratch_shapes=[pltpu.VMEM((tm, d), BF16)],
        compiler_params=_cparams("arbitrary", "arbitrary"),
        name="ffn_final" if final else "ffn",
    )(*args)


def _proj_kernel(x_ref, mod_ref, g_ref, w_ref, o_ref, h_ref):
    @pl.when(pl.program_id(1) == 0)
    def _():
        _norm_modulate(x_ref, g_ref, mod_ref, h_ref)

    o_ref[...] = jnp.dot(h_ref[...], w_ref[...].astype(BF16), preferred_element_type=F32)


def mixer_in_proj(x, mod2, g, w, n_cols, rows_info):
    m, d = x.shape
    n_prompt_rows, lat_rows = rows_info
    tm, tn = _row_tile(rows_info), _tile(n_cols, 512)
    return pl.pallas_call(
        _proj_kernel,
        grid=(m // tm, n_cols // tn),
        in_specs=[pl.BlockSpec((tm, d), lambda i, j: (i, 0)),
                  pl.BlockSpec((1, 2, d), lambda i, j: (_cond_row(i, tm, n_prompt_rows, lat_rows), 0, 0)),
                  pl.BlockSpec((1, d), lambda i, j: (0, 0)),
                  pl.BlockSpec((d, tn), lambda i, j: (0, j))],
        out_specs=pl.BlockSpec((tm, tn), lambda i, j: (i, j)),
        out_shape=jax.ShapeDtypeStruct((m, n_cols), F32),
        scratch_shapes=[pltpu.VMEM((tm, d), BF16)],
        compiler_params=_cparams("arbitrary", "arbitrary"),
        name="mixer_in_proj",
    )(x, mod2, g.reshape(1, d), w)


def _out_proj_kernel(mix_ref, w_ref, x_ref, gate_ref, o_ref):
    o_ref[...] = x_ref[...] + gate_ref[0] * _bdot(mix_ref[...], w_ref[...])


def mixer_out_proj(mix, w, x, gate, rows_info):
    m, k = mix.shape
    d = w.shape[1]
    n_prompt_rows, lat_rows = rows_info
    tm, tn = _row_tile(rows_info), _tile(d, 512)
    return pl.pallas_call(
        _out_proj_kernel,
        grid=(m // tm, d // tn),
        in_specs=[pl.BlockSpec((tm, k), lambda i, j: (i, 0)),
                  pl.BlockSpec((k, tn), lambda i, j: (0, j)),
                  pl.BlockSpec((tm, tn), lambda i, j: (i, j)),
                  pl.BlockSpec((1, 1, tn), lambda i, j: (_cond_row(i, tm, n_prompt_rows, lat_rows), 0, j))],
        out_specs=pl.BlockSpec((tm, tn), lambda i, j: (i, j)),
        out_shape=jax.ShapeDtypeStruct((m, d), F32),
        compiler_params=_cparams("arbitrary", "arbitrary"),
        name="mixer_out_proj",
    )(mix, w, x, gate)


def _heads(t, n):
    b, s, _ = t.shape
    return t.reshape(b, s, n, -1).transpose(0, 2, 1, 3)


def _merge(t):
    b, h, n, d = t.shape
    return t.transpose(0, 2, 1, 3).reshape(b, n, h * d)


def _axial_angles(n, head_dim):
    rows = n // GRID_W
    row = jnp.repeat(jnp.arange(rows, dtype=F32), GRID_W)
    col = jnp.tile(jnp.arange(GRID_W, dtype=F32), rows)
    d_axis = head_dim // 2
    inv = ROPE_BASE ** (-jnp.arange(0, d_axis, 2, dtype=F32) / d_axis)
    return row[:, None] * inv[None, :], col[:, None] * inv[None, :]


def _rotate(x, ang):
    x1, x2 = jnp.split(x, 2, axis=-1)
    cos, sin = jnp.cos(ang), jnp.sin(ang)
    return jnp.concatenate([x1 * cos - x2 * sin, x1 * sin + x2 * cos], axis=-1)


def _axial_rope(x, angles):
    ang_r, ang_c = angles
    xr, xc = jnp.split(x, 2, axis=-1)
    return jnp.concatenate([_rotate(xr, ang_r), _rotate(xc, ang_c)], axis=-1)


def _diff_attn(q1, q2, k1, k2, v, lam):
    scale = A_DH ** -0.5
    s1 = jnp.einsum('bhqd,bhkd->bhqk', q1, k1) * scale
    s2 = jnp.einsum('bhqd,bhkd->bhqk', q2, k2) * scale
    p = jax.nn.softmax(s1, axis=-1) - lam * jax.nn.softmax(s2, axis=-1)
    return jnp.einsum('bhqk,bhkd->bhqd', p, v)


def _gla_scan(q, k, v, log_a, s0, chunk=32):
    b, h, n, _ = q.shape
    dv = v.shape[-1]
    nc = n // chunk
    chunks = lambda t: jnp.moveaxis(t.reshape(b, h, nc, chunk, t.shape[-1]), 2, 0)
    causal = jnp.tril(jnp.ones((chunk, chunk), dtype=bool))[:, :, None]

    def step(state, inp):
        qc, kc, vc, gc = inp
        cum = jnp.cumsum(gc, axis=2)
        cum_last = cum[:, :, -1:, :]
        inter = jnp.einsum('bhtd,bhde->bhte', qc * jnp.exp(cum), state)
        rel = jnp.where(causal, cum[:, :, :, None, :] - cum[:, :, None, :, :], -jnp.inf)
        scores = jnp.einsum('bhtd,bhsd,bhtsd->bhts', qc, kc, jnp.exp(rel))
        intra = jnp.einsum('bhts,bhse->bhte', scores, vc)
        new_state = (jnp.exp(cum_last[:, :, 0, :])[..., None] * state
                     + jnp.einsum('bhsd,bhse->bhde', kc * jnp.exp(cum_last - cum), vc))
        return new_state, inter + intra

    s_fin, o = lax.scan(step, s0, (chunks(q), chunks(k), chunks(v), chunks(log_a)))
    return jnp.moveaxis(o, 0, 2).reshape(b, h, n, dv), s_fin


def _gla_bidir(q, k, v, la_f, la_b, s_f, s_b):
    o_f, s_f_new = _gla_scan(q, k, v, la_f, s_f)
    flip = lambda t: jnp.flip(t, axis=2)
    o_b, s_b_new = _gla_scan(flip(q), flip(k), flip(v), flip(la_b), s_b)
    return o_f + flip(o_b), s_f_new, s_b_new


def _ab_core(proj, low, ctx, ang, lam, lam_init, subln_g, alpha_w, alpha_b, bnorm_g):
    bsz = proj.shape[0]
    o = 0
    cut = lambda w: (proj[..., o:o + w], o + w)
    a_q, o = cut(A_QK_W)
    a_k, o = cut(A_QK_W)
    a_v, o = cut(A_V_W)
    b_q, o = cut(B_QK_W)
    b_k, o = cut(B_QK_W)
    b_v, o = cut(B_V_W)
    b_r, o = cut(B_V_W)
    a_q, a_k, v = _heads(a_q, A_HEADS), _heads(a_k, A_HEADS), _heads(a_v, A_HEADS)
    q1, q2, k1, k2 = a_q[..., :A_DH], a_q[..., A_DH:], a_k[..., :A_DH], a_k[..., A_DH:]
    la_f = jax.nn.log_sigmoid(low[..., :B_RANK] @ alpha_w[0] + alpha_b[0]) / B_TAU
    la_b = jax.nn.log_sigmoid(low[..., B_RANK:] @ alpha_w[1] + alpha_b[1]) / B_TAU
    bq, bk, bv = _heads(b_q, B_HEADS) * (B_DK ** -0.5), _heads(b_k, B_HEADS), _heads(b_v, B_HEADS)
    la_f, la_b = _heads(la_f, B_HEADS), _heads(la_b, B_HEADS)
    if ctx is None:
        a_o = _diff_attn(q1, q2, k1, k2, v, lam)
        zero = jnp.zeros((bsz, B_HEADS, B_DK, B_DV), F32)
        b_o, s_f, s_b = _gla_bidir(bq, bk, bv, la_f, la_b, zero, zero)
        extra = (jnp.concatenate([k1, k2], axis=-1), v, s_f, s_b)
    else:
        ctx_k, ctx_v, s_f, s_b = ctx
        q1, q2, k1, k2 = (_axial_rope(t, ang) for t in (q1, q2, k1, k2))
        ck1, ck2 = ctx_k[..., :A_DH], ctx_k[..., A_DH:]
        a_o = _diff_attn(q1, q2, jnp.concatenate([ck1, k1], axis=2), jnp.concatenate([ck2, k2], axis=2),
                         jnp.concatenate([ctx_v, v], axis=2), lam)
        b_o, _, _ = _gla_bidir(bq, bk, bv, la_f, la_b, s_f, s_b)
        extra = None
    a_out = _merge(_rms(a_o) * subln_g * (1.0 - lam_init))
    b_out = _merge(_rms(b_o) * bnorm_g) * jax.nn.silu(b_r)
    return jnp.concatenate([a_out, b_out], axis=-1), extra


def _c_core(proj, ctx, ang, sink):
    q = _heads(proj[..., :C_HEADS * C_DH], C_HEADS)
    k = _heads(proj[..., C_HEADS * C_DH:(C_HEADS + C_KV_HEADS) * C_DH], C_KV_HEADS)
    v = _heads(proj[..., (C_HEADS + C_KV_HEADS) * C_DH:], C_KV_HEADS)
    b, hq, n, dh = q.shape
    g = hq // C_KV_HEADS
    scale = dh ** -0.5
    sink_g = sink.reshape(1, C_KV_HEADS, g, 1, 1)
    if ctx is None:
        qg = q.reshape(b, C_KV_HEADS, g, n, dh)
        s = jnp.einsum('bhgqd,bhkd->bhgqk', qg, k) * scale
        sk = jnp.broadcast_to(sink_g, s.shape[:-1] + (1,))
        p = jax.nn.softmax(jnp.concatenate([sk, s], axis=-1), axis=-1)[..., 1:]
        o = jnp.einsum('bhgqk,bhkd->bhgqd', p, v).reshape(b, hq, n, dh)
        return _merge(o), (k, v)
    kc, vc = ctx
    q, k = _axial_rope(q, ang), _axial_rope(k, ang)
    qg = q.reshape(b, C_KV_HEADS, g, n, dh)
    s_ctx = jnp.einsum('bhgqd,bhkd->bhgqk', qg, kc) * scale
    s_lat = jnp.einsum('bhgqd,bhkd->bhgqk', qg, k) * scale
    pos = jnp.arange(n)
    valid = jnp.abs(pos[:, None] - pos[None, :]) <= C_WINDOW
    s_lat = jnp.where(valid, s_lat, -jnp.inf)
    sk = jnp.broadcast_to(sink_g, s_ctx.shape[:-1] + (1,))
    p = jax.nn.softmax(jnp.concatenate([sk, s_ctx, s_lat], axis=-1), axis=-1)
    nc = kc.shape[2]
    o = (jnp.einsum('bhgqk,bhkd->bhgqd', p[..., 1:1 + nc], vc)
         + jnp.einsum('bhgqk,bhkd->bhgqd', p[..., 1 + nc:], v))
    return _merge(o.reshape(b, hq, n, dh)), None


def kernel(x_prompt, x_sample, cache_a_k, cache_a_v, state_b_fwd, state_b_bwd, cache_c_k, cache_c_v, c, c_ctx, ada_w, ada_b, norm_g, ffn_w_in, ffn_w_out, ab_w_in, ab_w_out, a_lambda, a_subln_g, b_alpha_w, b_alpha_b, b_norm_g, c_w_in, c_w_out, c_sink, final_g):
    bp, sp, d = x_prompt.shape
    bs, ss, _ = x_sample.shape
    depth = ada_w.shape[0]
    mp, ms = bp * sp, bs * ss
    rows_info = (mp, ss)

    x = jnp.concatenate([x_prompt.reshape(mp, d), x_sample.reshape(ms, d)], axis=0)
    cond = jnp.concatenate([c_ctx[None, :], c, jnp.zeros((COND_ROWS - 1 - bs, d), F32)], axis=0)
    mods = adaln(cond, ada_w, ada_b).reshape(depth, COND_ROWS, N_MOD, d)
    ang_a = _axial_angles(ss, A_DH)
    ang_c = _axial_angles(ss, C_DH)

    outs = {}
    for l in range(depth):
        mod = mods[l]
        last = l == depth - 1
        x = ffn_half_step(x, mod[:, 0:3], norm_g[l, 0], ffn_w_in[l, 0], ffn_w_out[l, 0], rows_info)
        if l % 2 == 0:
            e = l // 2
            lam_init = 0.8 - 0.6 * math.exp(-0.3 * l)
            lp = a_lambda[e]
            lam = jnp.exp(jnp.sum(lp[0] * lp[1])) - jnp.exp(jnp.sum(lp[2] * lp[3])) + lam_init
            proj = mixer_in_proj(x, mod[:, 3:5], norm_g[l, 1], ab_w_in[e], AB_MAIN, rows_info)
            h_all = None
            low_w = ab_w_in[e][:, AB_MAIN:]
            low = _low_proj(x, mod[:, 3:5], norm_g[l, 1], low_w, rows_info)
            args = (lam, lam_init, a_subln_g[e], b_alpha_w[e], b_alpha_b[e], b_norm_g[e])
            mix_p, (ak, av, sf, sb) = _ab_core(proj[:mp].reshape(bp, sp, -1), low[:mp].reshape(bp, sp, -1),
                                               None, None, *args)
            ctx = (cache_a_k[:, e], cache_a_v[:, e], state_b_fwd[:, e], state_b_bwd[:, e])
            mix_s, _ = _ab_core(proj[mp:].reshape(bs, ss, -1), low[mp:].reshape(bs, ss, -1), ctx, ang_a, *args)
            outs.setdefault('a_k', []).append(ak)
            outs.setdefault('a_v', []).append(av)
            outs.setdefault('b_f', []).append(sf)
            outs.setdefault('b_b', []).append(sb)
            w_out = ab_w_out[e]
        else:
            o = l // 2
            n_cols = c_w_in.shape[2]
            proj = mixer_in_proj(x, mod[:, 3:5], norm_g[l, 1], c_w_in[o], n_cols, rows_info)
            mix_p, (ck, cv) = _c_core(proj[:mp].reshape(bp, sp, -1), None, None, c_sink[o])
            mix_s, _ = _c_core(proj[mp:].reshape(bs, ss, -1), (cache_c_k[:, o], cache_c_v[:, o]), ang_c, c_sink[o])
            outs.setdefault('c_k', []).append(ck)
            outs.setdefault('c_v', []).append(cv)
            w_out = c_w_out[o]
        mix = jnp.concatenate([mix_p.reshape(mp, -1), mix_s.reshape(ms, -1)], axis=0)
        x = mixer_out_proj(mix, w_out, x, mod[:, 5:6], rows_info)
        x = ffn_half_step(x, mod[:, 6:9], norm_g[l, 2], ffn_w_in[l, 1], ffn_w_out[l, 1], rows_info,
                          final_g=final_g if last else None)

    y_prompt = x[:mp].reshape(bp, sp, d)
    y_sample = x[mp:].reshape(bs, ss, d)
    stack = lambda name: jnp.stack(outs[name], axis=1)
    return (y_prompt, y_sample, stack('a_k'), stack('a_v'), stack('b_f'), stack('b_b'), stack('c_k'), stack('c_v'))


def _low_proj(x, mod2, g, low_w, rows_info):
    n_prompt_rows, lat_rows = rows_info
    m = x.shape[0]
    row = jnp.maximum((jnp.arange(m) - n_prompt_rows) // lat_rows + 1, 0)
    h = (_rms(x) * g) * (1.0 + mod2[row, 1]) + mod2[row, 0]
    return _bdot(h, low_w)
```

```python
import functools
import math

import numpy as np
import jax
import jax.numpy as jnp
from jax import lax
from jax.experimental import pallas as pl
from jax.experimental.pallas import tpu as pltpu

F32 = jnp.float32
BF16 = jnp.bfloat16

EPS = 1e-6
GRID_W = 64
ROPE_BASE = 10000.0
N_MOD = 9
A_HEADS, A_DH, A_DV = 8, 64, 128
B_HEADS, B_DK, B_DV, B_RANK, B_TAU = 4, 128, 256, 16, 16.0
C_HEADS, C_KV_HEADS, C_DH, C_WINDOW = 16, 4, 128, 128
A_QK_W = A_HEADS * 2 * A_DH
A_V_W = A_HEADS * A_DV
B_QK_W = B_HEADS * B_DK
B_V_W = B_HEADS * B_DV
AB_MAIN = 2 * A_QK_W + A_V_W + 2 * B_QK_W + 2 * B_V_W
C_GROUP = C_HEADS // C_KV_HEADS
C_Q_W = C_HEADS * C_DH
C_KV_W = C_KV_HEADS * C_DH

V7X_VMEM_BYTES = 64 * 1024 * 1024
VMEM_LIMIT_BYTES = V7X_VMEM_BYTES - 8 * 1024 * 1024
COND_ROWS = 8
ROW_CHUNK = 128
GATE_ROWS = 256


def _tile(n, pref, align=128):
    if n <= pref:
        return n
    t = (pref // align) * align
    while n % t:
        t -= align
    assert t > 0, (n, pref)
    return t


def _cparams(*sem):
    return pltpu.CompilerParams(dimension_semantics=sem, vmem_limit_bytes=VMEM_LIMIT_BYTES)


def _bdot(a, b):
    return jnp.dot(a.astype(BF16), b.astype(BF16), preferred_element_type=F32)


def _dot_nt(a, b):
    return lax.dot_general(a, b, (((1,), (1,)), ((), ())), preferred_element_type=F32)


def _dot_tn(a, b):
    return lax.dot_general(a, b, (((0,), (0,)), ((), ())), preferred_element_type=F32)


def _rms(x):
    return x * lax.rsqrt(jnp.mean(x * x, axis=-1, keepdims=True) + EPS)


def _adaln_kernel(cond_ref, w_ref, b_ref, o_ref):
    c = cond_ref[...]
    o_ref[0] = _bdot(c * jax.nn.sigmoid(c), w_ref[0]) + b_ref[0]


def adaln(cond, ada_w, ada_b):
    depth, d, n = ada_w.shape
    tn = _tile(n, 1024)
    return pl.pallas_call(
        _adaln_kernel,
        grid=(depth, n // tn),
        in_specs=[pl.BlockSpec((COND_ROWS, d), lambda l, j: (0, 0)),
                  pl.BlockSpec((1, d, tn), lambda l, j: (l, 0, j)),
                  pl.BlockSpec((1, 1, tn), lambda l, j: (l, 0, j))],
        out_specs=pl.BlockSpec((1, COND_ROWS, tn), lambda l, j: (l, 0, j)),
        out_shape=jax.ShapeDtypeStruct((depth, COND_ROWS, n), F32),
        compiler_params=_cparams("arbitrary", "arbitrary"),
        name="adaln",
    )(cond, ada_w, ada_b.reshape(depth, 1, n))


def _norm_modulate(x_ref, g_ref, mod_ref, h_ref):
    g = g_ref[...]
    shift = mod_ref[0, 0:1, :]
    scale1 = 1.0 + mod_ref[0, 1:2, :]

    def body(r, carry):
        rows = pl.ds(pl.multiple_of(r * ROW_CHUNK, ROW_CHUNK), ROW_CHUNK)
        h_ref[rows, :] = ((_rms(x_ref[rows, :]) * g) * scale1 + shift).astype(BF16)
        return carry

    lax.fori_loop(0, x_ref.shape[0] // ROW_CHUNK, body, 0)


def _row_tile(rows_info):
    n_prompt_rows, lat_rows = rows_info
    return _tile(math.gcd(n_prompt_rows, lat_rows), 1024, align=ROW_CHUNK)


def _cond_row(i, tm, n_prompt_rows, lat_rows):
    return jnp.maximum((i * tm - n_prompt_rows) // lat_rows + 1, 0)


def _ffn_kernel(x_ref, mod_ref, g_ref, wg_ref, wu_ref, wo_ref, *rest, n_chunk, final):
    if final:
        fg_ref, o_ref, h_ref = rest
    else:
        o_ref, h_ref = rest
    j = pl.program_id(1)

    @pl.when(j == 0)
    def _():
        _norm_modulate(x_ref, g_ref, mod_ref, h_ref)

    h = h_ref[...]
    gate = jnp.dot(h, wg_ref[...].astype(BF16), preferred_element_type=F32)
    up = jnp.dot(h, wu_ref[...].astype(BF16), preferred_element_type=F32)
    act = (gate * jax.nn.sigmoid(gate) * up).astype(BF16)
    d = o_ref.shape[1]
    for c in range(d // n_chunk):
        cols = slice(c * n_chunk, (c + 1) * n_chunk)
        part = jnp.dot(act, wo_ref[:, cols].astype(BF16), preferred_element_type=F32)

        @pl.when(j == 0)
        def _():
            o_ref[:, cols] = part

        @pl.when(j > 0)
        def _():
            o_ref[:, cols] += part

    @pl.when(j == pl.num_programs(1) - 1)
    def _():
        half_gate = 0.5 * mod_ref[0, 2:3, :]

        def body(r, carry):
            rows = pl.ds(pl.multiple_of(r * ROW_CHUNK, ROW_CHUNK), ROW_CHUNK)
            xn = x_ref[rows, :] + half_gate * o_ref[rows, :]
            if final:
                xn = _rms(xn) * fg_ref[...]
            o_ref[rows, :] = xn
            return carry

        lax.fori_loop(0, x_ref.shape[0] // ROW_CHUNK, body, 0)


def ffn_half_step(x, mod3, g, w_in, w_out, rows_info, final_g=None):
    m, d = x.shape
    f = w_out.shape[0]
    n_prompt_rows, lat_rows = rows_info
    tm, tf, n_chunk = _row_tile(rows_info), _tile(f, 256), _tile(d, 512)
    nf = f // tf
    final = final_g is not None
    row = lambda i, j: (_cond_row(i, tm, n_prompt_rows, lat_rows), 0, 0)
    in_specs = [pl.BlockSpec((tm, d), lambda i, j: (i, 0)),
                pl.BlockSpec((1, 3, d), row),
                pl.BlockSpec((1, d), lambda i, j: (0, 0)),
                pl.BlockSpec((d, tf), lambda i, j: (0, j)),
                pl.BlockSpec((d, tf), lambda i, j: (0, nf + j)),
                pl.BlockSpec((tf, d), lambda i, j: (j, 0))]
    args = [x, mod3, g.reshape(1, d), w_in, w_in, w_out]
    if final:
        in_specs.append(pl.BlockSpec((1, d), lambda i, j: (0, 0)))
        args.append(final_g.reshape(1, d))
    return pl.pallas_call(
        functools.partial(_ffn_kernel, n_chunk=n_chunk, final=final),
        grid=(m // tm, nf),
        in_specs=in_specs,
        out_specs=pl.BlockSpec((tm, d), lambda i, j: (i, 0)),
        out_shape=jax.ShapeDtypeStruct((m, d), F32),
        scratch_shapes=[pltpu.VMEM((tm, d), BF16)],
        compiler_params=_cparams("arbitrary", "arbitrary"),
        name="ffn_final" if final else "ffn",
    )(*args)


def _log_sigmoid(z):
    return jnp.minimum(z, 0.0) - jnp.log(1.0 + jnp.exp(-jnp.abs(z)))


def _proj_kernel(*refs, n_prompt_tiles, n_rope_tiles, rope_shift, with_gates):
    x_ref, mod_ref, g_ref, w_ref, cos_ref, sa_ref, sb_ref = refs[:7]
    if with_gates:
        wl_ref, wa_ref, ba_ref, o_ref, gates_ref, h_ref = refs[7:]
    else:
        o_ref, h_ref = refs[7:]
    i, j = pl.program_id(0), pl.program_id(1)

    @pl.when(j == 0)
    def _():
        _norm_modulate(x_ref, g_ref, mod_ref, h_ref)
        if with_gates:
            wl = wl_ref[...].astype(BF16)
            wa = wa_ref[...].astype(BF16)

            def body(r, carry):
                rows = pl.ds(pl.multiple_of(r * GATE_ROWS, GATE_ROWS), GATE_ROWS)
                low = jnp.dot(h_ref[rows, :], wl, preferred_element_type=F32)
                z = jnp.dot(low.astype(BF16), wa, preferred_element_type=F32) + ba_ref[...]
                gates_ref[rows, :] = _log_sigmoid(z) * (1.0 / B_TAU)
                return carry

            lax.fori_loop(0, x_ref.shape[0] // GATE_ROWS, body, 0)

    acc = jnp.dot(h_ref[...], w_ref[...].astype(BF16), preferred_element_type=F32)
    rope = jnp.logical_and(i >= n_prompt_tiles, j < n_rope_tiles)

    @pl.when(rope)
    def _():
        cos, sa, sb = cos_ref[...], sa_ref[...], sb_ref[...]
        for c in range(acc.shape[1] // 128):
            a = acc[:, c * 128:(c + 1) * 128]
            o_ref[:, c * 128:(c + 1) * 128] = (a * cos + pltpu.roll(a, 128 - rope_shift, 1) * sa
                                               + pltpu.roll(a, rope_shift, 1) * sb)

    @pl.when(jnp.logical_not(rope))
    def _():
        o_ref[...] = acc


def _rope_tables(n, head_dim, n_sub):
    assert head_dim * n_sub == 128
    rows = n // GRID_W
    row = jnp.repeat(jnp.arange(rows, dtype=F32), GRID_W)
    col = jnp.tile(jnp.arange(GRID_W, dtype=F32), rows)
    d_axis = head_dim // 2
    shift = d_axis // 2
    inv = ROPE_BASE ** (-jnp.arange(0, d_axis, 2, dtype=F32) / d_axis)
    lane = jnp.arange(128)
    sub = lane % head_dim
    is_col = (sub // d_axis) == 1
    within = sub % d_axis
    freq = within % shift
    second = (within // shift) == 1
    pos = jnp.where(is_col[None, :], col[:, None], row[:, None])
    ang = pos * inv[freq][None, :]
    cos, sin = jnp.cos(ang), jnp.sin(ang)
    sin_a = jnp.where(second[None, :], 0.0, -sin)
    sin_b = jnp.where(second[None, :], sin, 0.0)
    return cos, sin_a, sin_b, shift


def mixer_in_proj(x, mod2, g, w, n_cols, rows_info, rope, gates=None):
    m, d = x.shape
    n_prompt_rows, lat_rows = rows_info
    tm, tn = _row_tile(rows_info), _tile(n_cols, 512)
    cos, sin_a, sin_b, rope_shift, n_rope_cols = rope
    assert n_rope_cols % tn == 0 and n_prompt_rows % tm == 0
    npt, lat_tiles = n_prompt_rows // tm, lat_rows // tm
    cond = lambda i, j: (_cond_row(i, tm, n_prompt_rows, lat_rows), 0, 0)
    tab = lambda i, j: (jnp.maximum(i - npt, 0) % lat_tiles, 0)
    in_specs = [pl.BlockSpec((tm, d), lambda i, j: (i, 0)),
                pl.BlockSpec((1, 2, d), cond),
                pl.BlockSpec((1, d), lambda i, j: (0, 0)),
                pl.BlockSpec((d, tn), lambda i, j: (0, j)),
                pl.BlockSpec((tm, 128), tab), pl.BlockSpec((tm, 128), tab), pl.BlockSpec((tm, 128), tab)]
    args = [x, mod2, g.reshape(1, d), w, cos, sin_a, sin_b]
    out_specs = pl.BlockSpec((tm, tn), lambda i, j: (i, j))
    out_shape = jax.ShapeDtypeStruct((m, n_cols), F32)
    if gates is not None:
        w_low, w_alpha, b_alpha = gates
        n_gate = w_alpha.shape[1]
        in_specs += [pl.BlockSpec(w_low.shape, lambda i, j: (0, 0)),
                     pl.BlockSpec(w_alpha.shape, lambda i, j: (0, 0)),
                     pl.BlockSpec((1, n_gate), lambda i, j: (0, 0))]
        args += [w_low, w_alpha, b_alpha.reshape(1, n_gate)]
        out_specs = [out_specs, pl.BlockSpec((tm, n_gate), lambda i, j: (i, 0))]
        out_shape = [out_shape, jax.ShapeDtypeStruct((m, n_gate), F32)]
    return pl.pallas_call(
        functools.partial(_proj_kernel, n_prompt_tiles=npt, n_rope_tiles=n_rope_cols // tn,
                          rope_shift=rope_shift, with_gates=gates is not None),
        grid=(m // tm, n_cols // tn),
        in_specs=in_specs, out_specs=out_specs, out_shape=out_shape,
        scratch_shapes=[pltpu.VMEM((tm, d), BF16)],
        compiler_params=_cparams("arbitrary", "arbitrary"),
        name="mixer_in_proj_gated" if gates is not None else "mixer_in_proj",
    )(*args)


def _out_proj_kernel(*refs, n_parts, n_prompt_tiles):
    parts_p, parts_s = refs[:n_parts], refs[n_parts:2 * n_parts]
    w_refs = refs[2 * n_parts:3 * n_parts]
    x_ref, gate_ref, o_ref = refs[3 * n_parts:]
    i = pl.program_id(0)

    def run(parts):
        acc = None
        for p_ref, w_ref in zip(parts, w_refs):
            t = jnp.dot(p_ref[...], w_ref[...].astype(BF16), preferred_element_type=F32)
            acc = t if acc is None else acc + t
        o_ref[...] = x_ref[...] + gate_ref[0] * acc

    pl.when(i < n_prompt_tiles)(lambda: run(parts_p))
    pl.when(i >= n_prompt_tiles)(lambda: run(parts_s))


def mixer_out_proj(parts_p, parts_s, w, x, gate, rows_info):
    m, d = x.shape
    n_prompt_rows, lat_rows = rows_info
    tm, tn = _row_tile(rows_info), _tile(d, 512)
    npt = n_prompt_rows // tm
    n_parts = len(parts_p)
    in_specs, w_specs, off = [], [], 0
    for p in parts_p:
        in_specs.append(pl.BlockSpec((tm, p.shape[1]), lambda i, j: (jnp.minimum(i, npt - 1), 0)))
    for p in parts_s:
        kp = p.shape[1]
        in_specs.append(pl.BlockSpec((tm, kp), lambda i, j: (jnp.maximum(i - npt, 0), 0)))
        assert off % kp == 0
        w_specs.append(pl.BlockSpec((kp, tn), functools.partial(lambda i, j, rb: (rb, j), rb=off // kp)))
        off += kp
    in_specs += w_specs
    in_specs += [pl.BlockSpec((tm, tn), lambda i, j: (i, j)),
                 pl.BlockSpec((1, 1, tn), lambda i, j: (_cond_row(i, tm, n_prompt_rows, lat_rows), 0, j))]
    return pl.pallas_call(
        functools.partial(_out_proj_kernel, n_parts=n_parts, n_prompt_tiles=npt),
        grid=(m // tm, d // tn),
        in_specs=in_specs,
        out_specs=pl.BlockSpec((tm, tn), lambda i, j: (i, j)),
        out_shape=jax.ShapeDtypeStruct((m, d), F32),
        compiler_params=_cparams("arbitrary", "arbitrary"),
        name="mixer_out_proj",
    )(*parts_p, *parts_s, *([w] * n_parts), x, gate)


def _softmax_parts(parts, extra_logit=None):
    m = functools.reduce(jnp.maximum, [jnp.max(p, axis=-1, keepdims=True) for p in parts])
    if extra_logit is not None:
        m = jnp.maximum(m, extra_logit)
    es = [jnp.exp(p - m) for p in parts]
    den = functools.reduce(jnp.add, [jnp.sum(e, axis=-1, keepdims=True) for e in es])
    if extra_logit is not None:
        den = den + jnp.exp(extra_logit - m)
    inv = 1.0 / den
    return [e * inv for e in es]


def _diff_attn_kernel(*refs, hb, lam_init, has_ctx, emit_kv):
    it = iter(refs)
    lam_ref, g_ref, q_ref, k_ref, v_ref = (next(it) for _ in range(5))
    ck_ref, cv_ref = (next(it), next(it)) if has_ctx else (None, None)
    o_ref = next(it)
    nk_ref, nv_ref = (next(it), next(it)) if emit_kv else (None, None)

    lp = lam_ref[...]
    lam = (jnp.exp(jnp.sum(lp[0:1] * lp[1:2], axis=-1, keepdims=True))
           - jnp.exp(jnp.sum(lp[2:3] * lp[3:4], axis=-1, keepdims=True)) + lam_init)
    scale = A_DH ** -0.5
    first = lax.broadcasted_iota(jnp.int32, (1, 2 * A_DH), 1) < A_DH
    for hh in range(hb):
        cols = slice(hh * 128, (hh + 1) * 128)
        q, k, v = q_ref[:, cols], k_ref[:, cols], v_ref[:, cols]
        q1 = jnp.where(first, q, 0.0).astype(BF16)
        q2 = jnp.where(first, 0.0, q).astype(BF16)
        keys, vals = [k.astype(BF16)], [v.astype(BF16)]
        if has_ctx:
            keys.insert(0, ck_ref[0, 0, hh].astype(BF16))
            vals.insert(0, cv_ref[0, 0, hh].astype(BF16))
        p1 = _softmax_parts([_dot_nt(q1, kk) * scale for kk in keys])
        p2 = _softmax_parts([_dot_nt(q2, kk) * scale for kk in keys])
        o = None
        for a, b, vv in zip(p1, p2, vals):
            t = jnp.dot((a - lam * b).astype(BF16), vv, preferred_element_type=F32)
            o = t if o is None else o + t
        o_ref[:, cols] = ((_rms(o) * g_ref[...]) * (1.0 - lam_init)).astype(BF16)
        if emit_kv:
            nk_ref[0, 0, hh] = k
            nv_ref[0, 0, hh] = v


def diff_attention(proj, lam_p, subln_g, lam_init, *, row_off, batch, seq, tq, hb, ctx=None, emit_kv=False):
    width = hb * 128
    n_hg = A_HEADS // hb
    assert row_off % seq == 0 and seq % tq == 0
    qrow = lambda b, hg, i: ((row_off + b * seq) // tq + i, hg)
    krow = lambda b, hg, i: ((row_off + b * seq) // seq, A_QK_W // width + hg)
    vrow = lambda b, hg, i: ((row_off + b * seq) // seq, 2 * A_QK_W // width + hg)
    in_specs = [pl.BlockSpec(lam_p.shape, lambda b, hg, i: (0, 0)),
                pl.BlockSpec((1, A_DV), lambda b, hg, i: (0, 0)),
                pl.BlockSpec((tq, width), qrow),
                pl.BlockSpec((seq, width), krow),
                pl.BlockSpec((seq, width), vrow)]
    args = [lam_p, subln_g.reshape(1, A_DV), proj, proj, proj]
    if ctx is not None:
        ctx_k, ctx_v, e = ctx
        past = ctx_k.shape[3]
        cspec = lambda b, hg, i: (b, e, hg, 0, 0)
        in_specs += [pl.BlockSpec((1, 1, hb, past, 2 * A_DH), cspec), pl.BlockSpec((1, 1, hb, past, A_DV), cspec)]
        args += [ctx_k, ctx_v]
    out_specs = [pl.BlockSpec((tq, width), lambda b, hg, i: (b * (seq // tq) + i, hg))]
    out_shape = [jax.ShapeDtypeStruct((batch * seq, A_V_W), BF16)]
    if emit_kv:
        assert tq == seq
        kvspec = pl.BlockSpec((1, 1, hb, seq, 128), lambda b, hg, i: (b, 0, hg, 0, 0))
        out_specs += [kvspec, kvspec]
        out_shape += [jax.ShapeDtypeStruct((batch, 1, A_HEADS, seq, 128), F32)] * 2
    return pl.pallas_call(
        functools.partial(_diff_attn_kernel, hb=hb, lam_init=lam_init, has_ctx=ctx is not None, emit_kv=emit_kv),
        grid=(batch, n_hg, seq // tq),
        in_specs=in_specs, out_specs=out_specs, out_shape=out_shape,
        compiler_params=_cparams("arbitrary", "arbitrary", "arbitrary"),
        name="diff_attn_ctx" if ctx is not None else "diff_attn",
    )(*args)


def _gla_level_matrices(chunk, fwd):
    t = np.arange(chunk)[:, None]
    i = np.arange(chunk)[None, :]
    mats = [i <= t, i > t] if fwd else [i >= t, i < t]
    h = chunk // 2
    while h >= 1:
        p = t % (2 * h)
        base = t - p
        if fwd:
            m = base + h - 1
            a = np.where(p >= h, (i > m) & (i <= t), (i > t) & (i <= m))
        else:
            m = base + h
            a = np.where(p < h, (i >= t) & (i < m), (i >= m) & (i < t))
        mats.append(a)
        h //= 2
    return jnp.asarray(np.concatenate(mats, axis=0).astype(np.float32), dtype=BF16)


def _gla_kernel(*refs, chunk, has_state, emit_state):
    it = iter(refs)
    af_ref, ab_ref, q_ref, k_ref, v_ref, laf_ref, lab_ref, r_ref, g_ref = (next(it) for _ in range(9))
    sf_ref, sb_ref = (next(it), next(it)) if has_state else (None, None)
    o_ref = next(it)
    nsf_ref, nsb_ref = (next(it), next(it)) if emit_state else (None, None)
    acc_ref, st_ref = next(it), next(it)

    seq = q_ref.shape[0]
    n_chunks = seq // chunk
    n_lev = chunk.bit_length() - 1
    ti = lax.broadcasted_iota(jnp.int32, (chunk, chunk), 0)
    si = lax.broadcasted_iota(jnp.int32, (chunk, chunk), 1)
    split = ti ^ si
    scale = B_DK ** -0.5

    for fwd, a_ref, la_ref, s0_ref, ns_ref in ((True, af_ref, laf_ref, sf_ref, nsf_ref),
                                                (False, ab_ref, lab_ref, sb_ref, nsb_ref)):
        order = (ti > si) if fwd else (ti < si)
        st_ref[...] = s0_ref[0, 0, 0].T if has_state else jnp.zeros(st_ref.shape, F32)

        def body(cc, carry, fwd=fwd, a_ref=a_ref, la_ref=la_ref, order=order):
            c = cc if fwd else n_chunks - 1 - cc
            rows = pl.ds(pl.multiple_of(c * chunk, chunk), chunk)
            q = q_ref[rows, :] * scale
            k = k_ref[rows, :]
            vb = v_ref[rows, :].astype(BF16)
            la = la_ref[rows, :]
            la_hi = la.astype(BF16)
            la_lo = (la - la_hi.astype(F32)).astype(BF16)
            r2 = jnp.dot(a_ref[...], jnp.concatenate([la_hi, la_lo], axis=1), preferred_element_type=F32)
            e = jnp.exp(r2[:, :B_DK] + r2[:, B_DK:])
            st = st_ref[...]
            inter = _dot_nt((q * e[0:chunk]).astype(BF16), st.astype(BF16))
            ku = (k * e[chunk:2 * chunk]).astype(BF16)
            scores = jnp.where(ti == si, _dot_nt(q.astype(BF16), k.astype(BF16)), 0.0)
            for lev in range(n_lev):
                half = chunk >> (lev + 1)
                f = e[(2 + lev) * chunk:(3 + lev) * chunk]
                sc = _dot_nt((q * f).astype(BF16), (k * f).astype(BF16))
                scores = jnp.where(order & (split >= half) & (split < 2 * half), sc, scores)
            o = inter + jnp.dot(scores.astype(BF16), vb, preferred_element_type=F32)
            if fwd:
                acc_ref[rows, :] = o
                total = e[chunk - 1:chunk]
            else:
                acc_ref[rows, :] += o
                total = e[0:1]
            st_ref[...] = st * total + _dot_tn(vb, ku)
            return carry

        lax.fori_loop(0, n_chunks, body, 0)
        if emit_state:
            ns_ref[0, 0, 0] = st_ref[...].T

    def epilogue(r, carry):
        rows = pl.ds(pl.multiple_of(r * ROW_CHUNK, ROW_CHUNK), ROW_CHUNK)
        gate = r_ref[rows, :]
        o_ref[rows, :] = ((_rms(acc_ref[rows, :]) * g_ref[...]) * (gate * jax.nn.sigmoid(gate))).astype(BF16)
        return carry

    lax.fori_loop(0, seq // ROW_CHUNK, epilogue, 0)


def gla_bidirectional(proj, gates, bnorm_g, *, row_off, batch, seq, states=None, emit_state=False):
    chunk = min(256, seq)
    assert row_off % seq == 0 and seq % chunk == 0 and chunk & (chunk - 1) == 0
    rb = lambda b: (row_off + b * seq) // seq
    q_off = (2 * A_QK_W + A_V_W) // B_DK
    k_off = q_off + B_HEADS
    v_off = (2 * A_QK_W + A_V_W + 2 * B_QK_W) // B_DV
    r_off = v_off + B_HEADS
    a_f, a_b = _gla_level_matrices(chunk, True), _gla_level_matrices(chunk, False)
    whole = lambda b, h: (0, 0)
    in_specs = [pl.BlockSpec(a_f.shape, whole), pl.BlockSpec(a_b.shape, whole),
                pl.BlockSpec((seq, B_DK), lambda b, h: (rb(b), q_off + h)),
                pl.BlockSpec((seq, B_DK), lambda b, h: (rb(b), k_off + h)),
                pl.BlockSpec((seq, B_DV), lambda b, h: (rb(b), v_off + h)),
                pl.BlockSpec((seq, B_DK), lambda b, h: (rb(b), h)),
                pl.BlockSpec((seq, B_DK), lambda b, h: (rb(b), B_HEADS + h)),
                pl.BlockSpec((seq, B_DV), lambda b, h: (rb(b), r_off + h)),
                pl.BlockSpec((1, B_DV), whole)]
    args = [a_f, a_b, proj, proj, proj, gates, gates, proj, bnorm_g.reshape(1, B_DV)]
    if states is not None:
        s_f, s_b, e = states
        sspec = pl.BlockSpec((1, 1, 1, B_DK, B_DV), lambda b, h: (b, e, h, 0, 0))
        in_specs += [sspec, sspec]
        args += [s_f, s_b]
    out_specs = [pl.BlockSpec((seq, B_DV), lambda b, h: (b, h))]
    out_shape = [jax.ShapeDtypeStruct((batch * seq, B_V_W), BF16)]
    if emit_state:
        nspec = pl.BlockSpec((1, 1, 1, B_DK, B_DV), lambda b, h: (b, 0, h, 0, 0))
        out_specs += [nspec, nspec]
        out_shape += [jax.ShapeDtypeStruct((batch, 1, B_HEADS, B_DK, B_DV), F32)] * 2
    return pl.pallas_call(
        functools.partial(_gla_kernel, chunk=chunk, has_state=states is not None, emit_state=emit_state),
        grid=(batch, B_HEADS),
        in_specs=in_specs, out_specs=out_specs, out_shape=out_shape,
        scratch_shapes=[pltpu.VMEM((seq, B_DV), F32), pltpu.VMEM((B_DV, B_DK), F32)],
        compiler_params=_cparams("arbitrary", "arbitrary"),
        name="gla_state" if states is not None else "gla",
    )(*args)


def _sink_attn_kernel(sink_ref, q_ref, k_ref, v_ref, o_ref, nk_ref, nv_ref):
    seq = q_ref.shape[0]
    scale = C_DH ** -0.5
    for hk in range(C_KV_HEADS):
        kcols = slice(hk * C_DH, (hk + 1) * C_DH)
        k, v = k_ref[:, kcols], v_ref[:, kcols]
        q4 = jnp.concatenate([q_ref[:, (hk * C_GROUP + g) * C_DH:(hk * C_GROUP + g + 1) * C_DH]
                              for g in range(C_GROUP)], axis=0).astype(BF16)
        sink = jnp.concatenate([jnp.broadcast_to(sink_ref[hk, 0:1, g:g + 1], (seq, 1)) for g in range(C_GROUP)],
                               axis=0)
        (p,) = _softmax_parts([_dot_nt(q4, k.astype(BF16)) * scale], extra_logit=sink)
        o = jnp.dot(p.astype(BF16), v.astype(BF16), preferred_element_type=F32)
        for g in range(C_GROUP):
            h = hk * C_GROUP + g
            o_ref[:, h * C_DH:(h + 1) * C_DH] = o[g * seq:(g + 1) * seq].astype(BF16)
        nk_ref[0, 0, hk] = k
        nv_ref[0, 0, hk] = v


def sink_attention_context(proj, sink, *, batch, seq):
    sink3 = sink.reshape(C_KV_HEADS, 1, C_GROUP)
    kvspec = pl.BlockSpec((1, 1, C_KV_HEADS, seq, C_DH), lambda b: (b, 0, 0, 0, 0))
    return pl.pallas_call(
        _sink_attn_kernel,
        grid=(batch,),
        in_specs=[pl.BlockSpec(sink3.shape, lambda b: (0, 0, 0)),
                  pl.BlockSpec((seq, C_Q_W), lambda b: (b, 0)),
                  pl.BlockSpec((seq, C_KV_W), lambda b: (b, C_Q_W // C_KV_W)),
                  pl.BlockSpec((seq, C_KV_W), lambda b: (b, C_Q_W // C_KV_W + 1))],
        out_specs=[pl.BlockSpec((seq, C_Q_W), lambda b: (b, 0)), kvspec, kvspec],
        out_shape=[jax.ShapeDtypeStruct((batch * seq, C_Q_W), BF16)]
        + [jax.ShapeDtypeStruct((batch, 1, C_KV_HEADS, seq, C_DH), F32)] * 2,
        compiler_params=_cparams("arbitrary"),
        name="sink_attn",
    )(sink3, proj, proj, proj)


def _window_attn_kernel(sink_ref, q_ref, k_ref, v_ref, ck_ref, cv_ref, o_ref, *, band):
    tq = q_ref.shape[0]
    seq = k_ref.shape[0]
    scale = C_DH ** -0.5
    i = pl.program_id(2)
    start = pl.multiple_of(jnp.clip(i * tq - C_WINDOW, 0, seq - band), C_WINDOW)
    kb = k_ref[pl.ds(start, band), :].astype(BF16)
    vb = v_ref[pl.ds(start, band), :].astype(BF16)
    q4 = jnp.concatenate([q_ref[:, g * C_DH:(g + 1) * C_DH] for g in range(C_GROUP)], axis=0).astype(BF16)
    sink = jnp.concatenate([jnp.broadcast_to(sink_ref[0, 0:1, g:g + 1], (tq, 1)) for g in range(C_GROUP)], axis=0)
    qpos = i * tq + lax.broadcasted_iota(jnp.int32, (tq, band), 0)
    kpos = start + lax.broadcasted_iota(jnp.int32, (tq, band), 1)
    valid = jnp.abs(qpos - kpos) <= C_WINDOW
    valid4 = jnp.concatenate([valid] * C_GROUP, axis=0)
    s_band = jnp.where(valid4, _dot_nt(q4, kb) * scale, -jnp.inf)
    s_ctx = _dot_nt(q4, ck_ref[0, 0, 0].astype(BF16)) * scale
    p_ctx, p_band = _softmax_parts([s_ctx, s_band], extra_logit=sink)
    o = (jnp.dot(p_ctx.astype(BF16), cv_ref[0, 0, 0].astype(BF16), preferred_element_type=F32)
         + jnp.dot(p_band.astype(BF16), vb, preferred_element_type=F32))
    for g in range(C_GROUP):
        o_ref[:, g * C_DH:(g + 1) * C_DH] = o[g * tq:(g + 1) * tq].astype(BF16)


def window_attention_latent(proj, sink, ctx_k, ctx_v, e, *, row_off, batch, seq, tq=256):
    tq = min(tq, seq)
    band = min(tq + 2 * C_WINDOW, seq)
    assert row_off % seq == 0 and seq % tq == 0 and tq % C_WINDOW == 0
    past = ctx_k.shape[3]
    gw = C_GROUP * C_DH
    rb = lambda b: (row_off + b * seq) // seq
    cspec = pl.BlockSpec((1, 1, 1, past, C_DH), lambda b, hk, i: (b, e, hk, 0, 0))
    sink3 = sink.reshape(C_KV_HEADS, 1, C_GROUP)
    return pl.pallas_call(
        functools.partial(_window_attn_kernel, band=band),
        grid=(batch, C_KV_HEADS, seq // tq),
        in_specs=[pl.BlockSpec((1, 1, C_GROUP), lambda b, hk, i: (hk, 0, 0)),
                  pl.BlockSpec((tq, gw), lambda b, hk, i: ((row_off + b * seq) // tq + i, hk)),
                  pl.BlockSpec((seq, C_DH), lambda b, hk, i: (rb(b), C_Q_W // C_DH + hk)),
                  pl.BlockSpec((seq, C_DH), lambda b, hk, i: (rb(b), (C_Q_W + C_KV_W) // C_DH + hk)),
                  cspec, cspec],
        out_specs=pl.BlockSpec((tq, gw), lambda b, hk, i: (b * (seq // tq) + i, hk)),
        out_shape=jax.ShapeDtypeStruct((batch * seq, C_Q_W), BF16),
        compiler_params=_cparams("arbitrary", "arbitrary", "arbitrary"),
        name="window_attn",
    )(sink3, proj, proj, proj, ctx_k, ctx_v)


def kernel(x_prompt, x_sample, cache_a_k, cache_a_v, state_b_fwd, state_b_bwd, cache_c_k, cache_c_v, c, c_ctx, ada_w, ada_b, norm_g, ffn_w_in, ffn_w_out, ab_w_in, ab_w_out, a_lambda, a_subln_g, b_alpha_w, b_alpha_b, b_norm_g, c_w_in, c_w_out, c_sink, final_g):
    bp, sp, d = x_prompt.shape
    bs, ss, _ = x_sample.shape
    depth = ada_w.shape[0]
    mp, ms = bp * sp, bs * ss
    rows_info = (mp, ss)

    x = jnp.concatenate([x_prompt.reshape(mp, d), x_sample.reshape(ms, d)], axis=0)
    cond = jnp.concatenate([c_ctx[None, :], c, jnp.zeros((COND_ROWS - 1 - bs, d), F32)], axis=0)
    mods = adaln(cond, ada_w, ada_b).reshape(depth, COND_ROWS, N_MOD, d)
    rope_a = _rope_tables(ss, A_DH, 2) + (2 * A_QK_W,)
    rope_c = _rope_tables(ss, C_DH, 1) + (C_Q_W + C_KV_W,)

    a_k, a_v, b_f, b_b, c_k, c_v = [], [], [], [], [], []
    for l in range(depth):
        mod = mods[l]
        x = ffn_half_step(x, mod[:, 0:3], norm_g[l, 0], ffn_w_in[l, 0], ffn_w_out[l, 0], rows_info)
        if l % 2 == 0:
            e = l // 2
            lam_init = 0.8 - 0.6 * math.exp(-0.3 * l)
            zeros = jnp.zeros((B_RANK, B_QK_W), F32)
            w_alpha = jnp.concatenate([jnp.concatenate([b_alpha_w[e, 0], zeros], axis=1),
                                       jnp.concatenate([zeros, b_alpha_w[e, 1]], axis=1)], axis=0)
            gate_w = (ab_w_in[e][:, AB_MAIN:], w_alpha, b_alpha_b[e].reshape(-1))
            proj, gates = mixer_in_proj(x, mod[:, 3:5], norm_g[l, 1], ab_w_in[e], AB_MAIN, rows_info, rope_a,
                                        gates=gate_w)
            attn_p, ak, av = diff_attention(proj, a_lambda[e], a_subln_g[e], lam_init, row_off=0, batch=bp,
                                            seq=sp, tq=sp, hb=A_HEADS, emit_kv=True)
            (attn_s,) = diff_attention(proj, a_lambda[e], a_subln_g[e], lam_init, row_off=mp, batch=bs,
                                       seq=ss, tq=min(256, ss), hb=2, ctx=(cache_a_k, cache_a_v, e))
            gla_p, sf, sb = gla_bidirectional(proj, gates, b_norm_g[e], row_off=0, batch=bp, seq=sp,
                                              emit_state=True)
            (gla_s,) = gla_bidirectional(proj, gates, b_norm_g[e], row_off=mp, batch=bs, seq=ss,
                                         states=(state_b_fwd, state_b_bwd, e))
            a_k.append(ak), a_v.append(av), b_f.append(sf), b_b.append(sb)
            x = mixer_out_proj([attn_p, gla_p], [attn_s, gla_s], ab_w_out[e], x, mod[:, 5:6], rows_info)
        else:
            o = l // 2
            proj = mixer_in_proj(x, mod[:, 3:5], norm_g[l, 1], c_w_in[o], c_w_in.shape[2], rows_info, rope_c)
            mix_p, ck, cv = sink_attention_context(proj, c_sink[o], batch=bp, seq=sp)
            mix_s = window_attention_latent(proj, c_sink[o], cache_c_k, cache_c_v, o, row_off=mp, batch=bs, seq=ss)
            c_k.append(ck), c_v.append(cv)
            x = mixer_out_proj([mix_p], [mix_s], c_w_out[o], x, mod[:, 5:6], rows_info)
        x = ffn_half_step(x, mod[:, 6:9], norm_g[l, 2], ffn_w_in[l, 1], ffn_w_out[l, 1], rows_info,
                          final_g=final_g if l == depth - 1 else None)

    y_prompt = x[:mp].reshape(bp, sp, d)
    y_sample = x[mp:].reshape(bs, ss, d)
    cat = lambda parts: parts[0] if len(parts) == 1 else jnp.concatenate(parts, axis=1)
    return (y_prompt, y_sample, cat(a_k), cat(a_v), cat(b_f), cat(b_b), cat(c_k), cat(c_v))
```

```python
import functools
import math

import numpy as np
import jax
import jax.numpy as jnp
from jax import lax
from jax.experimental import pallas as pl
from jax.experimental.pallas import tpu as pltpu

F32 = jnp.float32
BF16 = jnp.bfloat16

EPS = 1e-6
GRID_W = 64
ROPE_BASE = 10000.0
N_MOD = 9
A_HEADS, A_DH, A_DV = 8, 64, 128
B_HEADS, B_DK, B_DV, B_RANK, B_TAU = 4, 128, 256, 16, 16.0
C_HEADS, C_KV_HEADS, C_DH, C_WINDOW = 16, 4, 128, 128
A_QK_W = A_HEADS * 2 * A_DH
A_V_W = A_HEADS * A_DV
B_QK_W = B_HEADS * B_DK
B_V_W = B_HEADS * B_DV
AB_MAIN = 2 * A_QK_W + A_V_W + 2 * B_QK_W + 2 * B_V_W
C_GROUP = C_HEADS // C_KV_HEADS
C_Q_W = C_HEADS * C_DH
C_KV_W = C_KV_HEADS * C_DH

V7X_VMEM_BYTES = 64 * 1024 * 1024
VMEM_LIMIT_BYTES = V7X_VMEM_BYTES - 8 * 1024 * 1024
COND_ROWS = 8
ROW_CHUNK = 128
GATE_ROWS = 256


def _tile(n, pref, align=128):
    if n <= pref:
        return n
    t = (pref // align) * align
    while n % t:
        t -= align
    assert t > 0, (n, pref)
    return t


def _cparams(*sem):
    return pltpu.CompilerParams(dimension_semantics=sem, vmem_limit_bytes=VMEM_LIMIT_BYTES)


def _bdot(a, b):
    return jnp.dot(a.astype(BF16), b.astype(BF16), preferred_element_type=F32)


def _dot_nt(a, b):
    return lax.dot_general(a, b, (((1,), (1,)), ((), ())), preferred_element_type=F32)


def _dot_tn(a, b):
    return lax.dot_general(a, b, (((0,), (0,)), ((), ())), preferred_element_type=F32)


def _rms(x):
    return x * lax.rsqrt(jnp.mean(x * x, axis=-1, keepdims=True) + EPS)


def _adaln_kernel(cond_ref, w_ref, b_ref, o_ref):
    c = cond_ref[...]
    o_ref[0] = _bdot(c * jax.nn.sigmoid(c), w_ref[0]) + b_ref[0]


def adaln(cond, ada_w, ada_b):
    depth, d, n = ada_w.shape
    tn = _tile(n, 1024)
    return pl.pallas_call(
        _adaln_kernel,
        grid=(depth, n // tn),
        in_specs=[pl.BlockSpec((COND_ROWS, d), lambda l, j: (0, 0)),
                  pl.BlockSpec((1, d, tn), lambda l, j: (l, 0, j)),
                  pl.BlockSpec((1, 1, tn), lambda l, j: (l, 0, j))],
        out_specs=pl.BlockSpec((1, COND_ROWS, tn), lambda l, j: (l, 0, j)),
        out_shape=jax.ShapeDtypeStruct((depth, COND_ROWS, n), F32),
        compiler_params=_cparams("arbitrary", "arbitrary"),
        name="adaln",
    )(cond, ada_w, ada_b.reshape(depth, 1, n))


def _norm_modulate(x_ref, g_ref, mod_ref, h_ref):
    g = g_ref[...]
    shift = mod_ref[0, 0:1, :]
    scale1 = 1.0 + mod_ref[0, 1:2, :]

    def body(r, carry):
        rows = pl.ds(pl.multiple_of(r * ROW_CHUNK, ROW_CHUNK), ROW_CHUNK)
        h_ref[rows, :] = ((_rms(x_ref[rows, :]) * g) * scale1 + shift).astype(BF16)
        return carry

    lax.fori_loop(0, x_ref.shape[0] // ROW_CHUNK, body, 0)


def _row_tile(rows_info):
    n_prompt_rows, lat_rows = rows_info
    return _tile(math.gcd(n_prompt_rows, lat_rows), 1024, align=ROW_CHUNK)


def _cond_row(i, tm, n_prompt_rows, lat_rows):
    return jnp.maximum((i * tm - n_prompt_rows) // lat_rows + 1, 0)


def _ffn_kernel(*refs, n_x, n_prompt_tiles, row_split, n_chunk, final):
    x_refs, refs = refs[:n_x], refs[n_x:]
    mod_ref, g_ref, wg_ref, wu_ref, wo_ref = refs[:5]
    refs = refs[5:]
    if final:
        fg_ref, refs = refs[0], refs[1:]
    o_ref, h_ref = refs
    i, j = pl.program_id(0), pl.program_id(1)

    def with_x(fn):
        if n_x == 1:
            fn(x_refs[0])
        else:
            pl.when(i < n_prompt_tiles)(lambda: fn(x_refs[0]))
            pl.when(i >= n_prompt_tiles)(lambda: fn(x_refs[1]))

    @pl.when(j == 0)
    def _():
        with_x(lambda x_ref: _norm_modulate(x_ref, g_ref, mod_ref, h_ref))

    tm, d = h_ref.shape
    rt = tm // row_split
    for r in range(row_split):
        rows = slice(r * rt, (r + 1) * rt)
        h = h_ref[rows, :]
        gate = jnp.dot(h, wg_ref[...], preferred_element_type=F32)
        up = jnp.dot(h, wu_ref[...], preferred_element_type=F32)
        act = (gate * jax.nn.sigmoid(gate) * up).astype(BF16)
        for c in range(d // n_chunk):
            cols = slice(c * n_chunk, (c + 1) * n_chunk)
            part = jnp.dot(act, wo_ref[:, cols], preferred_element_type=F32)

            @pl.when(j == 0)
            def _():
                o_ref[rows, cols] = part

            @pl.when(j > 0)
            def _():
                o_ref[rows, cols] += part

    @pl.when(j == pl.num_programs(1) - 1)
    def _():
        half_gate = 0.5 * mod_ref[0, 2:3, :]

        def finish(x_ref):
            def body(r, carry):
                rows = pl.ds(pl.multiple_of(r * ROW_CHUNK, ROW_CHUNK), ROW_CHUNK)
                xn = x_ref[rows, :] + half_gate * o_ref[rows, :]
                if final:
                    xn = _rms(xn) * fg_ref[...]
                o_ref[rows, :] = xn
                return carry

            lax.fori_loop(0, tm // ROW_CHUNK, body, 0)

        with_x(finish)


def ffn_half_step(xs, mod3, g, w_in, w_out, widx, rows_info, final_g=None):
    d = xs[0].shape[1]
    f = w_out.shape[-2]
    n_prompt_rows, lat_rows = rows_info
    m = sum(x.shape[0] for x in xs)
    tm, tf, n_chunk = _row_tile(rows_info), _tile(f, 512), _tile(d, 512)
    nf = f // tf
    npt = n_prompt_rows // tm
    final = final_g is not None
    l, k = widx
    if len(xs) == 1:
        x_specs = [pl.BlockSpec((tm, d), lambda i, j: (i, 0))]
    else:
        x_specs = [pl.BlockSpec((tm, d), lambda i, j: (jnp.minimum(i, npt - 1), 0), pipeline_mode=pl.Buffered(1)),
                   pl.BlockSpec((tm, d), lambda i, j: (jnp.maximum(i - npt, 0), 0), pipeline_mode=pl.Buffered(1))]
    in_specs = x_specs + [
        pl.BlockSpec((1, 3, d), lambda i, j: (_cond_row(i, tm, n_prompt_rows, lat_rows), 0, 0)),
        pl.BlockSpec((1, d), lambda i, j: (0, 0)),
        pl.BlockSpec((None, None, d, tf), lambda i, j: (l, k, 0, j)),
        pl.BlockSpec((None, None, d, tf), lambda i, j: (l, k, 0, nf + j)),
        pl.BlockSpec((None, None, tf, d), lambda i, j: (l, k, j, 0))]
    args = list(xs) + [mod3, g.reshape(1, d), w_in, w_in, w_out]
    if final:
        in_specs.append(pl.BlockSpec((1, d), lambda i, j: (0, 0)))
        args.append(final_g.reshape(1, d))
    return pl.pallas_call(
        functools.partial(_ffn_kernel, n_x=len(xs), n_prompt_tiles=npt, row_split=max(1, tm // 512),
                          n_chunk=n_chunk, final=final),
        grid=(m // tm, nf),
        in_specs=in_specs,
        out_specs=pl.BlockSpec((tm, d), lambda i, j: (i, 0)),
        out_shape=jax.ShapeDtypeStruct((m, d), F32),
        scratch_shapes=[pltpu.VMEM((tm, d), BF16)],
        compiler_params=_cparams("arbitrary", "arbitrary"),
        name="ffn_final" if final else "ffn",
    )(*args)


def _log_sigmoid(z):
    return jnp.minimum(z, 0.0) - jnp.log(1.0 + jnp.exp(-jnp.abs(z)))


def _proj_kernel(*refs, n_prompt_tiles, n_rope_tiles, rope_shift, with_gates):
    x_ref, mod_ref, g_ref, w_ref, cos_ref, sa_ref, sb_ref = refs[:7]
    if with_gates:
        wl_ref, wa_ref, ba_ref, o_ref, gates_ref, h_ref = refs[7:]
    else:
        o_ref, h_ref = refs[7:]
    i, j = pl.program_id(0), pl.program_id(1)

    @pl.when(j == 0)
    def _():
        _norm_modulate(x_ref, g_ref, mod_ref, h_ref)
        if with_gates:
            wl = wl_ref[...]
            wa = wa_ref[...].astype(BF16)

            def body(r, carry):
                rows = pl.ds(pl.multiple_of(r * GATE_ROWS, GATE_ROWS), GATE_ROWS)
                low = jnp.dot(h_ref[rows, :], wl, preferred_element_type=F32)
                z = jnp.dot(low.astype(BF16), wa, preferred_element_type=F32) + ba_ref[...]
                gates_ref[rows, :] = _log_sigmoid(z) * (1.0 / B_TAU)
                return carry

            lax.fori_loop(0, x_ref.shape[0] // GATE_ROWS, body, 0)

    acc = jnp.dot(h_ref[...], w_ref[...], preferred_element_type=F32)
    rope = jnp.logical_and(i >= n_prompt_tiles, j < n_rope_tiles)

    @pl.when(rope)
    def _():
        cos, sa, sb = cos_ref[...], sa_ref[...], sb_ref[...]
        for c in range(acc.shape[1] // 128):
            a = acc[:, c * 128:(c + 1) * 128]
            o_ref[:, c * 128:(c + 1) * 128] = (a * cos + pltpu.roll(a, 128 - rope_shift, 1) * sa
                                               + pltpu.roll(a, rope_shift, 1) * sb)

    @pl.when(jnp.logical_not(rope))
    def _():
        o_ref[...] = acc


def _rope_tables(n, head_dim, n_sub):
    assert head_dim * n_sub == 128
    rows = n // GRID_W
    row = jnp.repeat(jnp.arange(rows, dtype=F32), GRID_W)
    col = jnp.tile(jnp.arange(GRID_W, dtype=F32), rows)
    d_axis = head_dim // 2
    shift = d_axis // 2
    inv = ROPE_BASE ** (-jnp.arange(0, d_axis, 2, dtype=F32) / d_axis)
    lane = jnp.arange(128)
    sub = lane % head_dim
    is_col = (sub // d_axis) == 1
    within = sub % d_axis
    freq = within % shift
    second = (within // shift) == 1
    pos = jnp.where(is_col[None, :], col[:, None], row[:, None])
    ang = pos * inv[freq][None, :]
    cos, sin = jnp.cos(ang), jnp.sin(ang)
    sin_a = jnp.where(second[None, :], 0.0, -sin)
    sin_b = jnp.where(second[None, :], sin, 0.0)
    return cos, sin_a, sin_b, shift


def mixer_in_proj(x, mod2, g, w, e, n_cols, rows_info, rope, gates=None):
    m, d = x.shape
    n_prompt_rows, lat_rows = rows_info
    tm, tn = _row_tile(rows_info), _tile(n_cols, 512)
    cos, sin_a, sin_b, rope_shift, n_rope_cols = rope
    assert n_rope_cols % tn == 0 and n_prompt_rows % tm == 0
    npt, lat_tiles = n_prompt_rows // tm, lat_rows // tm
    cond = lambda i, j: (_cond_row(i, tm, n_prompt_rows, lat_rows), 0, 0)
    tab = lambda i, j: (jnp.maximum(i - npt, 0) % lat_tiles, 0)
    in_specs = [pl.BlockSpec((tm, d), lambda i, j: (i, 0)),
                pl.BlockSpec((1, 2, d), cond),
                pl.BlockSpec((1, d), lambda i, j: (0, 0)),
                pl.BlockSpec((None, d, tn), lambda i, j: (e, 0, j)),
                pl.BlockSpec((tm, 128), tab), pl.BlockSpec((tm, 128), tab), pl.BlockSpec((tm, 128), tab)]
    args = [x, mod2, g.reshape(1, d), w, cos, sin_a, sin_b]
    out_specs = pl.BlockSpec((tm, tn), lambda i, j: (i, j))
    out_shape = jax.ShapeDtypeStruct((m, n_cols), F32)
    if gates is not None:
        w_low, w_alpha, b_alpha = gates
        n_gate = w_alpha.shape[1]
        in_specs += [pl.BlockSpec(w_low.shape, lambda i, j: (0, 0)),
                     pl.BlockSpec(w_alpha.shape, lambda i, j: (0, 0)),
                     pl.BlockSpec((1, n_gate), lambda i, j: (0, 0))]
        args += [w_low, w_alpha, b_alpha.reshape(1, n_gate)]
        out_specs = [out_specs, pl.BlockSpec((tm, n_gate), lambda i, j: (i, 0))]
        out_shape = [out_shape, jax.ShapeDtypeStruct((m, n_gate), F32)]
    return pl.pallas_call(
        functools.partial(_proj_kernel, n_prompt_tiles=npt, n_rope_tiles=n_rope_cols // tn,
                          rope_shift=rope_shift, with_gates=gates is not None),
        grid=(m // tm, n_cols // tn),
        in_specs=in_specs, out_specs=out_specs, out_shape=out_shape,
        scratch_shapes=[pltpu.VMEM((tm, d), BF16)],
        compiler_params=_cparams("arbitrary", "arbitrary"),
        name="mixer_in_proj_gated" if gates is not None else "mixer_in_proj",
    )(*args)


def _out_proj_kernel(*refs, n_parts, n_prompt_tiles, row_split):
    parts_p, parts_s = refs[:n_parts], refs[n_parts:2 * n_parts]
    w_refs = refs[2 * n_parts:3 * n_parts]
    x_ref, gate_ref, o_ref = refs[3 * n_parts:]
    i = pl.program_id(0)

    tm = o_ref.shape[0]
    rt = tm // row_split

    def run(parts):
        for r in range(row_split):
            rows = slice(r * rt, (r + 1) * rt)
            acc = None
            for p_ref, w_ref in zip(parts, w_refs):
                t = jnp.dot(p_ref[rows, :], w_ref[...], preferred_element_type=F32)
                acc = t if acc is None else acc + t
            o_ref[rows, :] = x_ref[rows, :] + gate_ref[0] * acc

    pl.when(i < n_prompt_tiles)(lambda: run(parts_p))
    pl.when(i >= n_prompt_tiles)(lambda: run(parts_s))


def mixer_out_proj(parts_p, parts_s, w, e, x, gate, rows_info):
    m, d = x.shape
    n_prompt_rows, lat_rows = rows_info
    tm, tn = _row_tile(rows_info), _tile(d, 1024)
    npt = n_prompt_rows // tm
    n_parts = len(parts_p)
    in_specs, w_specs, off = [], [], 0
    for p in parts_p:
        in_specs.append(pl.BlockSpec((tm, p.shape[1]), lambda i, j: (jnp.minimum(i, npt - 1), 0)))
    for p in parts_s:
        kp = p.shape[1]
        in_specs.append(pl.BlockSpec((tm, kp), lambda i, j: (jnp.maximum(i - npt, 0), 0)))
        assert off % kp == 0
        w_specs.append(pl.BlockSpec((None, kp, tn), functools.partial(lambda i, j, rb: (e, rb, j), rb=off // kp)))
        off += kp
    in_specs += w_specs
    in_specs += [pl.BlockSpec((tm, tn), lambda i, j: (i, j)),
                 pl.BlockSpec((1, 1, tn), lambda i, j: (_cond_row(i, tm, n_prompt_rows, lat_rows), 0, j))]
    return pl.pallas_call(
        functools.partial(_out_proj_kernel, n_parts=n_parts, n_prompt_tiles=npt, row_split=max(1, tm // 512)),
        grid=(m // tm, d // tn),
        in_specs=in_specs,
        out_specs=pl.BlockSpec((tm, tn), lambda i, j: (i, j)),
        out_shape=jax.ShapeDtypeStruct((m, d), F32),
        compiler_params=_cparams("arbitrary", "arbitrary"),
        name="mixer_out_proj",
    )(*parts_p, *parts_s, *([w] * n_parts), x, gate)


def _softmax_parts(parts, extra_logit=None):
    m = functools.reduce(jnp.maximum, [jnp.max(p, axis=-1, keepdims=True) for p in parts])
    if extra_logit is not None:
        m = jnp.maximum(m, extra_logit)
    es = [jnp.exp(p - m) for p in parts]
    den = functools.reduce(jnp.add, [jnp.sum(e, axis=-1, keepdims=True) for e in es])
    if extra_logit is not None:
        den = den + jnp.exp(extra_logit - m)
    inv = 1.0 / den
    return [e * inv for e in es]


def _diff_attn_kernel(*refs, hb, lam_init, has_ctx, emit_kv):
    it = iter(refs)
    lam_ref, g_ref, q_ref, k_ref, v_ref = (next(it) for _ in range(5))
    ck_ref, cv_ref = (next(it), next(it)) if has_ctx else (None, None)
    o_ref = next(it)
    nk_ref, nv_ref = (next(it), next(it)) if emit_kv else (None, None)

    lp = lam_ref[...]
    lam = (jnp.exp(jnp.sum(lp[0:1] * lp[1:2], axis=-1, keepdims=True))
           - jnp.exp(jnp.sum(lp[2:3] * lp[3:4], axis=-1, keepdims=True)) + lam_init)
    scale = A_DH ** -0.5
    first = lax.broadcasted_iota(jnp.int32, (1, 2 * A_DH), 1) < A_DH
    for hh in range(hb):
        cols = slice(hh * 128, (hh + 1) * 128)
        q, k, v = q_ref[:, cols], k_ref[:, cols], v_ref[:, cols]
        q1 = jnp.where(first, q, 0.0).astype(BF16)
        q2 = jnp.where(first, 0.0, q).astype(BF16)
        keys, vals = [k.astype(BF16)], [v.astype(BF16)]
        if has_ctx:
            keys.insert(0, ck_ref[0, 0, hh].astype(BF16))
            vals.insert(0, cv_ref[0, 0, hh].astype(BF16))
        p1 = _softmax_parts([_dot_nt(q1, kk) * scale for kk in keys])
        p2 = _softmax_parts([_dot_nt(q2, kk) * scale for kk in keys])
        o = None
        for a, b, vv in zip(p1, p2, vals):
            t = jnp.dot((a - lam * b).astype(BF16), vv, preferred_element_type=F32)
            o = t if o is None else o + t
        o_ref[:, cols] = ((_rms(o) * g_ref[...]) * (1.0 - lam_init)).astype(BF16)
        if emit_kv:
            nk_ref[0, 0, hh] = k
            nv_ref[0, 0, hh] = v


def diff_attention(proj, lam_p, subln_g, lam_init, *, row_off, batch, seq, tq, hb, ctx=None, emit_kv=False):
    width = hb * 128
    n_hg = A_HEADS // hb
    assert row_off % seq == 0 and seq % tq == 0
    qrow = lambda b, hg, i: ((row_off + b * seq) // tq + i, hg)
    krow = lambda b, hg, i: ((row_off + b * seq) // seq, A_QK_W // width + hg)
    vrow = lambda b, hg, i: ((row_off + b * seq) // seq, 2 * A_QK_W // width + hg)
    in_specs = [pl.BlockSpec(lam_p.shape, lambda b, hg, i: (0, 0)),
                pl.BlockSpec((1, A_DV), lambda b, hg, i: (0, 0)),
                pl.BlockSpec((tq, width), qrow),
                pl.BlockSpec((seq, width), krow),
                pl.BlockSpec((seq, width), vrow)]
    args = [lam_p, subln_g.reshape(1, A_DV), proj, proj, proj]
    if ctx is not None:
        ctx_k, ctx_v, e = ctx
        past = ctx_k.shape[3]
        cspec = lambda b, hg, i: (b, e, hg, 0, 0)
        in_specs += [pl.BlockSpec((1, 1, hb, past, 2 * A_DH), cspec), pl.BlockSpec((1, 1, hb, past, A_DV), cspec)]
        args += [ctx_k, ctx_v]
    out_specs = [pl.BlockSpec((tq, width), lambda b, hg, i: (b * (seq // tq) + i, hg))]
    out_shape = [jax.ShapeDtypeStruct((batch * seq, A_V_W), BF16)]
    if emit_kv:
        assert tq == seq
        kvspec = pl.BlockSpec((1, 1, hb, seq, 128), lambda b, hg, i: (b, 0, hg, 0, 0))
        out_specs += [kvspec, kvspec]
        out_shape += [jax.ShapeDtypeStruct((batch, 1, A_HEADS, seq, 128), F32)] * 2
    return pl.pallas_call(
        functools.partial(_diff_attn_kernel, hb=hb, lam_init=lam_init, has_ctx=ctx is not None, emit_kv=emit_kv),
        grid=(batch, n_hg, seq // tq),
        in_specs=in_specs, out_specs=out_specs, out_shape=out_shape,
        compiler_params=_cparams("arbitrary", "arbitrary", "arbitrary"),
        name="diff_attn_ctx" if ctx is not None else "diff_attn",
    )(*args)


def _gla_level_matrices(chunk, fwd):
    t = np.arange(chunk)[:, None]
    i = np.arange(chunk)[None, :]
    mats = [i <= t, i > t] if fwd else [i >= t, i < t]
    h = chunk // 2
    while h >= 1:
        p = t % (2 * h)
        base = t - p
        if fwd:
            m = base + h - 1
            a = np.where(p >= h, (i > m) & (i <= t), (i > t) & (i <= m))
        else:
            m = base + h
            a = np.where(p < h, (i >= t) & (i < m), (i >= m) & (i < t))
        mats.append(a)
        h //= 2
    return jnp.asarray(np.concatenate(mats, axis=0).astype(np.float32), dtype=BF16)


def _gla_kernel(*refs, chunk, has_state, emit_state):
    it = iter(refs)
    af_ref, ab_ref, q_ref, k_ref, v_ref, laf_ref, lab_ref, r_ref, g_ref = (next(it) for _ in range(9))
    sf_ref, sb_ref = (next(it), next(it)) if has_state else (None, None)
    o_ref = next(it)
    nsf_ref, nsb_ref = (next(it), next(it)) if emit_state else (None, None)
    acc_ref, st_ref = next(it), next(it)

    seq = q_ref.shape[0]
    n_chunks = seq // chunk
    n_lev = chunk.bit_length() - 1
    ti = lax.broadcasted_iota(jnp.int32, (chunk, chunk), 0)
    si = lax.broadcasted_iota(jnp.int32, (chunk, chunk), 1)
    split = ti ^ si
    scale = B_DK ** -0.5

    for fwd, a_ref, la_ref, s0_ref, ns_ref in ((True, af_ref, laf_ref, sf_ref, nsf_ref),
                                                (False, ab_ref, lab_ref, sb_ref, nsb_ref)):
        order = (ti > si) if fwd else (ti < si)
        st_ref[...] = s0_ref[0, 0, 0].T if has_state else jnp.zeros(st_ref.shape, F32)

        def body(cc, carry, fwd=fwd, a_ref=a_ref, la_ref=la_ref, order=order):
            c = cc if fwd else n_chunks - 1 - cc
            rows = pl.ds(pl.multiple_of(c * chunk, chunk), chunk)
            q = q_ref[rows, :] * scale
            k = k_ref[rows, :]
            vb = v_ref[rows, :].astype(BF16)
            la = la_ref[rows, :]
            la_hi = la.astype(BF16)
            la_lo = (la - la_hi.astype(F32)).astype(BF16)
            r2 = jnp.dot(a_ref[...], jnp.concatenate([la_hi, la_lo], axis=1), preferred_element_type=F32)
            e = jnp.exp(r2[:, :B_DK] + r2[:, B_DK:])
            st = st_ref[...]
            inter = _dot_nt((q * e[0:chunk]).astype(BF16), st.astype(BF16))
            ku = (k * e[chunk:2 * chunk]).astype(BF16)
            scores = jnp.where(ti == si, _dot_nt(q.astype(BF16), k.astype(BF16)), 0.0)
            for lev in range(n_lev):
                half = chunk >> (lev + 1)
                f = e[(2 + lev) * chunk:(3 + lev) * chunk]
                sc = _dot_nt((q * f).astype(BF16), (k * f).astype(BF16))
                scores = jnp.where(order & (split >= half) & (split < 2 * half), sc, scores)
            o = inter + jnp.dot(scores.astype(BF16), vb, preferred_element_type=F32)
            if fwd:
                acc_ref[rows, :] = o
                total = e[chunk - 1:chunk]
            else:
                acc_ref[rows, :] += o
                total = e[0:1]
            st_ref[...] = st * total + _dot_tn(vb, ku)
            return carry

        lax.fori_loop(0, n_chunks, body, 0)
        if emit_state:
            ns_ref[0, 0, 0] = st_ref[...].T

    def epilogue(r, carry):
        rows = pl.ds(pl.multiple_of(r * ROW_CHUNK, ROW_CHUNK), ROW_CHUNK)
        gate = r_ref[rows, :]
        o_ref[rows, :] = ((_rms(acc_ref[rows, :]) * g_ref[...]) * (gate * jax.nn.sigmoid(gate))).astype(BF16)
        return carry

    lax.fori_loop(0, seq // ROW_CHUNK, epilogue, 0)


def gla_bidirectional(proj, gates, bnorm_g, *, row_off, batch, seq, states=None, emit_state=False):
    chunk = min(256, seq)
    assert row_off % seq == 0 and seq % chunk == 0 and chunk & (chunk - 1) == 0
    rb = lambda b: (row_off + b * seq) // seq
    q_off = (2 * A_QK_W + A_V_W) // B_DK
    k_off = q_off + B_HEADS
    v_off = (2 * A_QK_W + A_V_W + 2 * B_QK_W) // B_DV
    r_off = v_off + B_HEADS
    a_f, a_b = _gla_level_matrices(chunk, True), _gla_level_matrices(chunk, False)
    whole = lambda b, h: (0, 0)
    in_specs = [pl.BlockSpec(a_f.shape, whole), pl.BlockSpec(a_b.shape, whole),
                pl.BlockSpec((seq, B_DK), lambda b, h: (rb(b), q_off + h)),
                pl.BlockSpec((seq, B_DK), lambda b, h: (rb(b), k_off + h)),
                pl.BlockSpec((seq, B_DV), lambda b, h: (rb(b), v_off + h)),
                pl.BlockSpec((seq, B_DK), lambda b, h: (rb(b), h)),
                pl.BlockSpec((seq, B_DK), lambda b, h: (rb(b), B_HEADS + h)),
                pl.BlockSpec((seq, B_DV), lambda b, h: (rb(b), r_off + h)),
                pl.BlockSpec((1, B_DV), whole)]
    args = [a_f, a_b, proj, proj, proj, gates, gates, proj, bnorm_g.reshape(1, B_DV)]
    if states is not None:
        s_f, s_b, e = states
        sspec = pl.BlockSpec((1, 1, 1, B_DK, B_DV), lambda b, h: (b, e, h, 0, 0))
        in_specs += [sspec, sspec]
        args += [s_f, s_b]
    out_specs = [pl.BlockSpec((seq, B_DV), lambda b, h: (b, h))]
    out_shape = [jax.ShapeDtypeStruct((batch * seq, B_V_W), BF16)]
    if emit_state:
        nspec = pl.BlockSpec((1, 1, 1, B_DK, B_DV), lambda b, h: (b, 0, h, 0, 0))
        out_specs += [nspec, nspec]
        out_shape += [jax.ShapeDtypeStruct((batch, 1, B_HEADS, B_DK, B_DV), F32)] * 2
    return pl.pallas_call(
        functools.partial(_gla_kernel, chunk=chunk, has_state=states is not None, emit_state=emit_state),
        grid=(batch, B_HEADS),
        in_specs=in_specs, out_specs=out_specs, out_shape=out_shape,
        scratch_shapes=[pltpu.VMEM((seq, B_DV), F32), pltpu.VMEM((B_DV, B_DK), F32)],
        compiler_params=_cparams("arbitrary", "arbitrary"),
        name="gla_state" if states is not None else "gla",
    )(*args)


def _sink_attn_kernel(sink_ref, q_ref, k_ref, v_ref, o_ref, nk_ref, nv_ref):
    seq = q_ref.shape[0]
    scale = C_DH ** -0.5
    for hk in range(C_KV_HEADS):
        kcols = slice(hk * C_DH, (hk + 1) * C_DH)
        k, v = k_ref[:, kcols], v_ref[:, kcols]
        q4 = jnp.concatenate([q_ref[:, (hk * C_GROUP + g) * C_DH:(hk * C_GROUP + g + 1) * C_DH]
                              for g in range(C_GROUP)], axis=0).astype(BF16)
        sink = jnp.concatenate([jnp.broadcast_to(sink_ref[hk, 0:1, g:g + 1], (seq, 1)) for g in range(C_GROUP)],
                               axis=0)
        (p,) = _softmax_parts([_dot_nt(q4, k.astype(BF16)) * scale], extra_logit=sink)
        o = jnp.dot(p.astype(BF16), v.astype(BF16), preferred_element_type=F32)
        for g in range(C_GROUP):
            h = hk * C_GROUP + g
            o_ref[:, h * C_DH:(h + 1) * C_DH] = o[g * seq:(g + 1) * seq].astype(BF16)
        nk_ref[0, 0, hk] = k
        nv_ref[0, 0, hk] = v


def sink_attention_context(proj, sink, *, batch, seq):
    sink3 = sink.reshape(C_KV_HEADS, 1, C_GROUP)
    kvspec = pl.BlockSpec((1, 1, C_KV_HEADS, seq, C_DH), lambda b: (b, 0, 0, 0, 0))
    return pl.pallas_call(
        _sink_attn_kernel,
        grid=(batch,),
        in_specs=[pl.BlockSpec(sink3.shape, lambda b: (0, 0, 0)),
                  pl.BlockSpec((seq, C_Q_W), lambda b: (b, 0)),
                  pl.BlockSpec((seq, C_KV_W), lambda b: (b, C_Q_W // C_KV_W)),
                  pl.BlockSpec((seq, C_KV_W), lambda b: (b, C_Q_W // C_KV_W + 1))],
        out_specs=[pl.BlockSpec((seq, C_Q_W), lambda b: (b, 0)), kvspec, kvspec],
        out_shape=[jax.ShapeDtypeStruct((batch * seq, C_Q_W), BF16)]
        + [jax.ShapeDtypeStruct((batch, 1, C_KV_HEADS, seq, C_DH), F32)] * 2,
        compiler_params=_cparams("arbitrary"),
        name="sink_attn",
    )(sink3, proj, proj, proj)


def _window_attn_kernel(sink_ref, q_ref, k_ref, v_ref, ck_ref, cv_ref, o_ref, *, band):
    tq = q_ref.shape[0]
    seq = k_ref.shape[0]
    scale = C_DH ** -0.5
    i = pl.program_id(2)
    start = pl.multiple_of(jnp.clip(i * tq - C_WINDOW, 0, seq - band), C_WINDOW)
    kb = k_ref[pl.ds(start, band), :].astype(BF16)
    vb = v_ref[pl.ds(start, band), :].astype(BF16)
    q4 = jnp.concatenate([q_ref[:, g * C_DH:(g + 1) * C_DH] for g in range(C_GROUP)], axis=0).astype(BF16)
    sink = jnp.concatenate([jnp.broadcast_to(sink_ref[0, 0:1, g:g + 1], (tq, 1)) for g in range(C_GROUP)], axis=0)
    qpos = i * tq + lax.broadcasted_iota(jnp.int32, (tq, band), 0)
    kpos = start + lax.broadcasted_iota(jnp.int32, (tq, band), 1)
    valid = jnp.abs(qpos - kpos) <= C_WINDOW
    valid4 = jnp.concatenate([valid] * C_GROUP, axis=0)
    s_band = jnp.where(valid4, _dot_nt(q4, kb) * scale, -jnp.inf)
    s_ctx = _dot_nt(q4, ck_ref[0, 0, 0].astype(BF16)) * scale
    p_ctx, p_band = _softmax_parts([s_ctx, s_band], extra_logit=sink)
    o = (jnp.dot(p_ctx.astype(BF16), cv_ref[0, 0, 0].astype(BF16), preferred_element_type=F32)
         + jnp.dot(p_band.astype(BF16), vb, preferred_element_type=F32))
    for g in range(C_GROUP):
        o_ref[:, g * C_DH:(g + 1) * C_DH] = o[g * tq:(g + 1) * tq].astype(BF16)


def window_attention_latent(proj, sink, ctx_k, ctx_v, e, *, row_off, batch, seq, tq=256):
    tq = min(tq, seq)
    band = min(tq + 2 * C_WINDOW, seq)
    assert row_off % seq == 0 and seq % tq == 0 and tq % C_WINDOW == 0
    past = ctx_k.shape[3]
    gw = C_GROUP * C_DH
    rb = lambda b: (row_off + b * seq) // seq
    cspec = pl.BlockSpec((1, 1, 1, past, C_DH), lambda b, hk, i: (b, e, hk, 0, 0))
    sink3 = sink.reshape(C_KV_HEADS, 1, C_GROUP)
    return pl.pallas_call(
        functools.partial(_window_attn_kernel, band=band),
        grid=(batch, C_KV_HEADS, seq // tq),
        in_specs=[pl.BlockSpec((1, 1, C_GROUP), lambda b, hk, i: (hk, 0, 0)),
                  pl.BlockSpec((tq, gw), lambda b, hk, i: ((row_off + b * seq) // tq + i, hk)),
                  pl.BlockSpec((seq, C_DH), lambda b, hk, i: (rb(b), C_Q_W // C_DH + hk)),
                  pl.BlockSpec((seq, C_DH), lambda b, hk, i: (rb(b), (C_Q_W + C_KV_W) // C_DH + hk)),
                  cspec, cspec],
        out_specs=pl.BlockSpec((tq, gw), lambda b, hk, i: (b * (seq // tq) + i, hk)),
        out_shape=jax.ShapeDtypeStruct((batch * seq, C_Q_W), BF16),
        compiler_params=_cparams("arbitrary", "arbitrary", "arbitrary"),
        name="window_attn",
    )(sink3, proj, proj, proj, ctx_k, ctx_v)


def kernel(x_prompt, x_sample, cache_a_k, cache_a_v, state_b_fwd, state_b_bwd, cache_c_k, cache_c_v, c, c_ctx, ada_w, ada_b, norm_g, ffn_w_in, ffn_w_out, ab_w_in, ab_w_out, a_lambda, a_subln_g, b_alpha_w, b_alpha_b, b_norm_g, c_w_in, c_w_out, c_sink, final_g):
    bp, sp, d = x_prompt.shape
    bs, ss, _ = x_sample.shape
    depth = ada_w.shape[0]
    mp, ms = bp * sp, bs * ss
    rows_info = (mp, ss)

    cond = jnp.concatenate([c_ctx[None, :], c, jnp.zeros((COND_ROWS - 1 - bs, d), F32)], axis=0)
    mods = adaln(cond, ada_w, ada_b).reshape(depth, COND_ROWS, N_MOD, d)
    rope_a = _rope_tables(ss, A_DH, 2) + (2 * A_QK_W,)
    rope_c = _rope_tables(ss, C_DH, 1) + (C_Q_W + C_KV_W,)
    ffn_w_in, ffn_w_out, ab_w_in, ab_w_out, c_w_in, c_w_out = (
        w.astype(BF16) for w in (ffn_w_in, ffn_w_out, ab_w_in, ab_w_out, c_w_in, c_w_out))

    a_k, a_v, b_f, b_b, c_k, c_v = [], [], [], [], [], []
    xs = [x_prompt.reshape(mp, d), x_sample.reshape(ms, d)]
    for l in range(depth):
        mod = mods[l]
        x = ffn_half_step(xs, mod[:, 0:3], norm_g[l, 0], ffn_w_in, ffn_w_out, (l, 0), rows_info)
        if l % 2 == 0:
            e = l // 2
            lam_init = 0.8 - 0.6 * math.exp(-0.3 * l)
            zeros = jnp.zeros((B_RANK, B_QK_W), F32)
            w_alpha = jnp.concatenate([jnp.concatenate([b_alpha_w[e, 0], zeros], axis=1),
                                       jnp.concatenate([zeros, b_alpha_w[e, 1]], axis=1)], axis=0)
            gate_w = (ab_w_in[e][:, AB_MAIN:], w_alpha, b_alpha_b[e].reshape(-1))
            proj, gates = mixer_in_proj(x, mod[:, 3:5], norm_g[l, 1], ab_w_in, e, AB_MAIN, rows_info, rope_a,
                                        gates=gate_w)
            attn_p, ak, av = diff_attention(proj, a_lambda[e], a_subln_g[e], lam_init, row_off=0, batch=bp,
                                            seq=sp, tq=sp, hb=A_HEADS, emit_kv=True)
            (attn_s,) = diff_attention(proj, a_lambda[e], a_subln_g[e], lam_init, row_off=mp, batch=bs,
                                       seq=ss, tq=min(256, ss), hb=2, ctx=(cache_a_k, cache_a_v, e))
            gla_p, sf, sb = gla_bidirectional(proj, gates, b_norm_g[e], row_off=0, batch=bp, seq=sp,
                                              emit_state=True)
            (gla_s,) = gla_bidirectional(proj, gates, b_norm_g[e], row_off=mp, batch=bs, seq=ss,
                                         states=(state_b_fwd, state_b_bwd, e))
            a_k.append(ak), a_v.append(av), b_f.append(sf), b_b.append(sb)
            x = mixer_out_proj([attn_p, gla_p], [attn_s, gla_s], ab_w_out, e, x, mod[:, 5:6], rows_info)
        else:
            o = l // 2
            proj = mixer_in_proj(x, mod[:, 3:5], norm_g[l, 1], c_w_in, o, c_w_in.shape[2], rows_info, rope_c)
            mix_p, ck, cv = sink_attention_context(proj, c_sink[o], batch=bp, seq=sp)
            mix_s = window_attention_latent(proj, c_sink[o], cache_c_k, cache_c_v, o, row_off=mp, batch=bs, seq=ss)
            c_k.append(ck), c_v.append(cv)
            x = mixer_out_proj([mix_p], [mix_s], c_w_out, o, x, mod[:, 5:6], rows_info)
        x = ffn_half_step([x], mod[:, 6:9], norm_g[l, 2], ffn_w_in, ffn_w_out, (l, 1), rows_info,
                          final_g=final_g if l == depth - 1 else None)
        xs = [x]

    y_prompt, y_sample = x[:mp].reshape(bp, sp, d), x[mp:].reshape(bs, ss, d)
    cat = lambda parts: parts[0] if len(parts) == 1 else jnp.concatenate(parts, axis=1)
    return (y_prompt, y_sample, cat(a_k), cat(a_v), cat(b_f), cat(b_b), cat(c_k), cat(c_v))
```

```python
import functools
import math

import numpy as np
import jax
import jax.numpy as jnp
from jax import lax
from jax.experimental import pallas as pl
from jax.experimental.pallas import tpu as pltpu

F32 = jnp.float32
BF16 = jnp.bfloat16

EPS = 1e-6
GRID_W = 64
ROPE_BASE = 10000.0
N_MOD = 9
A_HEADS, A_DH, A_DV = 8, 64, 128
B_HEADS, B_DK, B_DV, B_RANK, B_TAU = 4, 128, 256, 16, 16.0
C_HEADS, C_KV_HEADS, C_DH, C_WINDOW = 16, 4, 128, 128
A_QK_W = A_HEADS * 2 * A_DH
A_V_W = A_HEADS * A_DV
B_QK_W = B_HEADS * B_DK
B_V_W = B_HEADS * B_DV
AB_MAIN = 2 * A_QK_W + A_V_W + 2 * B_QK_W + 2 * B_V_W
C_GROUP = C_HEADS // C_KV_HEADS
C_Q_W = C_HEADS * C_DH
C_KV_W = C_KV_HEADS * C_DH

V7X_VMEM_BYTES = 64 * 1024 * 1024
VMEM_LIMIT_BYTES = V7X_VMEM_BYTES - 8 * 1024 * 1024
COND_ROWS = 8
ROW_CHUNK = 128
GATE_ROWS = 256


def _tile(n, pref, align=128):
    if n <= pref:
        return n
    t = (pref // align) * align
    while n % t:
        t -= align
    assert t > 0, (n, pref)
    return t


def _cparams(*sem):
    return pltpu.CompilerParams(dimension_semantics=sem, vmem_limit_bytes=VMEM_LIMIT_BYTES)


def _bdot(a, b):
    return jnp.dot(a.astype(BF16), b.astype(BF16), preferred_element_type=F32)


def _dot_nt(a, b):
    return lax.dot_general(a, b, (((1,), (1,)), ((), ())), preferred_element_type=F32)


def _dot_tn(a, b):
    return lax.dot_general(a, b, (((0,), (0,)), ((), ())), preferred_element_type=F32)


def _rms(x):
    return x * lax.rsqrt(jnp.mean(x * x, axis=-1, keepdims=True) + EPS)


def _adaln_kernel(cond_ref, w_ref, b_ref, o_ref):
    c = cond_ref[...]
    o_ref[0] = _bdot(c * jax.nn.sigmoid(c), w_ref[0]) + b_ref[0]


def adaln(cond, ada_w, ada_b):
    depth, d, n = ada_w.shape
    tn = _tile(n, 1024)
    return pl.pallas_call(
        _adaln_kernel,
        grid=(depth, n // tn),
        in_specs=[pl.BlockSpec((COND_ROWS, d), lambda l, j: (0, 0)),
                  pl.BlockSpec((1, d, tn), lambda l, j: (l, 0, j)),
                  pl.BlockSpec((1, 1, tn), lambda l, j: (l, 0, j))],
        out_specs=pl.BlockSpec((1, COND_ROWS, tn), lambda l, j: (l, 0, j)),
        out_shape=jax.ShapeDtypeStruct((depth, COND_ROWS, n), F32),
        compiler_params=_cparams("arbitrary", "arbitrary"),
        name="adaln",
    )(cond, ada_w, ada_b.reshape(depth, 1, n))


def _norm_modulate(x_ref, g_ref, mod_ref, h_ref):
    g = g_ref[...]
    shift = mod_ref[0, 0:1, :]
    scale1 = 1.0 + mod_ref[0, 1:2, :]

    def body(r, carry):
        rows = pl.ds(pl.multiple_of(r * ROW_CHUNK, ROW_CHUNK), ROW_CHUNK)
        h_ref[rows, :] = ((_rms(x_ref[rows, :]) * g) * scale1 + shift).astype(BF16)
        return carry

    lax.fori_loop(0, x_ref.shape[0] // ROW_CHUNK, body, 0)


def _row_tile(rows_info):
    n_prompt_rows, lat_rows = rows_info
    return _tile(math.gcd(n_prompt_rows, lat_rows), 1024, align=ROW_CHUNK)


def _cond_row(i, tm, n_prompt_rows, lat_rows):
    return jnp.maximum((i * tm - n_prompt_rows) // lat_rows + 1, 0)


def _ffn_kernel(*refs, n_x, n_prompt_tiles, row_split, n_chunk, final):
    x_refs, refs = refs[:n_x], refs[n_x:]
    mod_ref, g_ref, wg_ref, wu_ref, wo_ref = refs[:5]
    refs = refs[5:]
    if final:
        fg_ref, refs = refs[0], refs[1:]
    o_ref, h_ref = refs
    i, j = pl.program_id(0), pl.program_id(1)

    def with_x(fn):
        if n_x == 1:
            fn(x_refs[0])
        else:
            pl.when(i < n_prompt_tiles)(lambda: fn(x_refs[0]))
            pl.when(i >= n_prompt_tiles)(lambda: fn(x_refs[1]))

    @pl.when(j == 0)
    def _():
        with_x(lambda x_ref: _norm_modulate(x_ref, g_ref, mod_ref, h_ref))
        o_ref[...] = jnp.zeros(o_ref.shape, F32)

    tm, d = h_ref.shape
    rt = tm // row_split
    for r in range(row_split):
        rows = slice(r * rt, (r + 1) * rt)
        h = h_ref[rows, :]
        gate = jnp.dot(h, wg_ref[...], preferred_element_type=F32)
        up = jnp.dot(h, wu_ref[...], preferred_element_type=F32)
        act = (gate * jax.nn.sigmoid(gate) * up).astype(BF16)
        for c in range(d // n_chunk):
            cols = slice(c * n_chunk, (c + 1) * n_chunk)
            part = jnp.dot(act, wo_ref[:, cols], preferred_element_type=F32)

            o_ref[rows, cols] += part

    @pl.when(j == pl.num_programs(1) - 1)
    def _():
        half_gate = 0.5 * mod_ref[0, 2:3, :]

        def finish(x_ref):
            def body(r, carry):
                rows = pl.ds(pl.multiple_of(r * ROW_CHUNK, ROW_CHUNK), ROW_CHUNK)
                xn = x_ref[rows, :] + half_gate * o_ref[rows, :]
                if final:
                    xn = _rms(xn) * fg_ref[...]
                o_ref[rows, :] = xn
                return carry

            lax.fori_loop(0, tm // ROW_CHUNK, body, 0)

        with_x(finish)


def ffn_half_step(xs, mod3, g, w_in, w_out, widx, rows_info, final_g=None):
    d = xs[0].shape[1]
    f = w_out.shape[-2]
    n_prompt_rows, lat_rows = rows_info
    m = sum(x.shape[0] for x in xs)
    tm, tf, n_chunk = _row_tile(rows_info), _tile(f, 512), _tile(d, 512)
    nf = f // tf
    npt = n_prompt_rows // tm
    final = final_g is not None
    l, k = widx
    if len(xs) == 1:
        x_specs = [pl.BlockSpec((tm, d), lambda i, j: (i, 0))]
    else:
        x_specs = [pl.BlockSpec((tm, d), lambda i, j: (jnp.minimum(i, npt - 1), 0), pipeline_mode=pl.Buffered(1)),
                   pl.BlockSpec((tm, d), lambda i, j: (jnp.maximum(i - npt, 0), 0), pipeline_mode=pl.Buffered(1))]
    in_specs = x_specs + [
        pl.BlockSpec((1, 3, d), lambda i, j: (_cond_row(i, tm, n_prompt_rows, lat_rows), 0, 0)),
        pl.BlockSpec((1, d), lambda i, j: (0, 0)),
        pl.BlockSpec((None, None, d, tf), lambda i, j: (l, k, 0, j)),
        pl.BlockSpec((None, None, d, tf), lambda i, j: (l, k, 0, nf + j)),
        pl.BlockSpec((None, None, tf, d), lambda i, j: (l, k, j, 0))]
    args = list(xs) + [mod3, g.reshape(1, d), w_in, w_in, w_out]
    if final:
        in_specs.append(pl.BlockSpec((1, d), lambda i, j: (0, 0)))
        args.append(final_g.reshape(1, d))
    return pl.pallas_call(
        functools.partial(_ffn_kernel, n_x=len(xs), n_prompt_tiles=npt, row_split=max(1, tm // 512),
                          n_chunk=n_chunk, final=final),
        grid=(m // tm, nf),
        in_specs=in_specs,
        out_specs=pl.BlockSpec((tm, d), lambda i, j: (i, 0)),
        out_shape=jax.ShapeDtypeStruct((m, d), F32),
        scratch_shapes=[pltpu.VMEM((tm, d), BF16)],
        compiler_params=_cparams("arbitrary", "arbitrary"),
        name="ffn_final" if final else "ffn",
    )(*args)


def _log_sigmoid(z):
    return jnp.minimum(z, 0.0) - jnp.log(1.0 + jnp.exp(-jnp.abs(z)))


def _proj_kernel(*refs, rope_shift, row_split, with_gates):
    x_ref, mod_ref, g_ref, w_ref, cos_ref, sa_ref, sb_ref = refs[:7]
    if with_gates:
        wl_ref, wa_ref, ba_ref, o_ref, gates_ref, h_ref = refs[7:]
    else:
        o_ref, h_ref = refs[7:]
    @pl.when(pl.program_id(1) == 0)
    def _():
        _norm_modulate(x_ref, g_ref, mod_ref, h_ref)
        if with_gates:
            wl = wl_ref[...]
            wa = wa_ref[...].astype(BF16)

            def body(r, carry):
                rows = pl.ds(pl.multiple_of(r * GATE_ROWS, GATE_ROWS), GATE_ROWS)
                low = jnp.dot(h_ref[rows, :], wl, preferred_element_type=F32)
                z = jnp.dot(low.astype(BF16), wa, preferred_element_type=F32) + ba_ref[...]
                gates_ref[rows, :] = _log_sigmoid(z) * (1.0 / B_TAU)
                return carry

            lax.fori_loop(0, x_ref.shape[0] // GATE_ROWS, body, 0)

    tm, tn = o_ref.shape
    rt = tm // row_split
    for r in range(row_split):
        rows = slice(r * rt, (r + 1) * rt)
        acc = jnp.dot(h_ref[rows, :], w_ref[...], preferred_element_type=F32)
        cos, sa, sb = cos_ref[rows, :], sa_ref[rows, :], sb_ref[rows, :]
        for c in range(tn // 128):
            a = acc[:, c * 128:(c + 1) * 128]
            o_ref[rows, c * 128:(c + 1) * 128] = (a * cos + pltpu.roll(a, 128 - rope_shift, 1) * sa
                                                  + pltpu.roll(a, rope_shift, 1) * sb)


def _rope_tables(n, head_dim, n_sub):
    assert head_dim * n_sub == 128
    rows = n // GRID_W
    row = jnp.repeat(jnp.arange(rows, dtype=F32), GRID_W)
    col = jnp.tile(jnp.arange(GRID_W, dtype=F32), rows)
    d_axis = head_dim // 2
    shift = d_axis // 2
    inv = ROPE_BASE ** (-jnp.arange(0, d_axis, 2, dtype=F32) / d_axis)
    lane = jnp.arange(128)
    sub = lane % head_dim
    is_col = (sub // d_axis) == 1
    within = sub % d_axis
    freq = within % shift
    second = (within // shift) == 1
    pos = jnp.where(is_col[None, :], col[:, None], row[:, None])
    ang = pos * inv[freq][None, :]
    cos, sin = jnp.cos(ang), jnp.sin(ang)
    sin_a = jnp.where(second[None, :], 0.0, -sin)
    sin_b = jnp.where(second[None, :], sin, 0.0)
    return cos, sin_a, sin_b, shift


def mixer_in_proj(x, mod2, g, w, e, n_cols, rows_info, rope, gates=None):
    m, d = x.shape
    n_prompt_rows, lat_rows = rows_info
    tm, tn = _row_tile(rows_info), _tile(n_cols, 512)
    cos, sin_a, sin_b, rope_shift, n_rope_cols = rope
    assert n_rope_cols % tn == 0 and n_prompt_rows % tm == 0
    npt, lat_tiles = n_prompt_rows // tm, lat_rows // tm
    cond = lambda i, j: (_cond_row(i, tm, n_prompt_rows, lat_rows), 0, 0)
    n_rope_tiles = n_rope_cols // tn
    tab = lambda i, j: (jnp.where(jnp.logical_and(i >= npt, j < n_rope_tiles), (i - npt) % lat_tiles, lat_tiles), 0)
    ident = lambda v: jnp.concatenate([v[0], jnp.full((tm, 128), v[1], F32)], axis=0)
    cos, sin_a, sin_b = ident((cos, 1.0)), ident((sin_a, 0.0)), ident((sin_b, 0.0))
    in_specs = [pl.BlockSpec((tm, d), lambda i, j: (i, 0)),
                pl.BlockSpec((1, 2, d), cond),
                pl.BlockSpec((1, d), lambda i, j: (0, 0)),
                pl.BlockSpec((None, d, tn), lambda i, j: (e, 0, j)),
                pl.BlockSpec((tm, 128), tab), pl.BlockSpec((tm, 128), tab), pl.BlockSpec((tm, 128), tab)]
    args = [x, mod2, g.reshape(1, d), w, cos, sin_a, sin_b]
    out_specs = pl.BlockSpec((tm, tn), lambda i, j: (i, j))
    out_shape = jax.ShapeDtypeStruct((m, n_cols), F32)
    if gates is not None:
        w_low, w_alpha, b_alpha = gates
        n_gate = w_alpha.shape[1]
        in_specs += [pl.BlockSpec(w_low.shape, lambda i, j: (0, 0)),
                     pl.BlockSpec(w_alpha.shape, lambda i, j: (0, 0)),
                     pl.BlockSpec((1, n_gate), lambda i, j: (0, 0))]
        args += [w_low, w_alpha, b_alpha.reshape(1, n_gate)]
        out_specs = [out_specs, pl.BlockSpec((tm, n_gate), lambda i, j: (i, 0))]
        out_shape = [out_shape, jax.ShapeDtypeStruct((m, n_gate), F32)]
    return pl.pallas_call(
        functools.partial(_proj_kernel, rope_shift=rope_shift, row_split=max(1, tm // 512),
                          with_gates=gates is not None),
        grid=(m // tm, n_cols // tn),
        in_specs=in_specs, out_specs=out_specs, out_shape=out_shape,
        scratch_shapes=[pltpu.VMEM((tm, d), BF16)],
        compiler_params=_cparams("arbitrary", "arbitrary"),
        name="mixer_in_proj_gated" if gates is not None else "mixer_in_proj",
    )(*args)


def _out_proj_kernel(*refs, n_parts, n_prompt_tiles, row_split):
    parts_p, parts_s = refs[:n_parts], refs[n_parts:2 * n_parts]
    w_refs = refs[2 * n_parts:3 * n_parts]
    x_ref, gate_ref, o_ref = refs[3 * n_parts:]
    i = pl.program_id(0)

    tm = o_ref.shape[0]
    rt = tm // row_split

    def run(parts):
        for r in range(row_split):
            rows = slice(r * rt, (r + 1) * rt)
            acc = None
            for p_ref, w_ref in zip(parts, w_refs):
                t = jnp.dot(p_ref[rows, :], w_ref[...], preferred_element_type=F32)
                acc = t if acc is None else acc + t
            o_ref[rows, :] = x_ref[rows, :] + gate_ref[0] * acc

    pl.when(i < n_prompt_tiles)(lambda: run(parts_p))
    pl.when(i >= n_prompt_tiles)(lambda: run(parts_s))


def mixer_out_proj(parts_p, parts_s, w, e, x, gate, rows_info):
    m, d = x.shape
    n_prompt_rows, lat_rows = rows_info
    tm, tn = _row_tile(rows_info), _tile(d, 1024)
    npt = n_prompt_rows // tm
    n_parts = len(parts_p)
    in_specs, w_specs, off = [], [], 0
    for p in parts_p:
        in_specs.append(pl.BlockSpec((tm, p.shape[1]), lambda i, j: (jnp.minimum(i, npt - 1), 0)))
    for p in parts_s:
        kp = p.shape[1]
        in_specs.append(pl.BlockSpec((tm, kp), lambda i, j: (jnp.maximum(i - npt, 0), 0)))
        assert off % kp == 0
        w_specs.append(pl.BlockSpec((None, kp, tn), functools.partial(lambda i, j, rb: (e, rb, j), rb=off // kp)))
        off += kp
    in_specs += w_specs
    in_specs += [pl.BlockSpec((tm, tn), lambda i, j: (i, j)),
                 pl.BlockSpec((1, 1, tn), lambda i, j: (_cond_row(i, tm, n_prompt_rows, lat_rows), 0, j))]
    return pl.pallas_call(
        functools.partial(_out_proj_kernel, n_parts=n_parts, n_prompt_tiles=npt, row_split=max(1, tm // 512)),
        grid=(m // tm, d // tn),
        in_specs=in_specs,
        out_specs=pl.BlockSpec((tm, tn), lambda i, j: (i, j)),
        out_shape=jax.ShapeDtypeStruct((m, d), F32),
        compiler_params=_cparams("arbitrary", "arbitrary"),
        name="mixer_out_proj",
    )(*parts_p, *parts_s, *([w] * n_parts), x, gate)


def _softmax_parts(parts, extra_logit=None, weight=None):
    m = functools.reduce(jnp.maximum, [jnp.max(p, axis=-1, keepdims=True) for p in parts])
    if extra_logit is not None:
        m = jnp.maximum(m, extra_logit)
    es = [jnp.exp(p - m) for p in parts]
    den = functools.reduce(jnp.add, [jnp.sum(e, axis=-1, keepdims=True) for e in es])
    if extra_logit is not None:
        den = den + jnp.exp(extra_logit - m)
    inv = 1.0 / den if weight is None else weight / den
    return [e * inv for e in es]


def _diff_attn_kernel(*refs, hb, lam_init, has_ctx, emit_kv):
    it = iter(refs)
    lam_ref, g_ref, q_ref, k_ref, v_ref = (next(it) for _ in range(5))
    ck_ref, cv_ref = (next(it), next(it)) if has_ctx else (None, None)
    o_ref = next(it)
    nk_ref, nv_ref = (next(it), next(it)) if emit_kv else (None, None)

    lp = lam_ref[...]
    lam = (jnp.exp(jnp.sum(lp[0:1] * lp[1:2], axis=-1, keepdims=True))
           - jnp.exp(jnp.sum(lp[2:3] * lp[3:4], axis=-1, keepdims=True)) + lam_init)
    scale = A_DH ** -0.5
    assert math.frexp(scale)[0] == 0.5
    first = lax.broadcasted_iota(jnp.int32, (1, 2 * A_DH), 1) < A_DH
    for hh in range(hb):
        cols = slice(hh * 128, (hh + 1) * 128)
        q, k, v = q_ref[:, cols] * scale, k_ref[:, cols], v_ref[:, cols]
        q1 = jnp.where(first, q, 0.0).astype(BF16)
        q2 = jnp.where(first, 0.0, q).astype(BF16)
        keys, vals = [k.astype(BF16)], [v.astype(BF16)]
        if has_ctx:
            keys.insert(0, ck_ref[0, 0, hh].astype(BF16))
            vals.insert(0, cv_ref[0, 0, hh].astype(BF16))
        p1 = _softmax_parts([_dot_nt(q1, kk) for kk in keys])
        p2 = _softmax_parts([_dot_nt(q2, kk) for kk in keys], weight=lam)
        o = None
        for a, b, vv in zip(p1, p2, vals):
            t = jnp.dot((a - b).astype(BF16), vv, preferred_element_type=F32)
            o = t if o is None else o + t
        o_ref[:, cols] = ((_rms(o) * g_ref[...]) * (1.0 - lam_init)).astype(BF16)
        if emit_kv:
            nk_ref[0, 0, hh] = k
            nv_ref[0, 0, hh] = v


def diff_attention(proj, lam_p, subln_g, lam_init, *, row_off, batch, seq, tq, hb, ctx=None, emit_kv=False):
    width = hb * 128
    n_hg = A_HEADS // hb
    assert row_off % seq == 0 and seq % tq == 0
    qrow = lambda b, hg, i: ((row_off + b * seq) // tq + i, hg)
    krow = lambda b, hg, i: ((row_off + b * seq) // seq, A_QK_W // width + hg)
    vrow = lambda b, hg, i: ((row_off + b * seq) // seq, 2 * A_QK_W // width + hg)
    in_specs = [pl.BlockSpec(lam_p.shape, lambda b, hg, i: (0, 0)),
                pl.BlockSpec((1, A_DV), lambda b, hg, i: (0, 0)),
                pl.BlockSpec((tq, width), qrow),
                pl.BlockSpec((seq, width), krow),
                pl.BlockSpec((seq, width), vrow)]
    args = [lam_p, subln_g.reshape(1, A_DV), proj, proj, proj]
    if ctx is not None:
        ctx_k, ctx_v, e = ctx
        past = ctx_k.shape[3]
        cspec = lambda b, hg, i: (b, e, hg, 0, 0)
        in_specs += [pl.BlockSpec((1, 1, hb, past, 2 * A_DH), cspec), pl.BlockSpec((1, 1, hb, past, A_DV), cspec)]
        args += [ctx_k, ctx_v]
    out_specs = [pl.BlockSpec((tq, width), lambda b, hg, i: (b * (seq // tq) + i, hg))]
    out_shape = [jax.ShapeDtypeStruct((batch * seq, A_V_W), BF16)]
    if emit_kv:
        assert tq == seq
        kvspec = pl.BlockSpec((1, 1, hb, seq, 128), lambda b, hg, i: (b, 0, hg, 0, 0))
        out_specs += [kvspec, kvspec]
        out_shape += [jax.ShapeDtypeStruct((batch, 1, A_HEADS, seq, 128), F32)] * 2
    return pl.pallas_call(
        functools.partial(_diff_attn_kernel, hb=hb, lam_init=lam_init, has_ctx=ctx is not None, emit_kv=emit_kv),
        grid=(batch, n_hg, seq // tq),
        in_specs=in_specs, out_specs=out_specs, out_shape=out_shape,
        compiler_params=_cparams("arbitrary", "arbitrary", "arbitrary"),
        name="diff_attn_ctx" if ctx is not None else "diff_attn",
    )(*args)


def _gla_level_matrices(chunk, fwd):
    t = np.arange(chunk)[:, None]
    i = np.arange(chunk)[None, :]
    mats = [i <= t, i > t] if fwd else [i >= t, i < t]
    h = chunk // 2
    while h >= 1:
        p = t % (2 * h)
        base = t - p
        if fwd:
            m = base + h - 1
            a = np.where(p >= h, (i > m) & (i <= t), (i > t) & (i <= m))
        else:
            m = base + h
            a = np.where(p < h, (i >= t) & (i < m), (i >= m) & (i < t))
        mats.append(a)
        h //= 2
    return jnp.asarray(np.concatenate(mats, axis=0).astype(np.float32), dtype=BF16)


def _gla_kernel(*refs, chunk, has_state, emit_state):
    it = iter(refs)
    af_ref, ab_ref, q_ref, k_ref, v_ref, laf_ref, lab_ref, r_ref, g_ref = (next(it) for _ in range(9))
    sf_ref, sb_ref = (next(it), next(it)) if has_state else (None, None)
    o_ref = next(it)
    nsf_ref, nsb_ref = (next(it), next(it)) if emit_state else (None, None)
    accf_ref, accb_ref, stf_ref, stb_ref = (next(it) for _ in range(4))

    seq = q_ref.shape[0]
    n_chunks = seq // chunk
    n_lev = chunk.bit_length() - 1
    ti = lax.broadcasted_iota(jnp.int32, (chunk, chunk), 0)
    si = lax.broadcasted_iota(jnp.int32, (chunk, chunk), 1)
    split = ti ^ si
    scale = B_DK ** -0.5

    def chunk_step(c, fwd, a_ref, la_ref, acc_ref, st_ref):
        rows = c * chunk if isinstance(c, int) else pl.multiple_of(c * chunk, chunk)
        rows = pl.ds(rows, chunk)
        order = (ti > si) if fwd else (ti < si)
        q = q_ref[rows, :] * scale
        k = k_ref[rows, :]
        vb = v_ref[rows, :].astype(BF16)
        la = la_ref[rows, :]
        la_hi = la.astype(BF16)
        la_lo = (la - la_hi.astype(F32)).astype(BF16)
        r2 = jnp.dot(a_ref[...], jnp.concatenate([la_hi, la_lo], axis=1), preferred_element_type=F32)
        e = jnp.exp(r2[:, :B_DK] + r2[:, B_DK:])
        st = st_ref[...]
        inter = _dot_nt((q * e[0:chunk]).astype(BF16), st.astype(BF16))
        ku = (k * e[chunk:2 * chunk]).astype(BF16)
        scores = jnp.where(ti == si, _dot_nt(q.astype(BF16), k.astype(BF16)), 0.0)
        for lev in range(n_lev):
            half = chunk >> (lev + 1)
            f = e[(2 + lev) * chunk:(3 + lev) * chunk]
            sc = _dot_nt((q * f).astype(BF16), (k * f).astype(BF16))
            scores = jnp.where(order & (split >= half) & (split < 2 * half), sc, scores)
        acc_ref[rows, :] = inter + jnp.dot(scores.astype(BF16), vb, preferred_element_type=F32)
        total = e[chunk - 1:chunk] if fwd else e[0:1]
        st_ref[...] = st * total + _dot_tn(vb, ku)

    for s0_ref, st_ref in ((sf_ref, stf_ref), (sb_ref, stb_ref)):
        st_ref[...] = s0_ref[0, 0, 0].T if has_state else jnp.zeros(st_ref.shape, F32)

    def body(cc, carry):
        chunk_step(cc, True, af_ref, laf_ref, accf_ref, stf_ref)
        chunk_step(n_chunks - 1 - cc, False, ab_ref, lab_ref, accb_ref, stb_ref)
        return carry

    if n_chunks == 1:
        body(0, 0)
    else:
        lax.fori_loop(0, n_chunks, body, 0)
    if emit_state:
        nsf_ref[0, 0, 0] = stf_ref[...].T
        nsb_ref[0, 0, 0] = stb_ref[...].T

    def epilogue(r, carry):
        rows = pl.ds(pl.multiple_of(r * ROW_CHUNK, ROW_CHUNK), ROW_CHUNK)
        gate = r_ref[rows, :]
        o = accf_ref[rows, :] + accb_ref[rows, :]
        o_ref[rows, :] = ((_rms(o) * g_ref[...]) * (gate * jax.nn.sigmoid(gate))).astype(BF16)
        return carry

    lax.fori_loop(0, seq // ROW_CHUNK, epilogue, 0)


def gla_bidirectional(proj, gates, bnorm_g, *, row_off, batch, seq, states=None, emit_state=False):
    chunk = min(256, seq)
    assert row_off % seq == 0 and seq % chunk == 0 and chunk & (chunk - 1) == 0
    rb = lambda b: (row_off + b * seq) // seq
    q_off = (2 * A_QK_W + A_V_W) // B_DK
    k_off = q_off + B_HEADS
    v_off = (2 * A_QK_W + A_V_W + 2 * B_QK_W) // B_DV
    r_off = v_off + B_HEADS
    a_f, a_b = _gla_level_matrices(chunk, True), _gla_level_matrices(chunk, False)
    whole = lambda b, h: (0, 0)
    in_specs = [pl.BlockSpec(a_f.shape, whole), pl.BlockSpec(a_b.shape, whole),
                pl.BlockSpec((seq, B_DK), lambda b, h: (rb(b), q_off + h)),
                pl.BlockSpec((seq, B_DK), lambda b, h: (rb(b), k_off + h)),
                pl.BlockSpec((seq, B_DV), lambda b, h: (rb(b), v_off + h)),
                pl.BlockSpec((seq, B_DK), lambda b, h: (rb(b), h)),
                pl.BlockSpec((seq, B_DK), lambda b, h: (rb(b), B_HEADS + h)),
                pl.BlockSpec((seq, B_DV), lambda b, h: (rb(b), r_off + h)),
                pl.BlockSpec((1, B_DV), whole)]
    args = [a_f, a_b, proj, proj, proj, gates, gates, proj, bnorm_g.reshape(1, B_DV)]
    if states is not None:
        s_f, s_b, e = states
        sspec = pl.BlockSpec((1, 1, 1, B_DK, B_DV), lambda b, h: (b, e, h, 0, 0))
        in_specs += [sspec, sspec]
        args += [s_f, s_b]
    out_specs = [pl.BlockSpec((seq, B_DV), lambda b, h: (b, h))]
    out_shape = [jax.ShapeDtypeStruct((batch * seq, B_V_W), BF16)]
    if emit_state:
        nspec = pl.BlockSpec((1, 1, 1, B_DK, B_DV), lambda b, h: (b, 0, h, 0, 0))
        out_specs += [nspec, nspec]
        out_shape += [jax.ShapeDtypeStruct((batch, 1, B_HEADS, B_DK, B_DV), F32)] * 2
    return pl.pallas_call(
        functools.partial(_gla_kernel, chunk=chunk, has_state=states is not None, emit_state=emit_state),
        grid=(batch, B_HEADS),
        in_specs=in_specs, out_specs=out_specs, out_shape=out_shape,
        scratch_shapes=[pltpu.VMEM((seq, B_DV), F32)] * 2 + [pltpu.VMEM((B_DV, B_DK), F32)] * 2,
        compiler_params=_cparams("arbitrary", "arbitrary"),
        name="gla_state" if states is not None else "gla",
    )(*args)


def _sink_attn_kernel(sink_ref, q_ref, k_ref, v_ref, o_ref, nk_ref, nv_ref):
    seq = q_ref.shape[0]
    scale = C_DH ** -0.5
    for hk in range(C_KV_HEADS):
        kcols = slice(hk * C_DH, (hk + 1) * C_DH)
        k, v = k_ref[:, kcols], v_ref[:, kcols]
        q4 = jnp.concatenate([q_ref[:, (hk * C_GROUP + g) * C_DH:(hk * C_GROUP + g + 1) * C_DH]
                              for g in range(C_GROUP)], axis=0).astype(BF16)
        sink = jnp.concatenate([jnp.broadcast_to(sink_ref[hk, 0:1, g:g + 1], (seq, 1)) for g in range(C_GROUP)],
                               axis=0)
        (p,) = _softmax_parts([_dot_nt(q4, k.astype(BF16)) * scale], extra_logit=sink)
        o = jnp.dot(p.astype(BF16), v.astype(BF16), preferred_element_type=F32)
        for g in range(C_GROUP):
            h = hk * C_GROUP + g
            o_ref[:, h * C_DH:(h + 1) * C_DH] = o[g * seq:(g + 1) * seq].astype(BF16)
        nk_ref[0, 0, hk] = k
        nv_ref[0, 0, hk] = v


def sink_attention_context(proj, sink, *, batch, seq):
    sink3 = sink.reshape(C_KV_HEADS, 1, C_GROUP)
    kvspec = pl.BlockSpec((1, 1, C_KV_HEADS, seq, C_DH), lambda b: (b, 0, 0, 0, 0))
    return pl.pallas_call(
        _sink_attn_kernel,
        grid=(batch,),
        in_specs=[pl.BlockSpec(sink3.shape, lambda b: (0, 0, 0)),
                  pl.BlockSpec((seq, C_Q_W), lambda b: (b, 0)),
                  pl.BlockSpec((seq, C_KV_W), lambda b: (b, C_Q_W // C_KV_W)),
                  pl.BlockSpec((seq, C_KV_W), lambda b: (b, C_Q_W // C_KV_W + 1))],
        out_specs=[pl.BlockSpec((seq, C_Q_W), lambda b: (b, 0)), kvspec, kvspec],
        out_shape=[jax.ShapeDtypeStruct((batch * seq, C_Q_W), BF16)]
        + [jax.ShapeDtypeStruct((batch, 1, C_KV_HEADS, seq, C_DH), F32)] * 2,
        compiler_params=_cparams("arbitrary"),
        name="sink_attn",
    )(sink3, proj, proj, proj)


def _window_attn_kernel(sink_ref, q_ref, k_ref, v_ref, ck_ref, cv_ref, o_ref, *, band, hkb):
    tq = q_ref.shape[0]
    seq = k_ref.shape[0]
    scale = C_DH ** -0.5
    i = pl.program_id(2)
    start = pl.multiple_of(jnp.clip(i * tq - C_WINDOW, 0, seq - band), C_WINDOW)
    qpos = i * tq + lax.broadcasted_iota(jnp.int32, (tq, band), 0)
    kpos = start + lax.broadcasted_iota(jnp.int32, (tq, band), 1)
    valid = jnp.abs(qpos - kpos) <= C_WINDOW
    valid4 = jnp.concatenate([valid] * C_GROUP, axis=0)
    gw = C_GROUP * C_DH
    for hk in range(hkb):
        kcols = slice(hk * C_DH, (hk + 1) * C_DH)
        kb = k_ref[pl.ds(start, band), kcols].astype(BF16)
        vb = v_ref[pl.ds(start, band), kcols].astype(BF16)
        q4 = jnp.concatenate([q_ref[:, hk * gw + g * C_DH:hk * gw + (g + 1) * C_DH] for g in range(C_GROUP)],
                             axis=0).astype(BF16)
        sink = jnp.concatenate([jnp.broadcast_to(sink_ref[hk, 0:1, g:g + 1], (tq, 1)) for g in range(C_GROUP)],
                               axis=0)
        s_band = jnp.where(valid4, _dot_nt(q4, kb) * scale, -jnp.inf)
        s_ctx = _dot_nt(q4, ck_ref[0, 0, hk].astype(BF16)) * scale
        p_ctx, p_band = _softmax_parts([s_ctx, s_band], extra_logit=sink)
        o = (jnp.dot(p_ctx.astype(BF16), cv_ref[0, 0, hk].astype(BF16), preferred_element_type=F32)
             + jnp.dot(p_band.astype(BF16), vb, preferred_element_type=F32))
        for g in range(C_GROUP):
            o_ref[:, hk * gw + g * C_DH:hk * gw + (g + 1) * C_DH] = o[g * tq:(g + 1) * tq].astype(BF16)


def window_attention_latent(proj, sink, ctx_k, ctx_v, e, *, row_off, batch, seq, tq=256, hkb=2):
    tq = min(tq, seq)
    band = min(tq + 2 * C_WINDOW, seq)
    assert row_off % seq == 0 and seq % tq == 0 and tq % C_WINDOW == 0 and C_KV_HEADS % hkb == 0
    past = ctx_k.shape[3]
    qw, kw = hkb * C_GROUP * C_DH, hkb * C_DH
    rb = lambda b: (row_off + b * seq) // seq
    cspec = pl.BlockSpec((1, 1, hkb, past, C_DH), lambda b, hg, i: (b, e, hg, 0, 0))
    sink3 = sink.reshape(C_KV_HEADS, 1, C_GROUP)
    return pl.pallas_call(
        functools.partial(_window_attn_kernel, band=band, hkb=hkb),
        grid=(batch, C_KV_HEADS // hkb, seq // tq),
        in_specs=[pl.BlockSpec((hkb, 1, C_GROUP), lambda b, hg, i: (hg, 0, 0)),
                  pl.BlockSpec((tq, qw), lambda b, hg, i: ((row_off + b * seq) // tq + i, hg)),
                  pl.BlockSpec((seq, kw), lambda b, hg, i: (rb(b), C_Q_W // kw + hg)),
                  pl.BlockSpec((seq, kw), lambda b, hg, i: (rb(b), (C_Q_W + C_KV_W) // kw + hg)),
                  cspec, cspec],
        out_specs=pl.BlockSpec((tq, qw), lambda b, hg, i: (b * (seq // tq) + i, hg)),
        out_shape=jax.ShapeDtypeStruct((batch * seq, C_Q_W), BF16),
        compiler_params=_cparams("arbitrary", "arbitrary", "arbitrary"),
        name="window_attn",
    )(sink3, proj, proj, proj, ctx_k, ctx_v)


def kernel(x_prompt, x_sample, cache_a_k, cache_a_v, state_b_fwd, state_b_bwd, cache_c_k, cache_c_v, c, c_ctx, ada_w, ada_b, norm_g, ffn_w_in, ffn_w_out, ab_w_in, ab_w_out, a_lambda, a_subln_g, b_alpha_w, b_alpha_b, b_norm_g, c_w_in, c_w_out, c_sink, final_g):
    bp, sp, d = x_prompt.shape
    bs, ss, _ = x_sample.shape
    depth = ada_w.shape[0]
    mp, ms = bp * sp, bs * ss
    rows_info = (mp, ss)

    cond = jnp.concatenate([c_ctx[None, :], c, jnp.zeros((COND_ROWS - 1 - bs, d), F32)], axis=0)
    mods = adaln(cond, ada_w, ada_b).reshape(depth, COND_ROWS, N_MOD, d)
    rope_a = _rope_tables(ss, A_DH, 2) + (2 * A_QK_W,)
    rope_c = _rope_tables(ss, C_DH, 1) + (C_Q_W + C_KV_W,)
    ffn_w_in, ffn_w_out, ab_w_in, ab_w_out, c_w_in, c_w_out = (
        w.astype(BF16) for w in (ffn_w_in, ffn_w_out, ab_w_in, ab_w_out, c_w_in, c_w_out))

    a_k, a_v, b_f, b_b, c_k, c_v = [], [], [], [], [], []
    xs = [x_prompt.reshape(mp, d), x_sample.reshape(ms, d)]
    for l in range(depth):
        mod = mods[l]
        x = ffn_half_step(xs, mod[:, 0:3], norm_g[l, 0], ffn_w_in, ffn_w_out, (l, 0), rows_info)
        if l % 2 == 0:
            e = l // 2
            lam_init = 0.8 - 0.6 * math.exp(-0.3 * l)
            zeros = jnp.zeros((B_RANK, B_QK_W), F32)
            w_alpha = jnp.concatenate([jnp.concatenate([b_alpha_w[e, 0], zeros], axis=1),
                                       jnp.concatenate([zeros, b_alpha_w[e, 1]], axis=1)], axis=0)
            gate_w = (ab_w_in[e][:, AB_MAIN:], w_alpha, b_alpha_b[e].reshape(-1))
            proj, gates = mixer_in_proj(x, mod[:, 3:5], norm_g[l, 1], ab_w_in, e, AB_MAIN, rows_info, rope_a,
                                        gates=gate_w)
            attn_p, ak, av = diff_attention(proj, a_lambda[e], a_subln_g[e], lam_init, row_off=0, batch=bp,
                                            seq=sp, tq=sp, hb=A_HEADS, emit_kv=True)
            (attn_s,) = diff_attention(proj, a_lambda[e], a_subln_g[e], lam_init, row_off=mp, batch=bs,
                                       seq=ss, tq=min(256, ss), hb=2, ctx=(cache_a_k, cache_a_v, e))
            gla_p, sf, sb = gla_bidirectional(proj, gates, b_norm_g[e], row_off=0, batch=bp, seq=sp,
                                              emit_state=True)
            (gla_s,) = gla_bidirectional(proj, gates, b_norm_g[e], row_off=mp, batch=bs, seq=ss,
                                         states=(state_b_fwd, state_b_bwd, e))
            a_k.append(ak), a_v.append(av), b_f.append(sf), b_b.append(sb)
            x = mixer_out_proj([attn_p, gla_p], [attn_s, gla_s], ab_w_out, e, x, mod[:, 5:6], rows_info)
        else:
            o = l // 2
            proj = mixer_in_proj(x, mod[:, 3:5], norm_g[l, 1], c_w_in, o, c_w_in.shape[2], rows_info, rope_c)
            mix_p, ck, cv = sink_attention_context(proj, c_sink[o], batch=bp, seq=sp)
            mix_s = window_attention_latent(proj, c_sink[o], cache_c_k, cache_c_v, o, row_off=mp, batch=bs, seq=ss)
            c_k.append(ck), c_v.append(cv)
            x = mixer_out_proj([mix_p], [mix_s], c_w_out, o, x, mod[:, 5:6], rows_info)
        x = ffn_half_step([x], mod[:, 6:9], norm_g[l, 2], ffn_w_in, ffn_w_out, (l, 1), rows_info,
                          final_g=final_g if l == depth - 1 else None)
        xs = [x]

    y_prompt, y_sample = x[:mp].reshape(bp, sp, d), x[mp:].reshape(bs, ss, d)
    cat = lambda parts: parts[0] if len(parts) == 1 else jnp.concatenate(parts, axis=1)
    return (y_prompt, y_sample, cat(a_k), cat(a_v), cat(b_f), cat(b_b), cat(c_k), cat(c_v))
```

```python
import functools
import math

import numpy as np
import jax
import jax.numpy as jnp
from jax import lax
from jax.experimental import pallas as pl
from jax.experimental.pallas import tpu as pltpu

F32 = jnp.float32
BF16 = jnp.bfloat16

EPS = 1e-6
GRID_W = 64
ROPE_BASE = 10000.0
N_MOD = 9
A_HEADS, A_DH, A_DV = 8, 64, 128
B_HEADS, B_DK, B_DV, B_RANK, B_TAU = 4, 128, 256, 16, 16.0
C_HEADS, C_KV_HEADS, C_DH, C_WINDOW = 16, 4, 128, 128
A_QK_W = A_HEADS * 2 * A_DH
A_V_W = A_HEADS * A_DV
B_QK_W = B_HEADS * B_DK
B_V_W = B_HEADS * B_DV
AB_MAIN = 2 * A_QK_W + A_V_W + 2 * B_QK_W + 2 * B_V_W
C_GROUP = C_HEADS // C_KV_HEADS
C_Q_W = C_HEADS * C_DH
C_KV_W = C_KV_HEADS * C_DH

V7X_VMEM_BYTES = 64 * 1024 * 1024
VMEM_LIMIT_BYTES = V7X_VMEM_BYTES - 8 * 1024 * 1024
COND_ROWS = 8
ROW_CHUNK = 128


def _tile(n, pref, align=128):
    if n <= pref:
        return n
    t = (pref // align) * align
    while n % t:
        t -= align
    assert t > 0, (n, pref)
    return t


def _cparams(*sem):
    return pltpu.CompilerParams(dimension_semantics=sem, vmem_limit_bytes=VMEM_LIMIT_BYTES)


def _bdot(a, b):
    return jnp.dot(a.astype(BF16), b.astype(BF16), preferred_element_type=F32)


def _dot_nt(a, b):
    return lax.dot_general(a, b, (((1,), (1,)), ((), ())), preferred_element_type=F32)


def _dot_tn(a, b):
    return lax.dot_general(a, b, (((0,), (0,)), ((), ())), preferred_element_type=F32)


def _rms(x):
    return x * lax.rsqrt(jnp.mean(x * x, axis=-1, keepdims=True) + EPS)


def _adaln_kernel(cond_ref, w_ref, b_ref, o_ref):
    c = cond_ref[...]
    o_ref[0] = _bdot(c * jax.nn.sigmoid(c), w_ref[0]) + b_ref[0]


def adaln(cond, ada_w, ada_b):
    depth, d, n = ada_w.shape
    tn = _tile(n, 1024)
    return pl.pallas_call(
        _adaln_kernel,
        grid=(depth, n // tn),
        in_specs=[pl.BlockSpec((COND_ROWS, d), lambda l, j: (0, 0)),
                  pl.BlockSpec((1, d, tn), lambda l, j: (l, 0, j)),
                  pl.BlockSpec((1, 1, tn), lambda l, j: (l, 0, j))],
        out_specs=pl.BlockSpec((1, COND_ROWS, tn), lambda l, j: (l, 0, j)),
        out_shape=jax.ShapeDtypeStruct((depth, COND_ROWS, n), F32),
        compiler_params=_cparams("arbitrary", "arbitrary"),
        name="adaln",
    )(cond, ada_w, ada_b.reshape(depth, 1, n))


def _norm_modulate(x_ref, g_ref, mod_ref, h_ref):
    g = g_ref[...]
    shift = mod_ref[0, 0:1, :]
    scale1 = 1.0 + mod_ref[0, 1:2, :]

    def body(r, carry):
        rows = pl.ds(pl.multiple_of(r * ROW_CHUNK, ROW_CHUNK), ROW_CHUNK)
        h_ref[rows, :] = ((_rms(x_ref[rows, :]) * g) * scale1 + shift).astype(BF16)
        return carry

    lax.fori_loop(0, x_ref.shape[0] // ROW_CHUNK, body, 0)


def _row_tile(rows_info):
    n_prompt_rows, lat_rows = rows_info
    return _tile(math.gcd(n_prompt_rows, lat_rows), 1024, align=ROW_CHUNK)


def _cond_row(i, tm, n_prompt_rows, lat_rows):
    return jnp.maximum((i * tm - n_prompt_rows) // lat_rows + 1, 0)


def _ffn_kernel(*refs, n_x, n_prompt_tiles, row_split, n_chunk, final):
    x_refs, refs = refs[:n_x], refs[n_x:]
    mod_ref, g_ref, wg_ref, wu_ref, wo_ref = refs[:5]
    refs = refs[5:]
    if final:
        fg_ref, refs = refs[0], refs[1:]
    o_ref, h_ref = refs
    i, j = pl.program_id(0), pl.program_id(1)

    def with_x(fn):
        if n_x == 1:
            fn(x_refs[0])
        else:
            pl.when(i < n_prompt_tiles)(lambda: fn(x_refs[0]))
            pl.when(i >= n_prompt_tiles)(lambda: fn(x_refs[1]))

    @pl.when(j == 0)
    def _():
        with_x(lambda x_ref: _norm_modulate(x_ref, g_ref, mod_ref, h_ref))
        o_ref[...] = jnp.zeros(o_ref.shape, F32)

    tm, d = h_ref.shape
    rt = tm // row_split
    for r in range(row_split):
        rows = slice(r * rt, (r + 1) * rt)
        h = h_ref[rows, :]
        gate = jnp.dot(h, wg_ref[...], preferred_element_type=F32)
        up = jnp.dot(h, wu_ref[...], preferred_element_type=F32)
        act = (gate * jax.nn.sigmoid(gate) * up).astype(BF16)
        for c in range(d // n_chunk):
            cols = slice(c * n_chunk, (c + 1) * n_chunk)
            part = jnp.dot(act, wo_ref[:, cols], preferred_element_type=F32)

            o_ref[rows, cols] += part

    @pl.when(j == pl.num_programs(1) - 1)
    def _():
        half_gate = 0.5 * mod_ref[0, 2:3, :]

        def finish(x_ref):
            def body(r, carry):
                rows = pl.ds(pl.multiple_of(r * ROW_CHUNK, ROW_CHUNK), ROW_CHUNK)
                xn = x_ref[rows, :] + half_gate * o_ref[rows, :]
                if final:
                    xn = _rms(xn) * fg_ref[...]
                o_ref[rows, :] = xn
                return carry

            lax.fori_loop(0, tm // ROW_CHUNK, body, 0)

        with_x(finish)


def ffn_half_step(xs, mod3, g, w_in, w_out, widx, rows_info, final_g=None):
    d = xs[0].shape[1]
    f = w_out.shape[-2]
    n_prompt_rows, lat_rows = rows_info
    m = sum(x.shape[0] for x in xs)
    tm, tf, n_chunk = _row_tile(rows_info), _tile(f, 512), _tile(d, 512)
    nf = f // tf
    npt = n_prompt_rows // tm
    final = final_g is not None
    l, k = widx
    if len(xs) == 1:
        x_specs = [pl.BlockSpec((tm, d), lambda i, j: (i, 0))]
    else:
        x_specs = [pl.BlockSpec((tm, d), lambda i, j: (jnp.minimum(i, npt - 1), 0), pipeline_mode=pl.Buffered(1)),
                   pl.BlockSpec((tm, d), lambda i, j: (jnp.maximum(i - npt, 0), 0), pipeline_mode=pl.Buffered(1))]
    in_specs = x_specs + [
        pl.BlockSpec((1, 3, d), lambda i, j: (_cond_row(i, tm, n_prompt_rows, lat_rows), 0, 0)),
        pl.BlockSpec((1, d), lambda i, j: (0, 0)),
        pl.BlockSpec((None, None, d, tf), lambda i, j: (l, k, 0, j)),
        pl.BlockSpec((None, None, d, tf), lambda i, j: (l, k, 0, nf + j)),
        pl.BlockSpec((None, None, tf, d), lambda i, j: (l, k, j, 0))]
    args = list(xs) + [mod3, g.reshape(1, d), w_in, w_in, w_out]
    if final:
        in_specs.append(pl.BlockSpec((1, d), lambda i, j: (0, 0)))
        args.append(final_g.reshape(1, d))
    return pl.pallas_call(
        functools.partial(_ffn_kernel, n_x=len(xs), n_prompt_tiles=npt, row_split=max(1, tm // 512),
                          n_chunk=n_chunk, final=final),
        grid=(m // tm, nf),
        in_specs=in_specs,
        out_specs=pl.BlockSpec((tm, d), lambda i, j: (i, 0)),
        out_shape=jax.ShapeDtypeStruct((m, d), F32),
        scratch_shapes=[pltpu.VMEM((tm, d), BF16)],
        compiler_params=_cparams("arbitrary", "arbitrary"),
        name="ffn_final" if final else "ffn",
    )(*args)


def _log_sigmoid(z):
    return jnp.minimum(z, 0.0) - jnp.log(1.0 + jnp.exp(-jnp.abs(z)))


def _proj_kernel(*refs, n_prompt_tiles, n_rope_cols, rope_shift, tn, with_gates):
    x_ref, mod_ref, g_ref, w_ref, cos_ref, sa_ref, sb_ref = refs[:7]
    if with_gates:
        wl_ref, wa_ref, ba_ref, o_ref, gates_ref = refs[7:]
    else:
        (o_ref,) = refs[7:]
    x = x_ref[...]
    h = (((_rms(x) * g_ref[...]) * (1.0 + mod_ref[0, 1:2, :])) + mod_ref[0, 0:1, :]).astype(BF16)
    if with_gates:
        low = jnp.dot(h, wl_ref[...], preferred_element_type=F32)
        z = jnp.dot(low.astype(BF16), wa_ref[...].astype(BF16), preferred_element_type=F32) + ba_ref[...]
        gates_ref[...] = _log_sigmoid(z) * (1.0 / B_TAU)

    def project(rotary):
        for c in range(o_ref.shape[1] // tn):
            acc = jnp.dot(h, w_ref[:, c * tn:(c + 1) * tn], preferred_element_type=F32)
            if rotary and c * tn < n_rope_cols:
                cos, sa, sb = cos_ref[...], sa_ref[...], sb_ref[...]
                for s in range(tn // 128):
                    a = acc[:, s * 128:(s + 1) * 128]
                    o_ref[:, c * tn + s * 128:c * tn + (s + 1) * 128] = (
                        a * cos + pltpu.roll(a, 128 - rope_shift, 1) * sa + pltpu.roll(a, rope_shift, 1) * sb)
            else:
                o_ref[:, c * tn:(c + 1) * tn] = acc

    i = pl.program_id(0)
    pl.when(i < n_prompt_tiles)(lambda: project(False))
    pl.when(i >= n_prompt_tiles)(lambda: project(True))


def _rope_tables(n, head_dim, n_sub):
    assert head_dim * n_sub == 128
    rows = n // GRID_W
    row = jnp.repeat(jnp.arange(rows, dtype=F32), GRID_W)
    col = jnp.tile(jnp.arange(GRID_W, dtype=F32), rows)
    d_axis = head_dim // 2
    shift = d_axis // 2
    inv = ROPE_BASE ** (-jnp.arange(0, d_axis, 2, dtype=F32) / d_axis)
    lane = jnp.arange(128)
    sub = lane % head_dim
    is_col = (sub // d_axis) == 1
    within = sub % d_axis
    freq = within % shift
    second = (within // shift) == 1
    pos = jnp.where(is_col[None, :], col[:, None], row[:, None])
    ang = pos * inv[freq][None, :]
    cos, sin = jnp.cos(ang), jnp.sin(ang)
    sin_a = jnp.where(second[None, :], 0.0, -sin)
    sin_b = jnp.where(second[None, :], sin, 0.0)
    return cos, sin_a, sin_b, shift


def mixer_in_proj(x, mod2, g, w, e, n_cols, rows_info, rope, gates=None):
    m, d = x.shape
    n_prompt_rows, lat_rows = rows_info
    cos, sin_a, sin_b, rope_shift, n_rope_cols = rope
    tm = _tile(math.gcd(n_prompt_rows, lat_rows), 256, align=ROW_CHUNK)
    tn = _tile(math.gcd(n_cols, n_rope_cols), 512)
    npt, lat_tiles = n_prompt_rows // tm, lat_rows // tm
    tab = lambda i: (jnp.maximum(i - npt, 0) % lat_tiles, 0)
    once = dict(pipeline_mode=pl.Buffered(1))
    in_specs = [pl.BlockSpec((tm, d), lambda i: (i, 0)),
                pl.BlockSpec((1, 2, d), lambda i: (_cond_row(i, tm, n_prompt_rows, lat_rows), 0, 0)),
                pl.BlockSpec((1, d), lambda i: (0, 0)),
                pl.BlockSpec((None, d, n_cols), lambda i: (e, 0, 0), **once),
                pl.BlockSpec((tm, 128), tab), pl.BlockSpec((tm, 128), tab), pl.BlockSpec((tm, 128), tab)]
    args = [x, mod2, g.reshape(1, d), w, cos, sin_a, sin_b]
    out_specs = pl.BlockSpec((tm, n_cols), lambda i: (i, 0))
    out_shape = jax.ShapeDtypeStruct((m, n_cols), F32)
    if gates is not None:
        w_low, w_alpha, b_alpha = gates
        n_gate = w_alpha.shape[1]
        in_specs += [pl.BlockSpec(w_low.shape, lambda i: (0, 0)),
                     pl.BlockSpec(w_alpha.shape, lambda i: (0, 0)),
                     pl.BlockSpec((1, n_gate), lambda i: (0, 0))]
        args += [w_low, w_alpha, b_alpha.reshape(1, n_gate)]
        out_specs = [out_specs, pl.BlockSpec((tm, n_gate), lambda i: (i, 0))]
        out_shape = [out_shape, jax.ShapeDtypeStruct((m, n_gate), F32)]
    return pl.pallas_call(
        functools.partial(_proj_kernel, n_prompt_tiles=npt, n_rope_cols=n_rope_cols, rope_shift=rope_shift,
                          tn=tn, with_gates=gates is not None),
        grid=(m // tm,),
        in_specs=in_specs, out_specs=out_specs, out_shape=out_shape,
        compiler_params=_cparams("arbitrary"),
        name="mixer_in_proj_gated" if gates is not None else "mixer_in_proj",
    )(*args)


def _out_proj_kernel(*refs, n_parts, n_prompt_tiles, tn):
    parts_p, parts_s = refs[:n_parts], refs[n_parts:2 * n_parts]
    w_refs = refs[2 * n_parts:3 * n_parts]
    x_ref, gate_ref, o_ref = refs[3 * n_parts:]
    i = pl.program_id(0)

    def run(parts):
        for c in range(o_ref.shape[1] // tn):
            cols = slice(c * tn, (c + 1) * tn)
            acc = None
            for p_ref, w_ref in zip(parts, w_refs):
                t = jnp.dot(p_ref[...], w_ref[:, cols], preferred_element_type=F32)
                acc = t if acc is None else acc + t
            o_ref[:, cols] = x_ref[:, cols] + gate_ref[0, :, cols] * acc

    pl.when(i < n_prompt_tiles)(lambda: run(parts_p))
    pl.when(i >= n_prompt_tiles)(lambda: run(parts_s))


def mixer_out_proj(parts_p, parts_s, w, e, x, gate, rows_info):
    m, d = x.shape
    n_prompt_rows, lat_rows = rows_info
    tm = _tile(math.gcd(n_prompt_rows, lat_rows), 512, align=ROW_CHUNK)
    npt = n_prompt_rows // tm
    n_parts = len(parts_p)
    in_specs, w_specs, off = [], [], 0
    for p in parts_p:
        in_specs.append(pl.BlockSpec((tm, p.shape[1]), lambda i: (jnp.minimum(i, npt - 1), 0)))
    for p in parts_s:
        kp = p.shape[1]
        in_specs.append(pl.BlockSpec((tm, kp), lambda i: (jnp.maximum(i - npt, 0), 0)))
        assert off % kp == 0
        w_specs.append(pl.BlockSpec((None, kp, d), functools.partial(lambda i, rb: (e, rb, 0), rb=off // kp),
                                    pipeline_mode=pl.Buffered(1)))
        off += kp
    in_specs += w_specs
    in_specs += [pl.BlockSpec((tm, d), lambda i: (i, 0)),
                 pl.BlockSpec((1, 1, d), lambda i: (_cond_row(i, tm, n_prompt_rows, lat_rows), 0, 0))]
    return pl.pallas_call(
        functools.partial(_out_proj_kernel, n_parts=n_parts, n_prompt_tiles=npt, tn=_tile(d, 512)),
        grid=(m // tm,),
        in_specs=in_specs,
        out_specs=pl.BlockSpec((tm, d), lambda i: (i, 0)),
        out_shape=jax.ShapeDtypeStruct((m, d), F32),
        compiler_params=_cparams("arbitrary"),
        name="mixer_out_proj",
    )(*parts_p, *parts_s, *([w] * n_parts), x, gate)


def _softmax_parts(parts, extra_logit=None, weight=None):
    m = functools.reduce(jnp.maximum, [jnp.max(p, axis=-1, keepdims=True) for p in parts])
    if extra_logit is not None:
        m = jnp.maximum(m, extra_logit)
    es = [jnp.exp(p - m) for p in parts]
    den = functools.reduce(jnp.add, [jnp.sum(e, axis=-1, keepdims=True) for e in es])
    if extra_logit is not None:
        den = den + jnp.exp(extra_logit - m)
    inv = 1.0 / den if weight is None else weight / den
    return [e * inv for e in es]


def _diff_attn_kernel(*refs, hb, lam_init, has_ctx, emit_kv):
    it = iter(refs)
    lam_ref, g_ref, q_ref, k_ref, v_ref = (next(it) for _ in range(5))
    ck_ref, cv_ref = (next(it), next(it)) if has_ctx else (None, None)
    o_ref = next(it)
    nk_ref, nv_ref = (next(it), next(it)) if emit_kv else (None, None)

    lp = lam_ref[...]
    lam = (jnp.exp(jnp.sum(lp[0:1] * lp[1:2], axis=-1, keepdims=True))
           - jnp.exp(jnp.sum(lp[2:3] * lp[3:4], axis=-1, keepdims=True)) + lam_init)
    scale = A_DH ** -0.5
    assert math.frexp(scale)[0] == 0.5
    first = lax.broadcasted_iota(jnp.int32, (1, 2 * A_DH), 1) < A_DH
    for hh in range(hb):
        cols = slice(hh * 128, (hh + 1) * 128)
        q, k, v = q_ref[:, cols] * scale, k_ref[:, cols], v_ref[:, cols]
        q1 = jnp.where(first, q, 0.0).astype(BF16)
        q2 = jnp.where(first, 0.0, q).astype(BF16)
        keys, vals = [k.astype(BF16)], [v.astype(BF16)]
        if has_ctx:
            keys.insert(0, ck_ref[0, 0, hh].astype(BF16))
            vals.insert(0, cv_ref[0, 0, hh].astype(BF16))
        p1 = _softmax_parts([_dot_nt(q1, kk) for kk in keys])
        p2 = _softmax_parts([_dot_nt(q2, kk) for kk in keys], weight=lam)
        o = None
        for a, b, vv in zip(p1, p2, vals):
            t = jnp.dot((a - b).astype(BF16), vv, preferred_element_type=F32)
            o = t if o is None else o + t
        o_ref[:, cols] = ((_rms(o) * g_ref[...]) * (1.0 - lam_init)).astype(BF16)
        if emit_kv:
            nk_ref[0, 0, hh] = k
            nv_ref[0, 0, hh] = v


def diff_attention(proj, lam_p, subln_g, lam_init, *, row_off, batch, seq, tq, hb, ctx=None, emit_kv=False):
    width = hb * 128
    n_hg = A_HEADS // hb
    assert row_off % seq == 0 and seq % tq == 0
    qrow = lambda b, hg, i: ((row_off + b * seq) // tq + i, hg)
    krow = lambda b, hg, i: ((row_off + b * seq) // seq, A_QK_W // width + hg)
    vrow = lambda b, hg, i: ((row_off + b * seq) // seq, 2 * A_QK_W // width + hg)
    in_specs = [pl.BlockSpec(lam_p.shape, lambda b, hg, i: (0, 0)),
                pl.BlockSpec((1, A_DV), lambda b, hg, i: (0, 0)),
                pl.BlockSpec((tq, width), qrow),
                pl.BlockSpec((seq, width), krow),
                pl.BlockSpec((seq, width), vrow)]
    args = [lam_p, subln_g.reshape(1, A_DV), proj, proj, proj]
    if ctx is not None:
        ctx_k, ctx_v, e = ctx
        past = ctx_k.shape[3]
        cspec = lambda b, hg, i: (b, e, hg, 0, 0)
        in_specs += [pl.BlockSpec((1, 1, hb, past, 2 * A_DH), cspec), pl.BlockSpec((1, 1, hb, past, A_DV), cspec)]
        args += [ctx_k, ctx_v]
    out_specs = [pl.BlockSpec((tq, width), lambda b, hg, i: (b * (seq // tq) + i, hg))]
    out_shape = [jax.ShapeDtypeStruct((batch * seq, A_V_W), BF16)]
    if emit_kv:
        assert tq == seq
        kvspec = pl.BlockSpec((1, 1, hb, seq, 128), lambda b, hg, i: (b, 0, hg, 0, 0))
        out_specs += [kvspec, kvspec]
        out_shape += [jax.ShapeDtypeStruct((batch, 1, A_HEADS, seq, 128), F32)] * 2
    return pl.pallas_call(
        functools.partial(_diff_attn_kernel, hb=hb, lam_init=lam_init, has_ctx=ctx is not None, emit_kv=emit_kv),
        grid=(batch, n_hg, seq // tq),
        in_specs=in_specs, out_specs=out_specs, out_shape=out_shape,
        compiler_params=_cparams("arbitrary", "arbitrary", "arbitrary"),
        name="diff_attn_ctx" if ctx is not None else "diff_attn",
    )(*args)


def _gla_level_matrices(chunk, fwd):
    t = np.arange(chunk)[:, None]
    i = np.arange(chunk)[None, :]
    mats = [i <= t, i > t] if fwd else [i >= t, i < t]
    h = chunk // 2
    while h >= 1:
        p = t % (2 * h)
        base = t - p
        if fwd:
            m = base + h - 1
            a = np.where(p >= h, (i > m) & (i <= t), (i > t) & (i <= m))
        else:
            m = base + h
            a = np.where(p < h, (i >= t) & (i < m), (i >= m) & (i < t))
        mats.append(a)
        h //= 2
    return jnp.asarray(np.concatenate(mats, axis=0).astype(np.float32), dtype=BF16)


def _gla_kernel(*refs, chunk, has_state, emit_state):
    it = iter(refs)
    af_ref, ab_ref, q_ref, k_ref, v_ref, laf_ref, lab_ref, r_ref, g_ref = (next(it) for _ in range(9))
    sf_ref, sb_ref = (next(it), next(it)) if has_state else (None, None)
    o_ref = next(it)
    nsf_ref, nsb_ref = (next(it), next(it)) if emit_state else (None, None)
    accf_ref, accb_ref, stf_ref, stb_ref = (next(it) for _ in range(4))

    seq = q_ref.shape[0]
    n_chunks = seq // chunk
    n_lev = chunk.bit_length() - 1
    ti = lax.broadcasted_iota(jnp.int32, (chunk, chunk), 0)
    si = lax.broadcasted_iota(jnp.int32, (chunk, chunk), 1)
    split = ti ^ si
    scale = B_DK ** -0.5

    def chunk_step(c, fwd, a_ref, la_ref, acc_ref, st_ref):
        rows = c * chunk if isinstance(c, int) else pl.multiple_of(c * chunk, chunk)
        rows = pl.ds(rows, chunk)
        order = (ti > si) if fwd else (ti < si)
        q = q_ref[rows, :] * scale
        k = k_ref[rows, :]
        vb = v_ref[rows, :].astype(BF16)
        la = la_ref[rows, :]
        la_hi = la.astype(BF16)
        la_lo = (la - la_hi.astype(F32)).astype(BF16)
        r2 = jnp.dot(a_ref[...], jnp.concatenate([la_hi, la_lo], axis=1), preferred_element_type=F32)
        e = jnp.exp(r2[:, :B_DK] + r2[:, B_DK:])
        st = st_ref[...]
        inter = _dot_nt((q * e[0:chunk]).astype(BF16), st.astype(BF16))
        ku = (k * e[chunk:2 * chunk]).astype(BF16)
        scores = jnp.where(ti == si, _dot_nt(q.astype(BF16), k.astype(BF16)), 0.0)
        for lev in range(n_lev):
            half = chunk >> (lev + 1)
            f = e[(2 + lev) * chunk:(3 + lev) * chunk]
            sc = _dot_nt((q * f).astype(BF16), (k * f).astype(BF16))
            scores = jnp.where(order & (split >= half) & (split < 2 * half), sc, scores)
        acc_ref[rows, :] = inter + jnp.dot(scores.astype(BF16), vb, preferred_element_type=F32)
        total = e[chunk - 1:chunk] if fwd else e[0:1]
        st_ref[...] = st * total + _dot_tn(vb, ku)

    for s0_ref, st_ref in ((sf_ref, stf_ref), (sb_ref, stb_ref)):
        st_ref[...] = s0_ref[0, 0, 0].T if has_state else jnp.zeros(st_ref.shape, F32)

    def body(cc, carry):
        chunk_step(cc, True, af_ref, laf_ref, accf_ref, stf_ref)
        chunk_step(n_chunks - 1 - cc, False, ab_ref, lab_ref, accb_ref, stb_ref)
        return carry

    if n_chunks == 1:
        body(0, 0)
    else:
        lax.fori_loop(0, n_chunks, body, 0)
    if emit_state:
        nsf_ref[0, 0, 0] = stf_ref[...].T
        nsb_ref[0, 0, 0] = stb_ref[...].T

    def epilogue(r, carry):
        rows = pl.ds(pl.multiple_of(r * ROW_CHUNK, ROW_CHUNK), ROW_CHUNK)
        gate = r_ref[rows, :]
        o = accf_ref[rows, :] + accb_ref[rows, :]
        o_ref[rows, :] = ((_rms(o) * g_ref[...]) * (gate * jax.nn.sigmoid(gate))).astype(BF16)
        return carry

    lax.fori_loop(0, seq // ROW_CHUNK, epilogue, 0)


def gla_bidirectional(proj, gates, bnorm_g, *, row_off, batch, seq, states=None, emit_state=False):
    chunk = min(256, seq)
    assert row_off % seq == 0 and seq % chunk == 0 and chunk & (chunk - 1) == 0
    rb = lambda b: (row_off + b * seq) // seq
    q_off = (2 * A_QK_W + A_V_W) // B_DK
    k_off = q_off + B_HEADS
    v_off = (2 * A_QK_W + A_V_W + 2 * B_QK_W) // B_DV
    r_off = v_off + B_HEADS
    a_f, a_b = _gla_level_matrices(chunk, True), _gla_level_matrices(chunk, False)
    whole = lambda b, h: (0, 0)
    in_specs = [pl.BlockSpec(a_f.shape, whole), pl.BlockSpec(a_b.shape, whole),
                pl.BlockSpec((seq, B_DK), lambda b, h: (rb(b), q_off + h)),
                pl.BlockSpec((seq, B_DK), lambda b, h: (rb(b), k_off + h)),
                pl.BlockSpec((seq, B_DV), lambda b, h: (rb(b), v_off + h)),
                pl.BlockSpec((seq, B_DK), lambda b, h: (rb(b), h)),
                pl.BlockSpec((seq, B_DK), lambda b, h: (rb(b), B_HEADS + h)),
                pl.BlockSpec((seq, B_DV), lambda b, h: (rb(b), r_off + h)),
                pl.BlockSpec((1, B_DV), whole)]
    args = [a_f, a_b, proj, proj, proj, gates, gates, proj, bnorm_g.reshape(1, B_DV)]
    if states is not None:
        s_f, s_b, e = states
        sspec = pl.BlockSpec((1, 1, 1, B_DK, B_DV), lambda b, h: (b, e, h, 0, 0))
        in_specs += [sspec, sspec]
        args += [s_f, s_b]
    out_specs = [pl.BlockSpec((seq, B_DV), lambda b, h: (b, h))]
    out_shape = [jax.ShapeDtypeStruct((batch * seq, B_V_W), BF16)]
    if emit_state:
        nspec = pl.BlockSpec((1, 1, 1, B_DK, B_DV), lambda b, h: (b, 0, h, 0, 0))
        out_specs += [nspec, nspec]
        out_shape += [jax.ShapeDtypeStruct((batch, 1, B_HEADS, B_DK, B_DV), F32)] * 2
    return pl.pallas_call(
        functools.partial(_gla_kernel, chunk=chunk, has_state=states is not None, emit_state=emit_state),
        grid=(batch, B_HEADS),
        in_specs=in_specs, out_specs=out_specs, out_shape=out_shape,
        scratch_shapes=[pltpu.VMEM((seq, B_DV), F32)] * 2 + [pltpu.VMEM((B_DV, B_DK), F32)] * 2,
        compiler_params=_cparams("arbitrary", "arbitrary"),
        name="gla_state" if states is not None else "gla",
    )(*args)


def _sink_attn_kernel(sink_ref, q_ref, k_ref, v_ref, o_ref, nk_ref, nv_ref):
    seq = q_ref.shape[0]
    scale = C_DH ** -0.5
    for hk in range(C_KV_HEADS):
        kcols = slice(hk * C_DH, (hk + 1) * C_DH)
        k, v = k_ref[:, kcols], v_ref[:, kcols]
        q4 = jnp.concatenate([q_ref[:, (hk * C_GROUP + g) * C_DH:(hk * C_GROUP + g + 1) * C_DH]
                              for g in range(C_GROUP)], axis=0).astype(BF16)
        sink = jnp.concatenate([jnp.broadcast_to(sink_ref[hk, 0:1, g:g + 1], (seq, 1)) for g in range(C_GROUP)],
                               axis=0)
        (p,) = _softmax_parts([_dot_nt(q4, k.astype(BF16)) * scale], extra_logit=sink)
        o = jnp.dot(p.astype(BF16), v.astype(BF16), preferred_element_type=F32)
        for g in range(C_GROUP):
            h = hk * C_GROUP + g
            o_ref[:, h * C_DH:(h + 1) * C_DH] = o[g * seq:(g + 1) * seq].astype(BF16)
        nk_ref[0, 0, hk] = k
        nv_ref[0, 0, hk] = v


def sink_attention_context(proj, sink, *, batch, seq):
    sink3 = sink.reshape(C_KV_HEADS, 1, C_GROUP)
    kvspec = pl.BlockSpec((1, 1, C_KV_HEADS, seq, C_DH), lambda b: (b, 0, 0, 0, 0))
    return pl.pallas_call(
        _sink_attn_kernel,
        grid=(batch,),
        in_specs=[pl.BlockSpec(sink3.shape, lambda b: (0, 0, 0)),
                  pl.BlockSpec((seq, C_Q_W), lambda b: (b, 0)),
                  pl.BlockSpec((seq, C_KV_W), lambda b: (b, C_Q_W // C_KV_W)),
                  pl.BlockSpec((seq, C_KV_W), lambda b: (b, C_Q_W // C_KV_W + 1))],
        out_specs=[pl.BlockSpec((seq, C_Q_W), lambda b: (b, 0)), kvspec, kvspec],
        out_shape=[jax.ShapeDtypeStruct((batch * seq, C_Q_W), BF16)]
        + [jax.ShapeDtypeStruct((batch, 1, C_KV_HEADS, seq, C_DH), F32)] * 2,
        compiler_params=_cparams("arbitrary"),
        name="sink_attn",
    )(sink3, proj, proj, proj)


def _window_attn_kernel(sink_ref, q_ref, k_ref, v_ref, ck_ref, cv_ref, o_ref, *, band, hkb):
    tq = q_ref.shape[0]
    seq = k_ref.shape[0]
    scale = C_DH ** -0.5
    i = pl.program_id(2)
    start = pl.multiple_of(jnp.clip(i * tq - C_WINDOW, 0, seq - band), C_WINDOW)
    qpos = i * tq + lax.broadcasted_iota(jnp.int32, (tq, band), 0)
    kpos = start + lax.broadcasted_iota(jnp.int32, (tq, band), 1)
    valid = jnp.abs(qpos - kpos) <= C_WINDOW
    valid4 = jnp.concatenate([valid] * C_GROUP, axis=0)
    gw = C_GROUP * C_DH
    for hk in range(hkb):
        kcols = slice(hk * C_DH, (hk + 1) * C_DH)
        kb = k_ref[pl.ds(start, band), kcols].astype(BF16)
        vb = v_ref[pl.ds(start, band), kcols].astype(BF16)
        q4 = jnp.concatenate([q_ref[:, hk * gw + g * C_DH:hk * gw + (g + 1) * C_DH] for g in range(C_GROUP)],
                             axis=0).astype(BF16)
        sink = jnp.concatenate([jnp.broadcast_to(sink_ref[hk, 0:1, g:g + 1], (tq, 1)) for g in range(C_GROUP)],
                               axis=0)
        s_band = jnp.where(valid4, _dot_nt(q4, kb) * scale, -jnp.inf)
        s_ctx = _dot_nt(q4, ck_ref[0, 0, hk].astype(BF16)) * scale
        p_ctx, p_band = _softmax_parts([s_ctx, s_band], extra_logit=sink)
        o = (jnp.dot(p_ctx.astype(BF16), cv_ref[0, 0, hk].astype(BF16), preferred_element_type=F32)
             + jnp.dot(p_band.astype(BF16), vb, preferred_element_type=F32))
        for g in range(C_GROUP):
            o_ref[:, hk * gw + g * C_DH:hk * gw + (g + 1) * C_DH] = o[g * tq:(g + 1) * tq].astype(BF16)


def window_attention_latent(proj, sink, ctx_k, ctx_v, e, *, row_off, batch, seq, tq=256, hkb=2):
    tq = min(tq, seq)
    band = min(tq + 2 * C_WINDOW, seq)
    assert row_off % seq == 0 and seq % tq == 0 and tq % C_WINDOW == 0 and C_KV_HEADS % hkb == 0
    past = ctx_k.shape[3]
    qw, kw = hkb * C_GROUP * C_DH, hkb * C_DH
    rb = lambda b: (row_off + b * seq) // seq
    cspec = pl.BlockSpec((1, 1, hkb, past, C_DH), lambda b, hg, i: (b, e, hg, 0, 0))
    sink3 = sink.reshape(C_KV_HEADS, 1, C_GROUP)
    return pl.pallas_call(
        functools.partial(_window_attn_kernel, band=band, hkb=hkb),
        grid=(batch, C_KV_HEADS // hkb, seq // tq),
        in_specs=[pl.BlockSpec((hkb, 1, C_GROUP), lambda b, hg, i: (hg, 0, 0)),
                  pl.BlockSpec((tq, qw), lambda b, hg, i: ((row_off + b * seq) // tq + i, hg)),
                  pl.BlockSpec((seq, kw), lambda b, hg, i: (rb(b), C_Q_W // kw + hg)),
                  pl.BlockSpec((seq, kw), lambda b, hg, i: (rb(b), (C_Q_W + C_KV_W) // kw + hg)),
                  cspec, cspec],
        out_specs=pl.BlockSpec((tq, qw), lambda b, hg, i: (b * (seq // tq) + i, hg)),
        out_shape=jax.ShapeDtypeStruct((batch * seq, C_Q_W), BF16),
        compiler_params=_cparams("arbitrary", "arbitrary", "arbitrary"),
        name="window_attn",
    )(sink3, proj, proj, proj, ctx_k, ctx_v)


def kernel(x_prompt, x_sample, cache_a_k, cache_a_v, state_b_fwd, state_b_bwd, cache_c_k, cache_c_v, c, c_ctx, ada_w, ada_b, norm_g, ffn_w_in, ffn_w_out, ab_w_in, ab_w_out, a_lambda, a_subln_g, b_alpha_w, b_alpha_b, b_norm_g, c_w_in, c_w_out, c_sink, final_g):
    bp, sp, d = x_prompt.shape
    bs, ss, _ = x_sample.shape
    depth = ada_w.shape[0]
    mp, ms = bp * sp, bs * ss
    rows_info = (mp, ss)

    cond = jnp.concatenate([c_ctx[None, :], c, jnp.zeros((COND_ROWS - 1 - bs, d), F32)], axis=0)
    mods = adaln(cond, ada_w, ada_b).reshape(depth, COND_ROWS, N_MOD, d)
    rope_a = _rope_tables(ss, A_DH, 2) + (2 * A_QK_W,)
    rope_c = _rope_tables(ss, C_DH, 1) + (C_Q_W + C_KV_W,)
    ffn_w_in, ffn_w_out, ab_w_in, ab_w_out, c_w_in, c_w_out = (
        w.astype(BF16) for w in (ffn_w_in, ffn_w_out, ab_w_in, ab_w_out, c_w_in, c_w_out))

    a_k, a_v, b_f, b_b, c_k, c_v = [], [], [], [], [], []
    xs = [x_prompt.reshape(mp, d), x_sample.reshape(ms, d)]
    for l in range(depth):
        mod = mods[l]
        x = ffn_half_step(xs, mod[:, 0:3], norm_g[l, 0], ffn_w_in, ffn_w_out, (l, 0), rows_info)
        if l % 2 == 0:
            e = l // 2
            lam_init = 0.8 - 0.6 * math.exp(-0.3 * l)
            zeros = jnp.zeros((B_RANK, B_QK_W), F32)
            w_alpha = jnp.concatenate([jnp.concatenate([b_alpha_w[e, 0], zeros], axis=1),
                                       jnp.concatenate([zeros, b_alpha_w[e, 1]], axis=1)], axis=0)
            gate_w = (ab_w_in[e][:, AB_MAIN:], w_alpha, b_alpha_b[e].reshape(-1))
            proj, gates = mixer_in_proj(x, mod[:, 3:5], norm_g[l, 1], ab_w_in, e, AB_MAIN, rows_info, rope_a,
                                        gates=gate_w)
            attn_p, ak, av = diff_attention(proj, a_lambda[e], a_subln_g[e], lam_init, row_off=0, batch=bp,
                                            seq=sp, tq=sp, hb=A_HEADS, emit_kv=True)
            (attn_s,) = diff_attention(proj, a_lambda[e], a_subln_g[e], lam_init, row_off=mp, batch=bs,
                                       seq=ss, tq=min(256, ss), hb=2, ctx=(cache_a_k, cache_a_v, e))
            gla_p, sf, sb = gla_bidirectional(proj, gates, b_norm_g[e], row_off=0, batch=bp, seq=sp,
                                              emit_state=True)
            (gla_s,) = gla_bidirectional(proj, gates, b_norm_g[e], row_off=mp, batch=bs, seq=ss,
                                         states=(state_b_fwd, state_b_bwd, e))
            a_k.append(ak), a_v.append(av), b_f.append(sf), b_b.append(sb)
            x = mixer_out_proj([attn_p, gla_p], [attn_s, gla_s], ab_w_out, e, x, mod[:, 5:6], rows_info)
        else:
            o = l // 2
            proj = mixer_in_proj(x, mod[:, 3:5], norm_g[l, 1], c_w_in, o, c_w_in.shape[2], rows_info, rope_c)
            mix_p, ck, cv = sink_attention_context(proj, c_sink[o], batch=bp, seq=sp)
            mix_s = window_attention_latent(proj, c_sink[o], cache_c_k, cache_c_v, o, row_off=mp, batch=bs, seq=ss)
            c_k.append(ck), c_v.append(cv)
            x = mixer_out_proj([mix_p], [mix_s], c_w_out, o, x, mod[:, 5:6], rows_info)
        x = ffn_half_step([x], mod[:, 6:9], norm_g[l, 2], ffn_w_in, ffn_w_out, (l, 1), rows_info,
                          final_g=final_g if l == depth - 1 else None)
        xs = [x]

    y_prompt, y_sample = x[:mp].reshape(bp, sp, d), x[mp:].reshape(bs, ss, d)
    cat = lambda parts: parts[0] if len(parts) == 1 else jnp.concatenate(parts, axis=1)
    return (y_prompt, y_sample, cat(a_k), cat(a_v), cat(b_f), cat(b_b), cat(c_k), cat(c_v))
```

```python
import functools
import math

import numpy as np
import jax
import jax.numpy as jnp
from jax import lax
from jax.experimental import pallas as pl
from jax.experimental.pallas import tpu as pltpu

F32 = jnp.float32
BF16 = jnp.bfloat16

EPS = 1e-6
GRID_W = 64
ROPE_BASE = 10000.0
N_MOD = 9
A_HEADS, A_DH, A_DV = 8, 64, 128
B_HEADS, B_DK, B_DV, B_RANK, B_TAU = 4, 128, 256, 16, 16.0
C_HEADS, C_KV_HEADS, C_DH, C_WINDOW = 16, 4, 128, 128
A_QK_W = A_HEADS * 2 * A_DH
A_V_W = A_HEADS * A_DV
B_QK_W = B_HEADS * B_DK
B_V_W = B_HEADS * B_DV
AB_MAIN = 2 * A_QK_W + A_V_W + 2 * B_QK_W + 2 * B_V_W
C_GROUP = C_HEADS // C_KV_HEADS
C_Q_W = C_HEADS * C_DH
C_KV_W = C_KV_HEADS * C_DH

V7X_VMEM_BYTES = 64 * 1024 * 1024
VMEM_LIMIT_BYTES = V7X_VMEM_BYTES - 8 * 1024 * 1024
COND_ROWS = 8
ROW_CHUNK = 128


def _tile(n, pref, align=128):
    if n <= pref:
        return n
    t = (pref // align) * align
    while n % t:
        t -= align
    assert t > 0, (n, pref)
    return t


def _cparams(*sem):
    return pltpu.CompilerParams(dimension_semantics=sem, vmem_limit_bytes=VMEM_LIMIT_BYTES)


def _bdot(a, b):
    return jnp.dot(a.astype(BF16), b.astype(BF16), preferred_element_type=F32)


def _dot_nt(a, b):
    return lax.dot_general(a, b, (((1,), (1,)), ((), ())), preferred_element_type=F32)


def _dot_tn(a, b):
    return lax.dot_general(a, b, (((0,), (0,)), ((), ())), preferred_element_type=F32)


def _rms(x):
    return x * lax.rsqrt(jnp.mean(x * x, axis=-1, keepdims=True) + EPS)


def _adaln_kernel(cond_ref, w_ref, b_ref, o_ref):
    c = cond_ref[...]
    o_ref[0] = _bdot(c * jax.nn.sigmoid(c), w_ref[0]) + b_ref[0]


def adaln(cond, ada_w, ada_b):
    depth, d, n = ada_w.shape
    tn = _tile(n, 1024)
    return pl.pallas_call(
        _adaln_kernel,
        grid=(depth, n // tn),
        in_specs=[pl.BlockSpec((COND_ROWS, d), lambda l, j: (0, 0)),
                  pl.BlockSpec((1, d, tn), lambda l, j: (l, 0, j)),
                  pl.BlockSpec((1, 1, tn), lambda l, j: (l, 0, j))],
        out_specs=pl.BlockSpec((1, COND_ROWS, tn), lambda l, j: (l, 0, j)),
        out_shape=jax.ShapeDtypeStruct((depth, COND_ROWS, n), F32),
        compiler_params=_cparams("arbitrary", "arbitrary"),
        name="adaln",
    )(cond, ada_w, ada_b.reshape(depth, 1, n))


def _norm_modulate(x_ref, g_ref, mod_ref, h_ref):
    g = g_ref[...]
    shift = mod_ref[0, 0:1, :]
    scale1 = 1.0 + mod_ref[0, 1:2, :]

    def body(r, carry):
        rows = pl.ds(pl.multiple_of(r * ROW_CHUNK, ROW_CHUNK), ROW_CHUNK)
        h_ref[rows, :] = ((_rms(x_ref[rows, :]) * g) * scale1 + shift).astype(BF16)
        return carry

    lax.fori_loop(0, x_ref.shape[0] // ROW_CHUNK, body, 0)


def _row_tile(rows_info):
    n_prompt_rows, lat_rows = rows_info
    return _tile(math.gcd(n_prompt_rows, lat_rows), 1024, align=ROW_CHUNK)


def _cond_row(i, tm, n_prompt_rows, lat_rows):
    return jnp.maximum((i * tm - n_prompt_rows) // lat_rows + 1, 0)


def _ffn_kernel(*refs, n_x, n_prompt_tiles, row_split, n_chunk, final):
    x_refs, refs = refs[:n_x], refs[n_x:]
    mod_ref, g_ref, wg_ref, wu_ref, wo_ref = refs[:5]
    refs = refs[5:]
    if final:
        fg_ref, refs = refs[0], refs[1:]
    o_ref, h_ref = refs
    i, j = pl.program_id(0), pl.program_id(1)

    def with_x(fn):
        if n_x == 1:
            fn(x_refs[0])
        else:
            pl.when(i < n_prompt_tiles)(lambda: fn(x_refs[0]))
            pl.when(i >= n_prompt_tiles)(lambda: fn(x_refs[1]))

    @pl.when(j == 0)
    def _():
        with_x(lambda x_ref: _norm_modulate(x_ref, g_ref, mod_ref, h_ref))
        o_ref[...] = jnp.zeros(o_ref.shape, F32)

    tm, d = h_ref.shape
    rt = tm // row_split
    wg, wu, wo = (w_ref[...].astype(BF16) for w_ref in (wg_ref, wu_ref, wo_ref))
    for r in range(row_split):
        rows = slice(r * rt, (r + 1) * rt)
        h = h_ref[rows, :]
        gate = jnp.dot(h, wg, preferred_element_type=F32)
        up = jnp.dot(h, wu, preferred_element_type=F32)
        act = (gate * jax.nn.sigmoid(gate) * up).astype(BF16)
        for c in range(d // n_chunk):
            cols = slice(c * n_chunk, (c + 1) * n_chunk)
            o_ref[rows, cols] += jnp.dot(act, wo[:, cols], preferred_element_type=F32)

    @pl.when(j == pl.num_programs(1) - 1)
    def _():
        half_gate = 0.5 * mod_ref[0, 2:3, :]

        def finish(x_ref):
            def body(r, carry):
                rows = pl.ds(pl.multiple_of(r * ROW_CHUNK, ROW_CHUNK), ROW_CHUNK)
                xn = x_ref[rows, :] + half_gate * o_ref[rows, :]
                if final:
                    xn = _rms(xn) * fg_ref[...]
                o_ref[rows, :] = xn
                return carry

            lax.fori_loop(0, tm // ROW_CHUNK, body, 0)

        with_x(finish)


def ffn_half_step(xs, mod3, g, w_in, w_out, widx, rows_info, final_g=None):
    d = xs[0].shape[1]
    f = w_out.shape[-2]
    n_prompt_rows, lat_rows = rows_info
    m = sum(x.shape[0] for x in xs)
    tm, tf, n_chunk = _row_tile(rows_info), _tile(f, 256), _tile(d, 512)
    nf = f // tf
    npt = n_prompt_rows // tm
    final = final_g is not None
    l, k = widx
    if len(xs) == 1:
        x_specs = [pl.BlockSpec((tm, d), lambda i, j: (i, 0))]
    else:
        x_specs = [pl.BlockSpec((tm, d), lambda i, j: (jnp.minimum(i, npt - 1), 0), pipeline_mode=pl.Buffered(1)),
                   pl.BlockSpec((tm, d), lambda i, j: (jnp.maximum(i - npt, 0), 0), pipeline_mode=pl.Buffered(1))]
    in_specs = x_specs + [
        pl.BlockSpec((1, 3, d), lambda i, j: (_cond_row(i, tm, n_prompt_rows, lat_rows), 0, 0)),
        pl.BlockSpec((1, d), lambda i, j: (0, 0)),
        pl.BlockSpec((None, None, d, tf), lambda i, j: (l, k, 0, j)),
        pl.BlockSpec((None, None, d, tf), lambda i, j: (l, k, 0, nf + j)),
        pl.BlockSpec((None, None, tf, d), lambda i, j: (l, k, j, 0))]
    args = list(xs) + [mod3, g.reshape(1, d), w_in, w_in, w_out]
    if final:
        in_specs.append(pl.BlockSpec((1, d), lambda i, j: (0, 0)))
        args.append(final_g.reshape(1, d))
    return pl.pallas_call(
        functools.partial(_ffn_kernel, n_x=len(xs), n_prompt_tiles=npt, row_split=max(1, tm // 512),
                          n_chunk=n_chunk, final=final),
        grid=(m // tm, nf),
        in_specs=in_specs,
        out_specs=pl.BlockSpec((tm, d), lambda i, j: (i, 0)),
        out_shape=jax.ShapeDtypeStruct((m, d), F32),
        scratch_shapes=[pltpu.VMEM((tm, d), BF16)],
        compiler_params=_cparams("arbitrary", "arbitrary"),
        name="ffn_final" if final else "ffn",
    )(*args)


def _log_sigmoid(z):
    return jnp.minimum(z, 0.0) - jnp.log(1.0 + jnp.exp(-jnp.abs(z)))


def _proj_kernel(*refs, n_prompt_tiles, n_rope_cols, rope_shift, tn, with_gates):
    x_ref, mod_ref, g_ref, w_ref, cos_ref, sa_ref, sb_ref = refs[:7]
    if with_gates:
        wl_ref, wa_ref, ba_ref, o_ref, gates_ref = refs[7:]
    else:
        (o_ref,) = refs[7:]
    x = x_ref[...]
    h = (((_rms(x) * g_ref[...]) * (1.0 + mod_ref[0, 1:2, :])) + mod_ref[0, 0:1, :]).astype(BF16)
    if with_gates:
        low = jnp.dot(h, wl_ref[...], preferred_element_type=F32)
        z = jnp.dot(low.astype(BF16), wa_ref[...].astype(BF16), preferred_element_type=F32) + ba_ref[...]
        gates_ref[...] = _log_sigmoid(z) * (1.0 / B_TAU)

    def project(rotary):
        for c in range(o_ref.shape[1] // tn):
            acc = jnp.dot(h, w_ref[:, c * tn:(c + 1) * tn], preferred_element_type=F32)
            if rotary and c * tn < n_rope_cols:
                cos, sa, sb = cos_ref[...], sa_ref[...], sb_ref[...]
                for s in range(tn // 128):
                    a = acc[:, s * 128:(s + 1) * 128]
                    o_ref[:, c * tn + s * 128:c * tn + (s + 1) * 128] = (
                        a * cos + pltpu.roll(a, 128 - rope_shift, 1) * sa + pltpu.roll(a, rope_shift, 1) * sb)
            else:
                o_ref[:, c * tn:(c + 1) * tn] = acc

    i = pl.program_id(0)
    pl.when(i < n_prompt_tiles)(lambda: project(False))
    pl.when(i >= n_prompt_tiles)(lambda: project(True))


def _rope_tables(n, head_dim, n_sub):
    assert head_dim * n_sub == 128
    rows = n // GRID_W
    row = jnp.repeat(jnp.arange(rows, dtype=F32), GRID_W)
    col = jnp.tile(jnp.arange(GRID_W, dtype=F32), rows)
    d_axis = head_dim // 2
    shift = d_axis // 2
    inv = ROPE_BASE ** (-jnp.arange(0, d_axis, 2, dtype=F32) / d_axis)
    lane = jnp.arange(128)
    sub = lane % head_dim
    is_col = (sub // d_axis) == 1
    within = sub % d_axis
    freq = within % shift
    second = (within // shift) == 1
    pos = jnp.where(is_col[None, :], col[:, None], row[:, None])
    ang = pos * inv[freq][None, :]
    cos, sin = jnp.cos(ang), jnp.sin(ang)
    sin_a = jnp.where(second[None, :], 0.0, -sin)
    sin_b = jnp.where(second[None, :], sin, 0.0)
    return cos, sin_a, sin_b, shift


def mixer_in_proj(x, mod2, g, w, e, n_cols, rows_info, rope, gates=None):
    m, d = x.shape
    n_prompt_rows, lat_rows = rows_info
    cos, sin_a, sin_b, rope_shift, n_rope_cols = rope
    tm = _tile(math.gcd(n_prompt_rows, lat_rows), 256, align=ROW_CHUNK)
    tn = _tile(math.gcd(n_cols, n_rope_cols), 512)
    npt, lat_tiles = n_prompt_rows // tm, lat_rows // tm
    tab = lambda i: (jnp.maximum(i - npt, 0) % lat_tiles, 0)
    once = dict(pipeline_mode=pl.Buffered(1))
    in_specs = [pl.BlockSpec((tm, d), lambda i: (i, 0)),
                pl.BlockSpec((1, 2, d), lambda i: (_cond_row(i, tm, n_prompt_rows, lat_rows), 0, 0)),
                pl.BlockSpec((1, d), lambda i: (0, 0)),
                pl.BlockSpec((None, d, n_cols), lambda i: (e, 0, 0), **once),
                pl.BlockSpec((tm, 128), tab), pl.BlockSpec((tm, 128), tab), pl.BlockSpec((tm, 128), tab)]
    args = [x, mod2, g.reshape(1, d), w, cos, sin_a, sin_b]
    out_specs = pl.BlockSpec((tm, n_cols), lambda i: (i, 0))
    out_shape = jax.ShapeDtypeStruct((m, n_cols), F32)
    if gates is not None:
        w_low, w_alpha, b_alpha = gates
        n_gate = w_alpha.shape[1]
        in_specs += [pl.BlockSpec(w_low.shape, lambda i: (0, 0)),
                     pl.BlockSpec(w_alpha.shape, lambda i: (0, 0)),
                     pl.BlockSpec((1, n_gate), lambda i: (0, 0))]
        args += [w_low, w_alpha, b_alpha.reshape(1, n_gate)]
        out_specs = [out_specs, pl.BlockSpec((tm, n_gate), lambda i: (i, 0))]
        out_shape = [out_shape, jax.ShapeDtypeStruct((m, n_gate), F32)]
    return pl.pallas_call(
        functools.partial(_proj_kernel, n_prompt_tiles=npt, n_rope_cols=n_rope_cols, rope_shift=rope_shift,
                          tn=tn, with_gates=gates is not None),
        grid=(m // tm,),
        in_specs=in_specs, out_specs=out_specs, out_shape=out_shape,
        compiler_params=_cparams("arbitrary"),
        name="mixer_in_proj_gated" if gates is not None else "mixer_in_proj",
    )(*args)


def _out_proj_kernel(*refs, n_parts, n_prompt_tiles, tn):
    parts_p, parts_s = refs[:n_parts], refs[n_parts:2 * n_parts]
    w_refs = refs[2 * n_parts:3 * n_parts]
    x_ref, gate_ref, o_ref = refs[3 * n_parts:]
    i = pl.program_id(0)

    def run(parts):
        for c in range(o_ref.shape[1] // tn):
            cols = slice(c * tn, (c + 1) * tn)
            acc = None
            for p_ref, w_ref in zip(parts, w_refs):
                t = jnp.dot(p_ref[...], w_ref[:, cols], preferred_element_type=F32)
                acc = t if acc is None else acc + t
            o_ref[:, cols] = x_ref[:, cols] + gate_ref[0, :, cols] * acc

    pl.when(i < n_prompt_tiles)(lambda: run(parts_p))
    pl.when(i >= n_prompt_tiles)(lambda: run(parts_s))


def mixer_out_proj(parts_p, parts_s, w, e, x, gate, rows_info):
    m, d = x.shape
    n_prompt_rows, lat_rows = rows_info
    tm = _tile(math.gcd(n_prompt_rows, lat_rows), 512, align=ROW_CHUNK)
    npt = n_prompt_rows // tm
    n_parts = len(parts_p)
    in_specs, w_specs, off = [], [], 0
    for p in parts_p:
        in_specs.append(pl.BlockSpec((tm, p.shape[1]), lambda i: (jnp.minimum(i, npt - 1), 0)))
    for p in parts_s:
        kp = p.shape[1]
        in_specs.append(pl.BlockSpec((tm, kp), lambda i: (jnp.maximum(i - npt, 0), 0)))
        assert off % kp == 0
        w_specs.append(pl.BlockSpec((None, kp, d), functools.partial(lambda i, rb: (e, rb, 0), rb=off // kp),
                                    pipeline_mode=pl.Buffered(1)))
        off += kp
    in_specs += w_specs
    in_specs += [pl.BlockSpec((tm, d), lambda i: (i, 0)),
                 pl.BlockSpec((1, 1, d), lambda i: (_cond_row(i, tm, n_prompt_rows, lat_rows), 0, 0))]
    return pl.pallas_call(
        functools.partial(_out_proj_kernel, n_parts=n_parts, n_prompt_tiles=npt, tn=_tile(d, 512)),
        grid=(m // tm,),
        in_specs=in_specs,
        out_specs=pl.BlockSpec((tm, d), lambda i: (i, 0)),
        out_shape=jax.ShapeDtypeStruct((m, d), F32),
        compiler_params=_cparams("arbitrary"),
        name="mixer_out_proj",
    )(*parts_p, *parts_s, *([w] * n_parts), x, gate)


def _softmax_parts(parts, extra_logit=None, weight=None):
    m = functools.reduce(jnp.maximum, [jnp.max(p, axis=-1, keepdims=True) for p in parts])
    if extra_logit is not None:
        m = jnp.maximum(m, extra_logit)
    es = [jnp.exp(p - m) for p in parts]
    den = functools.reduce(jnp.add, [jnp.sum(e, axis=-1, keepdims=True) for e in es])
    if extra_logit is not None:
        den = den + jnp.exp(extra_logit - m)
    inv = 1.0 / den if weight is None else weight / den
    return [e * inv for e in es]


def _diff_attn_kernel(*refs, hb, lam_init, has_ctx, emit_kv):
    it = iter(refs)
    lam_ref, g_ref, q_ref, k_ref, v_ref = (next(it) for _ in range(5))
    ck_ref, cv_ref = (next(it), next(it)) if has_ctx else (None, None)
    o_ref = next(it)
    nk_ref, nv_ref = (next(it), next(it)) if emit_kv else (None, None)

    lp = lam_ref[...]
    lam = (jnp.exp(jnp.sum(lp[0:1] * lp[1:2], axis=-1, keepdims=True))
           - jnp.exp(jnp.sum(lp[2:3] * lp[3:4], axis=-1, keepdims=True)) + lam_init)
    scale = A_DH ** -0.5
    assert math.frexp(scale)[0] == 0.5
    first = lax.broadcasted_iota(jnp.int32, (1, 2 * A_DH), 1) < A_DH
    for hh in range(hb):
        cols = slice(hh * 128, (hh + 1) * 128)
        q, k, v = q_ref[:, cols] * scale, k_ref[:, cols], v_ref[:, cols]
        q1 = jnp.where(first, q, 0.0).astype(BF16)
        q2 = jnp.where(first, 0.0, q).astype(BF16)
        keys, vals = [k.astype(BF16)], [v.astype(BF16)]
        if has_ctx:
            keys.insert(0, ck_ref[0, 0, hh].astype(BF16))
            vals.insert(0, cv_ref[0, 0, hh].astype(BF16))
        p1 = _softmax_parts([_dot_nt(q1, kk) for kk in keys])
        p2 = _softmax_parts([_dot_nt(q2, kk) for kk in keys], weight=lam)
        o = None
        for a, b, vv in zip(p1, p2, vals):
            t = jnp.dot((a - b).astype(BF16), vv, preferred_element_type=F32)
            o = t if o is None else o + t
        o_ref[:, cols] = ((_rms(o) * g_ref[...]) * (1.0 - lam_init)).astype(BF16)
        if emit_kv:
            nk_ref[0, 0, hh] = k
            nv_ref[0, 0, hh] = v


def diff_attention(proj, lam_p, subln_g, lam_init, *, row_off, batch, seq, tq, hb, ctx=None, emit_kv=False):
    width = hb * 128
    n_hg = A_HEADS // hb
    assert row_off % seq == 0 and seq % tq == 0
    qrow = lambda b, hg, i: ((row_off + b * seq) // tq + i, hg)
    krow = lambda b, hg, i: ((row_off + b * seq) // seq, A_QK_W // width + hg)
    vrow = lambda b, hg, i: ((row_off + b * seq) // seq, 2 * A_QK_W // width + hg)
    in_specs = [pl.BlockSpec(lam_p.shape, lambda b, hg, i: (0, 0)),
                pl.BlockSpec((1, A_DV), lambda b, hg, i: (0, 0)),
                pl.BlockSpec((tq, width), qrow),
                pl.BlockSpec((seq, width), krow),
                pl.BlockSpec((seq, width), vrow)]
    args = [lam_p, subln_g.reshape(1, A_DV), proj, proj, proj]
    if ctx is not None:
        ctx_k, ctx_v, e = ctx
        past = ctx_k.shape[3]
        cspec = lambda b, hg, i: (b, e, hg, 0, 0)
        in_specs += [pl.BlockSpec((1, 1, hb, past, 2 * A_DH), cspec), pl.BlockSpec((1, 1, hb, past, A_DV), cspec)]
        args += [ctx_k, ctx_v]
    out_specs = [pl.BlockSpec((tq, width), lambda b, hg, i: (b * (seq // tq) + i, hg))]
    out_shape = [jax.ShapeDtypeStruct((batch * seq, A_V_W), BF16)]
    if emit_kv:
        assert tq == seq
        kvspec = pl.BlockSpec((1, 1, hb, seq, 128), lambda b, hg, i: (b, 0, hg, 0, 0))
        out_specs += [kvspec, kvspec]
        out_shape += [jax.ShapeDtypeStruct((batch, 1, A_HEADS, seq, 128), F32)] * 2
    return pl.pallas_call(
        functools.partial(_diff_attn_kernel, hb=hb, lam_init=lam_init, has_ctx=ctx is not None, emit_kv=emit_kv),
        grid=(batch, n_hg, seq // tq),
        in_specs=in_specs, out_specs=out_specs, out_shape=out_shape,
        compiler_params=_cparams("arbitrary", "arbitrary", "arbitrary"),
        name="diff_attn_ctx" if ctx is not None else "diff_attn",
    )(*args)


def _gla_level_matrices(chunk, fwd):
    t = np.arange(chunk)[:, None]
    i = np.arange(chunk)[None, :]
    mats = [i <= t, i > t] if fwd else [i >= t, i < t]
    h = chunk // 2
    while h >= 1:
        p = t % (2 * h)
        base = t - p
        if fwd:
            m = base + h - 1
            a = np.where(p >= h, (i > m) & (i <= t), (i > t) & (i <= m))
        else:
            m = base + h
            a = np.where(p < h, (i >= t) & (i < m), (i >= m) & (i < t))
        mats.append(a)
        h //= 2
    return jnp.asarray(np.concatenate(mats, axis=0).astype(np.float32), dtype=BF16)


def _gla_kernel(*refs, chunk, has_state, emit_state):
    it = iter(refs)
    af_ref, ab_ref, q_ref, k_ref, v_ref, laf_ref, lab_ref, r_ref, g_ref = (next(it) for _ in range(9))
    sf_ref, sb_ref = (next(it), next(it)) if has_state else (None, None)
    o_ref = next(it)
    nsf_ref, nsb_ref = (next(it), next(it)) if emit_state else (None, None)
    accf_ref, accb_ref, stf_ref, stb_ref = (next(it) for _ in range(4))

    seq = q_ref.shape[0]
    n_chunks = seq // chunk
    n_lev = chunk.bit_length() - 1
    ti = lax.broadcasted_iota(jnp.int32, (chunk, chunk), 0)
    si = lax.broadcasted_iota(jnp.int32, (chunk, chunk), 1)
    split = ti ^ si
    scale = B_DK ** -0.5

    def chunk_step(c, fwd, a_ref, la_ref, acc_ref, st_ref):
        rows = c * chunk if isinstance(c, int) else pl.multiple_of(c * chunk, chunk)
        rows = pl.ds(rows, chunk)
        order = (ti > si) if fwd else (ti < si)
        q = q_ref[rows, :] * scale
        k = k_ref[rows, :]
        vb = v_ref[rows, :].astype(BF16)
        la = la_ref[rows, :]
        la_hi = la.astype(BF16)
        la_lo = (la - la_hi.astype(F32)).astype(BF16)
        r2 = jnp.dot(a_ref[...], jnp.concatenate([la_hi, la_lo], axis=1), preferred_element_type=F32)
        e = jnp.exp(r2[:, :B_DK] + r2[:, B_DK:])
        st = st_ref[...]
        inter = _dot_nt((q * e[0:chunk]).astype(BF16), st.astype(BF16))
        ku = (k * e[chunk:2 * chunk]).astype(BF16)
        scores = jnp.where(ti == si, _dot_nt(q.astype(BF16), k.astype(BF16)), 0.0)
        for lev in range(n_lev):
            half = chunk >> (lev + 1)
            f = e[(2 + lev) * chunk:(3 + lev) * chunk]
            sc = _dot_nt((q * f).astype(BF16), (k * f).astype(BF16))
            scores = jnp.where(order & (split >= half) & (split < 2 * half), sc, scores)
        acc_ref[rows, :] = inter + jnp.dot(scores.astype(BF16), vb, preferred_element_type=F32)
        total = e[chunk - 1:chunk] if fwd else e[0:1]
        st_ref[...] = st * total + _dot_tn(vb, ku)

    for s0_ref, st_ref in ((sf_ref, stf_ref), (sb_ref, stb_ref)):
        st_ref[...] = s0_ref[0, 0, 0].T if has_state else jnp.zeros(st_ref.shape, F32)

    def body(cc, carry):
        chunk_step(cc, True, af_ref, laf_ref, accf_ref, stf_ref)
        chunk_step(n_chunks - 1 - cc, False, ab_ref, lab_ref, accb_ref, stb_ref)
        return carry

    if n_chunks == 1:
        body(0, 0)
    else:
        lax.fori_loop(0, n_chunks, body, 0)
    if emit_state:
        nsf_ref[0, 0, 0] = stf_ref[...].T
        nsb_ref[0, 0, 0] = stb_ref[...].T

    def epilogue(r, carry):
        rows = pl.ds(pl.multiple_of(r * ROW_CHUNK, ROW_CHUNK), ROW_CHUNK)
        gate = r_ref[rows, :]
        o = accf_ref[rows, :] + accb_ref[rows, :]
        o_ref[rows, :] = ((_rms(o) * g_ref[...]) * (gate * jax.nn.sigmoid(gate))).astype(BF16)
        return carry

    lax.fori_loop(0, seq // ROW_CHUNK, epilogue, 0)


def gla_bidirectional(proj, gates, bnorm_g, *, row_off, batch, seq, states=None, emit_state=False):
    chunk = min(256, seq)
    assert row_off % seq == 0 and seq % chunk == 0 and chunk & (chunk - 1) == 0
    rb = lambda b: (row_off + b * seq) // seq
    q_off = (2 * A_QK_W + A_V_W) // B_DK
    k_off = q_off + B_HEADS
    v_off = (2 * A_QK_W + A_V_W + 2 * B_QK_W) // B_DV
    r_off = v_off + B_HEADS
    a_f, a_b = _gla_level_matrices(chunk, True), _gla_level_matrices(chunk, False)
    whole = lambda b, h: (0, 0)
    in_specs = [pl.BlockSpec(a_f.shape, whole), pl.BlockSpec(a_b.shape, whole),
                pl.BlockSpec((seq, B_DK), lambda b, h: (rb(b), q_off + h)),
                pl.BlockSpec((seq, B_DK), lambda b, h: (rb(b), k_off + h)),
                pl.BlockSpec((seq, B_DV), lambda b, h: (rb(b), v_off + h)),
                pl.BlockSpec((seq, B_DK), lambda b, h: (rb(b), h)),
                pl.BlockSpec((seq, B_DK), lambda b, h: (rb(b), B_HEADS + h)),
                pl.BlockSpec((seq, B_DV), lambda b, h: (rb(b), r_off + h)),
                pl.BlockSpec((1, B_DV), whole)]
    args = [a_f, a_b, proj, proj, proj, gates, gates, proj, bnorm_g.reshape(1, B_DV)]
    if states is not None:
        s_f, s_b, e = states
        sspec = pl.BlockSpec((1, 1, 1, B_DK, B_DV), lambda b, h: (b, e, h, 0, 0))
        in_specs += [sspec, sspec]
        args += [s_f, s_b]
    out_specs = [pl.BlockSpec((seq, B_DV), lambda b, h: (b, h))]
    out_shape = [jax.ShapeDtypeStruct((batch * seq, B_V_W), BF16)]
    if emit_state:
        nspec = pl.BlockSpec((1, 1, 1, B_DK, B_DV), lambda b, h: (b, 0, h, 0, 0))
        out_specs += [nspec, nspec]
        out_shape += [jax.ShapeDtypeStruct((batch, 1, B_HEADS, B_DK, B_DV), F32)] * 2
    return pl.pallas_call(
        functools.partial(_gla_kernel, chunk=chunk, has_state=states is not None, emit_state=emit_state),
        grid=(batch, B_HEADS),
        in_specs=in_specs, out_specs=out_specs, out_shape=out_shape,
        scratch_shapes=[pltpu.VMEM((seq, B_DV), F32)] * 2 + [pltpu.VMEM((B_DV, B_DK), F32)] * 2,
        compiler_params=_cparams("arbitrary", "arbitrary"),
        name="gla_state" if states is not None else "gla",
    )(*args)


def _sink_attn_kernel(sink_ref, q_ref, k_ref, v_ref, o_ref, nk_ref, nv_ref):
    seq = q_ref.shape[0]
    scale = C_DH ** -0.5
    for hk in range(C_KV_HEADS):
        kcols = slice(hk * C_DH, (hk + 1) * C_DH)
        k, v = k_ref[:, kcols], v_ref[:, kcols]
        q4 = jnp.concatenate([q_ref[:, (hk * C_GROUP + g) * C_DH:(hk * C_GROUP + g + 1) * C_DH]
                              for g in range(C_GROUP)], axis=0).astype(BF16)
        sink = jnp.concatenate([jnp.broadcast_to(sink_ref[hk, 0:1, g:g + 1], (seq, 1)) for g in range(C_GROUP)],
                               axis=0)
        (p,) = _softmax_parts([_dot_nt(q4, k.astype(BF16)) * scale], extra_logit=sink)
        o = jnp.dot(p.astype(BF16), v.astype(BF16), preferred_element_type=F32)
        for g in range(C_GROUP):
            h = hk * C_GROUP + g
            o_ref[:, h * C_DH:(h + 1) * C_DH] = o[g * seq:(g + 1) * seq].astype(BF16)
        nk_ref[0, 0, hk] = k
        nv_ref[0, 0, hk] = v


def sink_attention_context(proj, sink, *, batch, seq):
    sink3 = sink.reshape(C_KV_HEADS, 1, C_GROUP)
    kvspec = pl.BlockSpec((1, 1, C_KV_HEADS, seq, C_DH), lambda b: (b, 0, 0, 0, 0))
    return pl.pallas_call(
        _sink_attn_kernel,
        grid=(batch,),
        in_specs=[pl.BlockSpec(sink3.shape, lambda b: (0, 0, 0)),
                  pl.BlockSpec((seq, C_Q_W), lambda b: (b, 0)),
                  pl.BlockSpec((seq, C_KV_W), lambda b: (b, C_Q_W // C_KV_W)),
                  pl.BlockSpec((seq, C_KV_W), lambda b: (b, C_Q_W // C_KV_W + 1))],
        out_specs=[pl.BlockSpec((seq, C_Q_W), lambda b: (b, 0)), kvspec, kvspec],
        out_shape=[jax.ShapeDtypeStruct((batch * seq, C_Q_W), BF16)]
        + [jax.ShapeDtypeStruct((batch, 1, C_KV_HEADS, seq, C_DH), F32)] * 2,
        compiler_params=_cparams("arbitrary"),
        name="sink_attn",
    )(sink3, proj, proj, proj)


def _window_attn_kernel(sink_ref, q_ref, k_ref, v_ref, ck_ref, cv_ref, o_ref, *, band, hkb):
    tq = q_ref.shape[0]
    seq = k_ref.shape[0]
    scale = C_DH ** -0.5
    i = pl.program_id(2)
    start = pl.multiple_of(jnp.clip(i * tq - C_WINDOW, 0, seq - band), C_WINDOW)
    qpos = i * tq + lax.broadcasted_iota(jnp.int32, (tq, band), 0)
    kpos = start + lax.broadcasted_iota(jnp.int32, (tq, band), 1)
    valid = jnp.abs(qpos - kpos) <= C_WINDOW
    valid4 = jnp.concatenate([valid] * C_GROUP, axis=0)
    gw = C_GROUP * C_DH
    for hk in range(hkb):
        kcols = slice(hk * C_DH, (hk + 1) * C_DH)
        kb = k_ref[pl.ds(start, band), kcols].astype(BF16)
        vb = v_ref[pl.ds(start, band), kcols].astype(BF16)
        q4 = jnp.concatenate([q_ref[:, hk * gw + g * C_DH:hk * gw + (g + 1) * C_DH] for g in range(C_GROUP)],
                             axis=0).astype(BF16)
        sink = jnp.concatenate([jnp.broadcast_to(sink_ref[hk, 0:1, g:g + 1], (tq, 1)) for g in range(C_GROUP)],
                               axis=0)
        s_band = jnp.where(valid4, _dot_nt(q4, kb) * scale, -jnp.inf)
        s_ctx = _dot_nt(q4, ck_ref[0, 0, hk].astype(BF16)) * scale
        p_ctx, p_band = _softmax_parts([s_ctx, s_band], extra_logit=sink)
        o = (jnp.dot(p_ctx.astype(BF16), cv_ref[0, 0, hk].astype(BF16), preferred_element_type=F32)
             + jnp.dot(p_band.astype(BF16), vb, preferred_element_type=F32))
        for g in range(C_GROUP):
            o_ref[:, hk * gw + g * C_DH:hk * gw + (g + 1) * C_DH] = o[g * tq:(g + 1) * tq].astype(BF16)


def window_attention_latent(proj, sink, ctx_k, ctx_v, e, *, row_off, batch, seq, tq=256, hkb=2):
    tq = min(tq, seq)
    band = min(tq + 2 * C_WINDOW, seq)
    assert row_off % seq == 0 and seq % tq == 0 and tq % C_WINDOW == 0 and C_KV_HEADS % hkb == 0
    past = ctx_k.shape[3]
    qw, kw = hkb * C_GROUP * C_DH, hkb * C_DH
    rb = lambda b: (row_off + b * seq) // seq
    cspec = pl.BlockSpec((1, 1, hkb, past, C_DH), lambda b, hg, i: (b, e, hg, 0, 0))
    sink3 = sink.reshape(C_KV_HEADS, 1, C_GROUP)
    return pl.pallas_call(
        functools.partial(_window_attn_kernel, band=band, hkb=hkb),
        grid=(batch, C_KV_HEADS // hkb, seq // tq),
        in_specs=[pl.BlockSpec((hkb, 1, C_GROUP), lambda b, hg, i: (hg, 0, 0)),
                  pl.BlockSpec((tq, qw), lambda b, hg, i: ((row_off + b * seq) // tq + i, hg)),
                  pl.BlockSpec((seq, kw), lambda b, hg, i: (rb(b), C_Q_W // kw + hg)),
                  pl.BlockSpec((seq, kw), lambda b, hg, i: (rb(b), (C_Q_W + C_KV_W) // kw + hg)),
                  cspec, cspec],
        out_specs=pl.BlockSpec((tq, qw), lambda b, hg, i: (b * (seq // tq) + i, hg)),
        out_shape=jax.ShapeDtypeStruct((batch * seq, C_Q_W), BF16),
        compiler_params=_cparams("arbitrary", "arbitrary", "arbitrary"),
        name="window_attn",
    )(sink3, proj, proj, proj, ctx_k, ctx_v)


def kernel(x_prompt, x_sample, cache_a_k, cache_a_v, state_b_fwd, state_b_bwd, cache_c_k, cache_c_v, c, c_ctx, ada_w, ada_b, norm_g, ffn_w_in, ffn_w_out, ab_w_in, ab_w_out, a_lambda, a_subln_g, b_alpha_w, b_alpha_b, b_norm_g, c_w_in, c_w_out, c_sink, final_g):
    bp, sp, d = x_prompt.shape
    bs, ss, _ = x_sample.shape
    depth = ada_w.shape[0]
    mp, ms = bp * sp, bs * ss
    rows_info = (mp, ss)

    cond = jnp.concatenate([c_ctx[None, :], c, jnp.zeros((COND_ROWS - 1 - bs, d), F32)], axis=0)
    mods = adaln(cond, ada_w, ada_b).reshape(depth, COND_ROWS, N_MOD, d)
    rope_a = _rope_tables(ss, A_DH, 2) + (2 * A_QK_W,)
    rope_c = _rope_tables(ss, C_DH, 1) + (C_Q_W + C_KV_W,)
    ab_w_in, ab_w_out, c_w_in, c_w_out = (w.astype(BF16) for w in (ab_w_in, ab_w_out, c_w_in, c_w_out))

    a_k, a_v, b_f, b_b, c_k, c_v = [], [], [], [], [], []
    xs = [x_prompt.reshape(mp, d), x_sample.reshape(ms, d)]
    for l in range(depth):
        mod = mods[l]
        x = ffn_half_step(xs, mod[:, 0:3], norm_g[l, 0], ffn_w_in, ffn_w_out, (l, 0), rows_info)
        if l % 2 == 0:
            e = l // 2
            lam_init = 0.8 - 0.6 * math.exp(-0.3 * l)
            zeros = jnp.zeros((B_RANK, B_QK_W), F32)
            w_alpha = jnp.concatenate([jnp.concatenate([b_alpha_w[e, 0], zeros], axis=1),
                                       jnp.concatenate([zeros, b_alpha_w[e, 1]], axis=1)], axis=0)
            gate_w = (ab_w_in[e][:, AB_MAIN:], w_alpha, b_alpha_b[e].reshape(-1))
            proj, gates = mixer_in_proj(x, mod[:, 3:5], norm_g[l, 1], ab_w_in, e, AB_MAIN, rows_info, rope_a,
                                        gates=gate_w)
            attn_p, ak, av = diff_attention(proj, a_lambda[e], a_subln_g[e], lam_init, row_off=0, batch=bp,
                                            seq=sp, tq=sp, hb=A_HEADS, emit_kv=True)
            (attn_s,) = diff_attention(proj, a_lambda[e], a_subln_g[e], lam_init, row_off=mp, batch=bs,
                                       seq=ss, tq=min(256, ss), hb=2, ctx=(cache_a_k, cache_a_v, e))
            gla_p, sf, sb = gla_bidirectional(proj, gates, b_norm_g[e], row_off=0, batch=bp, seq=sp,
                                              emit_state=True)
            (gla_s,) = gla_bidirectional(proj, gates, b_norm_g[e], row_off=mp, batch=bs, seq=ss,
                                         states=(state_b_fwd, state_b_bwd, e))
            a_k.append(ak), a_v.append(av), b_f.append(sf), b_b.append(sb)
            x = mixer_out_proj([attn_p, gla_p], [attn_s, gla_s], ab_w_out, e, x, mod[:, 5:6], rows_info)
        else:
            o = l // 2
            proj = mixer_in_proj(x, mod[:, 3:5], norm_g[l, 1], c_w_in, o, c_w_in.shape[2], rows_info, rope_c)
            mix_p, ck, cv = sink_attention_context(proj, c_sink[o], batch=bp, seq=sp)
            mix_s = window_attention_latent(proj, c_sink[o], cache_c_k, cache_c_v, o, row_off=mp, batch=bs, seq=ss)
            c_k.append(ck), c_v.append(cv)
            x = mixer_out_proj([mix_p], [mix_s], c_w_out, o, x, mod[:, 5:6], rows_info)
        x = ffn_half_step([x], mod[:, 6:9], norm_g[l, 2], ffn_w_in, ffn_w_out, (l, 1), rows_info,
                          final_g=final_g if l == depth - 1 else None)
        xs = [x]

    y_prompt, y_sample = x[:mp].reshape(bp, sp, d), x[mp:].reshape(bs, ss, d)
    cat = lambda parts: parts[0] if len(parts) == 1 else jnp.concatenate(parts, axis=1)
    return (y_prompt, y_sample, cat(a_k), cat(a_v), cat(b_f), cat(b_b), cat(c_k), cat(c_v))
```

```python
import functools
import math

import numpy as np
import jax
import jax.numpy as jnp
from jax import lax
from jax.experimental import pallas as pl
from jax.experimental.pallas import tpu as pltpu

F32 = jnp.float32
BF16 = jnp.bfloat16

EPS = 1e-6
GRID_W = 64
ROPE_BASE = 10000.0
N_MOD = 9
A_HEADS, A_DH, A_DV = 8, 64, 128
B_HEADS, B_DK, B_DV, B_RANK, B_TAU = 4, 128, 256, 16, 16.0
C_HEADS, C_KV_HEADS, C_DH, C_WINDOW = 16, 4, 128, 128
A_QK_W = A_HEADS * 2 * A_DH
A_V_W = A_HEADS * A_DV
B_QK_W = B_HEADS * B_DK
B_V_W = B_HEADS * B_DV
AB_MAIN = 2 * A_QK_W + A_V_W + 2 * B_QK_W + 2 * B_V_W
C_GROUP = C_HEADS // C_KV_HEADS
C_Q_W = C_HEADS * C_DH
C_KV_W = C_KV_HEADS * C_DH

V7X_VMEM_BYTES = 64 * 1024 * 1024
VMEM_LIMIT_BYTES = V7X_VMEM_BYTES - 8 * 1024 * 1024
COND_ROWS = 8
ROW_CHUNK = 128


def _tile(n, pref, align=128):
    if n <= pref:
        return n
    t = (pref // align) * align
    while n % t:
        t -= align
    assert t > 0, (n, pref)
    return t


def _cparams(*sem):
    return pltpu.CompilerParams(dimension_semantics=sem, vmem_limit_bytes=VMEM_LIMIT_BYTES)


def _bdot(a, b):
    return jnp.dot(a.astype(BF16), b.astype(BF16), preferred_element_type=F32)


def _dot_nt(a, b):
    return lax.dot_general(a, b, (((1,), (1,)), ((), ())), preferred_element_type=F32)


def _dot_tn(a, b):
    return lax.dot_general(a, b, (((0,), (0,)), ((), ())), preferred_element_type=F32)


def _rms(x):
    return x * lax.rsqrt(jnp.mean(x * x, axis=-1, keepdims=True) + EPS)


def _adaln_kernel(cond_ref, w_ref, b_ref, o_ref):
    c = cond_ref[...]
    o_ref[0] = _bdot(c * jax.nn.sigmoid(c), w_ref[0]) + b_ref[0]


def adaln(cond, ada_w, ada_b):
    depth, d, n = ada_w.shape
    tn = _tile(n, 1024)
    return pl.pallas_call(
        _adaln_kernel,
        grid=(depth, n // tn),
        in_specs=[pl.BlockSpec((COND_ROWS, d), lambda l, j: (0, 0)),
                  pl.BlockSpec((1, d, tn), lambda l, j: (l, 0, j)),
                  pl.BlockSpec((1, 1, tn), lambda l, j: (l, 0, j))],
        out_specs=pl.BlockSpec((1, COND_ROWS, tn), lambda l, j: (l, 0, j)),
        out_shape=jax.ShapeDtypeStruct((depth, COND_ROWS, n), F32),
        compiler_params=_cparams("arbitrary", "arbitrary"),
        name="adaln",
    )(cond, ada_w, ada_b.reshape(depth, 1, n))


def _norm_modulate(x_ref, g_ref, mod_ref, h_ref):
    g = g_ref[...]
    shift = mod_ref[0, 0:1, :]
    scale1 = 1.0 + mod_ref[0, 1:2, :]

    def body(r, carry):
        rows = pl.ds(pl.multiple_of(r * ROW_CHUNK, ROW_CHUNK), ROW_CHUNK)
        h_ref[rows, :] = ((_rms(x_ref[rows, :]) * g) * scale1 + shift).astype(BF16)
        return carry

    lax.fori_loop(0, x_ref.shape[0] // ROW_CHUNK, body, 0)


def _row_tile(rows_info):
    n_prompt_rows, lat_rows = rows_info
    return _tile(math.gcd(n_prompt_rows, lat_rows), 1024, align=ROW_CHUNK)


def _cond_row(i, tm, n_prompt_rows, lat_rows):
    return jnp.maximum((i * tm - n_prompt_rows) // lat_rows + 1, 0)


def _ffn_kernel(*refs, n_x, n_prompt_tiles, row_split, n_chunk, final):
    x_refs, refs = refs[:n_x], refs[n_x:]
    mod_ref, g_ref, wg_ref, wu_ref, wo_ref = refs[:5]
    refs = refs[5:]
    if final:
        fg_ref, refs = refs[0], refs[1:]
    o_ref, h_ref = refs
    i, j = pl.program_id(0), pl.program_id(1)

    def with_x(fn):
        if n_x == 1:
            fn(x_refs[0])
        else:
            pl.when(i < n_prompt_tiles)(lambda: fn(x_refs[0]))
            pl.when(i >= n_prompt_tiles)(lambda: fn(x_refs[1]))

    @pl.when(j == 0)
    def _():
        with_x(lambda x_ref: _norm_modulate(x_ref, g_ref, mod_ref, h_ref))
        o_ref[...] = jnp.zeros(o_ref.shape, F32)

    tm, d = h_ref.shape
    rt = tm // row_split
    wg, wu, wo = (w_ref[...].astype(BF16) for w_ref in (wg_ref, wu_ref, wo_ref))
    for r in range(row_split):
        rows = slice(r * rt, (r + 1) * rt)
        h = h_ref[rows, :]
        gate = jnp.dot(h, wg, preferred_element_type=F32)
        up = jnp.dot(h, wu, preferred_element_type=F32)
        act = (gate * jax.nn.sigmoid(gate) * up).astype(BF16)
        for c in range(d // n_chunk):
            cols = slice(c * n_chunk, (c + 1) * n_chunk)
            o_ref[rows, cols] += jnp.dot(act, wo[:, cols], preferred_element_type=F32)

    @pl.when(j == pl.num_programs(1) - 1)
    def _():
        half_gate = 0.5 * mod_ref[0, 2:3, :]

        def finish(x_ref):
            def body(r, carry):
                rows = pl.ds(pl.multiple_of(r * ROW_CHUNK, ROW_CHUNK), ROW_CHUNK)
                xn = x_ref[rows, :] + half_gate * o_ref[rows, :]
                if final:
                    xn = _rms(xn) * fg_ref[...]
                o_ref[rows, :] = xn
                return carry

            lax.fori_loop(0, tm // ROW_CHUNK, body, 0)

        with_x(finish)


def ffn_half_step(xs, mod3, g, w_in, w_out, widx, rows_info, final_g=None, row_range=None):
    d = xs[0].shape[1]
    f = w_out.shape[-2]
    n_prompt_rows, lat_rows = rows_info
    tm, tf, n_chunk = _row_tile(rows_info), _tile(f, 256), _tile(d, 512)
    first_row, m = row_range if row_range is not None else (0, sum(x.shape[0] for x in xs))
    assert first_row % tm == 0 and m % tm == 0
    t0 = first_row // tm
    nf = f // tf
    npt = n_prompt_rows // tm
    final = final_g is not None
    l, k = widx
    if len(xs) == 1:
        x_specs = [pl.BlockSpec((tm, d), lambda i, j: (t0 + i, 0))]
    else:
        assert row_range is None
        x_specs = [pl.BlockSpec((tm, d), lambda i, j: (jnp.minimum(i, npt - 1), 0), pipeline_mode=pl.Buffered(1)),
                   pl.BlockSpec((tm, d), lambda i, j: (jnp.maximum(i - npt, 0), 0), pipeline_mode=pl.Buffered(1))]
    in_specs = x_specs + [
        pl.BlockSpec((1, 3, d), lambda i, j: (_cond_row(t0 + i, tm, n_prompt_rows, lat_rows), 0, 0)),
        pl.BlockSpec((1, d), lambda i, j: (0, 0)),
        pl.BlockSpec((None, None, d, tf), lambda i, j: (l, k, 0, j)),
        pl.BlockSpec((None, None, d, tf), lambda i, j: (l, k, 0, nf + j)),
        pl.BlockSpec((None, None, tf, d), lambda i, j: (l, k, j, 0))]
    args = list(xs) + [mod3, g.reshape(1, d), w_in, w_in, w_out]
    if final:
        in_specs.append(pl.BlockSpec((1, d), lambda i, j: (0, 0)))
        args.append(final_g.reshape(1, d))
    return pl.pallas_call(
        functools.partial(_ffn_kernel, n_x=len(xs), n_prompt_tiles=npt, row_split=max(1, tm // 512),
                          n_chunk=n_chunk, final=final),
        grid=(m // tm, nf),
        in_specs=in_specs,
        out_specs=pl.BlockSpec((tm, d), lambda i, j: (i, 0)),
        out_shape=jax.ShapeDtypeStruct((m, d), F32),
        scratch_shapes=[pltpu.VMEM((tm, d), BF16)],
        compiler_params=_cparams("arbitrary", "arbitrary"),
        name="ffn_final" if final else "ffn",
    )(*args)


def _log_sigmoid(z):
    return jnp.minimum(z, 0.0) - jnp.log(1.0 + jnp.exp(-jnp.abs(z)))


def _proj_kernel(*refs, n_prompt_tiles, n_rope_cols, rope_shift, tn, with_gates):
    x_ref, mod_ref, g_ref, w_ref, cos_ref, sa_ref, sb_ref = refs[:7]
    if with_gates:
        wl_ref, wa_ref, ba_ref, o_ref, gates_ref = refs[7:]
    else:
        (o_ref,) = refs[7:]
    x = x_ref[...]
    h = (((_rms(x) * g_ref[...]) * (1.0 + mod_ref[0, 1:2, :])) + mod_ref[0, 0:1, :]).astype(BF16)
    if with_gates:
        low = jnp.dot(h, wl_ref[...], preferred_element_type=F32)
        z = jnp.dot(low.astype(BF16), wa_ref[...].astype(BF16), preferred_element_type=F32) + ba_ref[...]
        gates_ref[...] = _log_sigmoid(z) * (1.0 / B_TAU)

    def project(rotary):
        for c in range(o_ref.shape[1] // tn):
            acc = jnp.dot(h, w_ref[:, c * tn:(c + 1) * tn], preferred_element_type=F32)
            if rotary and c * tn < n_rope_cols:
                cos, sa, sb = cos_ref[...], sa_ref[...], sb_ref[...]
                for s in range(tn // 128):
                    a = acc[:, s * 128:(s + 1) * 128]
                    o_ref[:, c * tn + s * 128:c * tn + (s + 1) * 128] = (
                        a * cos + pltpu.roll(a, 128 - rope_shift, 1) * sa + pltpu.roll(a, rope_shift, 1) * sb)
            else:
                o_ref[:, c * tn:(c + 1) * tn] = acc

    i = pl.program_id(0)
    pl.when(i < n_prompt_tiles)(lambda: project(False))
    pl.when(i >= n_prompt_tiles)(lambda: project(True))


def _rope_tables(n, head_dim, n_sub):
    assert head_dim * n_sub == 128
    rows = n // GRID_W
    row = jnp.repeat(jnp.arange(rows, dtype=F32), GRID_W)
    col = jnp.tile(jnp.arange(GRID_W, dtype=F32), rows)
    d_axis = head_dim // 2
    shift = d_axis // 2
    inv = ROPE_BASE ** (-jnp.arange(0, d_axis, 2, dtype=F32) / d_axis)
    lane = jnp.arange(128)
    sub = lane % head_dim
    is_col = (sub // d_axis) == 1
    within = sub % d_axis
    freq = within % shift
    second = (within // shift) == 1
    pos = jnp.where(is_col[None, :], col[:, None], row[:, None])
    ang = pos * inv[freq][None, :]
    cos, sin = jnp.cos(ang), jnp.sin(ang)
    sin_a = jnp.where(second[None, :], 0.0, -sin)
    sin_b = jnp.where(second[None, :], sin, 0.0)
    return cos, sin_a, sin_b, shift


def mixer_in_proj(x, mod2, g, w, e, n_cols, rows_info, rope, gates=None):
    m, d = x.shape
    n_prompt_rows, lat_rows = rows_info
    cos, sin_a, sin_b, rope_shift, n_rope_cols = rope
    tm = _tile(math.gcd(n_prompt_rows, lat_rows), 256, align=ROW_CHUNK)
    tn = _tile(math.gcd(n_cols, n_rope_cols), 512)
    npt, lat_tiles = n_prompt_rows // tm, lat_rows // tm
    tab = lambda i: (jnp.maximum(i - npt, 0) % lat_tiles, 0)
    once = dict(pipeline_mode=pl.Buffered(1))
    in_specs = [pl.BlockSpec((tm, d), lambda i: (i, 0)),
                pl.BlockSpec((1, 2, d), lambda i: (_cond_row(i, tm, n_prompt_rows, lat_rows), 0, 0)),
                pl.BlockSpec((1, d), lambda i: (0, 0)),
                pl.BlockSpec((None, d, n_cols), lambda i: (e, 0, 0), **once),
                pl.BlockSpec((tm, 128), tab), pl.BlockSpec((tm, 128), tab), pl.BlockSpec((tm, 128), tab)]
    args = [x, mod2, g.reshape(1, d), w, cos, sin_a, sin_b]
    out_specs = pl.BlockSpec((tm, n_cols), lambda i: (i, 0))
    out_shape = jax.ShapeDtypeStruct((m, n_cols), F32)
    if gates is not None:
        w_low, w_alpha, b_alpha = gates
        n_gate = w_alpha.shape[1]
        in_specs += [pl.BlockSpec(w_low.shape, lambda i: (0, 0)),
                     pl.BlockSpec(w_alpha.shape, lambda i: (0, 0)),
                     pl.BlockSpec((1, n_gate), lambda i: (0, 0))]
        args += [w_low, w_alpha, b_alpha.reshape(1, n_gate)]
        out_specs = [out_specs, pl.BlockSpec((tm, n_gate), lambda i: (i, 0))]
        out_shape = [out_shape, jax.ShapeDtypeStruct((m, n_gate), F32)]
    return pl.pallas_call(
        functools.partial(_proj_kernel, n_prompt_tiles=npt, n_rope_cols=n_rope_cols, rope_shift=rope_shift,
                          tn=tn, with_gates=gates is not None),
        grid=(m // tm,),
        in_specs=in_specs, out_specs=out_specs, out_shape=out_shape,
        compiler_params=_cparams("arbitrary"),
        name="mixer_in_proj_gated" if gates is not None else "mixer_in_proj",
    )(*args)


def _out_proj_kernel(*refs, n_parts, n_prompt_tiles, tn):
    parts_p, parts_s = refs[:n_parts], refs[n_parts:2 * n_parts]
    w_refs = refs[2 * n_parts:3 * n_parts]
    x_ref, gate_ref, o_ref = refs[3 * n_parts:]
    i = pl.program_id(0)

    def run(parts):
        for c in range(o_ref.shape[1] // tn):
            cols = slice(c * tn, (c + 1) * tn)
            acc = None
            for p_ref, w_ref in zip(parts, w_refs):
                t = jnp.dot(p_ref[...], w_ref[:, cols], preferred_element_type=F32)
                acc = t if acc is None else acc + t
            o_ref[:, cols] = x_ref[:, cols] + gate_ref[0, :, cols] * acc

    pl.when(i < n_prompt_tiles)(lambda: run(parts_p))
    pl.when(i >= n_prompt_tiles)(lambda: run(parts_s))


def mixer_out_proj(parts_p, parts_s, w, e, x, gate, rows_info):
    m, d = x.shape
    n_prompt_rows, lat_rows = rows_info
    tm = _tile(math.gcd(n_prompt_rows, lat_rows), 512, align=ROW_CHUNK)
    npt = n_prompt_rows // tm
    n_parts = len(parts_p)
    in_specs, w_specs, off = [], [], 0
    for p in parts_p:
        in_specs.append(pl.BlockSpec((tm, p.shape[1]), lambda i: (jnp.minimum(i, npt - 1), 0)))
    for p in parts_s:
        kp = p.shape[1]
        in_specs.append(pl.BlockSpec((tm, kp), lambda i: (jnp.maximum(i - npt, 0), 0)))
        assert off % kp == 0
        w_specs.append(pl.BlockSpec((None, kp, d), functools.partial(lambda i, rb: (e, rb, 0), rb=off // kp),
                                    pipeline_mode=pl.Buffered(1)))
        off += kp
    in_specs += w_specs
    in_specs += [pl.BlockSpec((tm, d), lambda i: (i, 0)),
                 pl.BlockSpec((1, 1, d), lambda i: (_cond_row(i, tm, n_prompt_rows, lat_rows), 0, 0))]
    return pl.pallas_call(
        functools.partial(_out_proj_kernel, n_parts=n_parts, n_prompt_tiles=npt, tn=_tile(d, 512)),
        grid=(m // tm,),
        in_specs=in_specs,
        out_specs=pl.BlockSpec((tm, d), lambda i: (i, 0)),
        out_shape=jax.ShapeDtypeStruct((m, d), F32),
        compiler_params=_cparams("arbitrary"),
        name="mixer_out_proj",
    )(*parts_p, *parts_s, *([w] * n_parts), x, gate)


def _softmax_parts(parts, extra_logit=None, weight=None):
    m = functools.reduce(jnp.maximum, [jnp.max(p, axis=-1, keepdims=True) for p in parts])
    if extra_logit is not None:
        m = jnp.maximum(m, extra_logit)
    es = [jnp.exp(p - m) for p in parts]
    den = functools.reduce(jnp.add, [jnp.sum(e, axis=-1, keepdims=True) for e in es])
    if extra_logit is not None:
        den = den + jnp.exp(extra_logit - m)
    inv = 1.0 / den if weight is None else weight / den
    return [e * inv for e in es]


def _diff_attn_kernel(*refs, hb, lam_init, has_ctx, emit_kv):
    it = iter(refs)
    lam_ref, g_ref, q_ref, k_ref, v_ref = (next(it) for _ in range(5))
    ck_ref, cv_ref = (next(it), next(it)) if has_ctx else (None, None)
    o_ref = next(it)
    nk_ref, nv_ref = (next(it), next(it)) if emit_kv else (None, None)

    lp = lam_ref[...]
    lam = (jnp.exp(jnp.sum(lp[0:1] * lp[1:2], axis=-1, keepdims=True))
           - jnp.exp(jnp.sum(lp[2:3] * lp[3:4], axis=-1, keepdims=True)) + lam_init)
    scale = A_DH ** -0.5
    assert math.frexp(scale)[0] == 0.5
    first = lax.broadcasted_iota(jnp.int32, (1, 2 * A_DH), 1) < A_DH
    tq = q_ref.shape[0]
    heads = [slice(hh * 128, (hh + 1) * 128) for hh in range(hb)]
    q1, q2, keys, vals = [], [], [], []
    for hh, cols in enumerate(heads):
        q, k, v = q_ref[:, cols] * scale, k_ref[:, cols], v_ref[:, cols]
        q1.append(jnp.where(first, q, 0.0).astype(BF16))
        q2.append(jnp.where(first, 0.0, q).astype(BF16))
        keys.append(([ck_ref[0, 0, hh].astype(BF16)] if has_ctx else []) + [k.astype(BF16)])
        vals.append(([cv_ref[0, 0, hh].astype(BF16)] if has_ctx else []) + [v.astype(BF16)])
        if emit_kv:
            nk_ref[0, 0, hh] = k
            nv_ref[0, 0, hh] = v
    parts = range(len(keys[0]))
    stack = lambda qs: [jnp.concatenate([_dot_nt(qs[hh], keys[hh][p]) for hh in range(hb)], axis=0) for p in parts]
    p1 = _softmax_parts(stack(q1))
    p2 = _softmax_parts(stack(q2), weight=lam)
    pd = [(a - b).astype(BF16) for a, b in zip(p1, p2)]
    outs = []
    for hh in range(hb):
        o = None
        for p in parts:
            t = jnp.dot(pd[p][hh * tq:(hh + 1) * tq], vals[hh][p], preferred_element_type=F32)
            o = t if o is None else o + t
        outs.append(o)
    y = ((_rms(jnp.concatenate(outs, axis=0)) * g_ref[...]) * (1.0 - lam_init)).astype(BF16)
    for hh, cols in enumerate(heads):
        o_ref[:, cols] = y[hh * tq:(hh + 1) * tq]


def diff_attention(proj, lam_p, subln_g, lam_init, *, row_off, batch, seq, tq, hb, ctx=None, emit_kv=False):
    width = hb * 128
    n_hg = A_HEADS // hb
    assert row_off % seq == 0 and seq % tq == 0
    qrow = lambda b, hg, i: ((row_off + b * seq) // tq + i, hg)
    krow = lambda b, hg, i: ((row_off + b * seq) // seq, A_QK_W // width + hg)
    vrow = lambda b, hg, i: ((row_off + b * seq) // seq, 2 * A_QK_W // width + hg)
    in_specs = [pl.BlockSpec(lam_p.shape, lambda b, hg, i: (0, 0)),
                pl.BlockSpec((1, A_DV), lambda b, hg, i: (0, 0)),
                pl.BlockSpec((tq, width), qrow),
                pl.BlockSpec((seq, width), krow),
                pl.BlockSpec((seq, width), vrow)]
    args = [lam_p, subln_g.reshape(1, A_DV), proj, proj, proj]
    if ctx is not None:
        ctx_k, ctx_v, e = ctx
        past = ctx_k.shape[3]
        cspec = lambda b, hg, i: (b, e, hg, 0, 0)
        in_specs += [pl.BlockSpec((1, 1, hb, past, 2 * A_DH), cspec), pl.BlockSpec((1, 1, hb, past, A_DV), cspec)]
        args += [ctx_k, ctx_v]
    out_specs = [pl.BlockSpec((tq, width), lambda b, hg, i: (b * (seq // tq) + i, hg))]
    out_shape = [jax.ShapeDtypeStruct((batch * seq, A_V_W), BF16)]
    if emit_kv:
        assert tq == seq
        kvspec = pl.BlockSpec((1, 1, hb, seq, 128), lambda b, hg, i: (b, 0, hg, 0, 0))
        out_specs += [kvspec, kvspec]
        out_shape += [jax.ShapeDtypeStruct((batch, 1, A_HEADS, seq, 128), F32)] * 2
    return pl.pallas_call(
        functools.partial(_diff_attn_kernel, hb=hb, lam_init=lam_init, has_ctx=ctx is not None, emit_kv=emit_kv),
        grid=(batch, n_hg, seq // tq),
        in_specs=in_specs, out_specs=out_specs, out_shape=out_shape,
        compiler_params=_cparams("arbitrary", "arbitrary", "arbitrary"),
        name="diff_attn_ctx" if ctx is not None else "diff_attn",
    )(*args)


def _gla_level_matrices(chunk, fwd):
    t = np.arange(chunk)[:, None]
    i = np.arange(chunk)[None, :]
    mats = [i <= t, i > t] if fwd else [i >= t, i < t]
    h = chunk // 2
    while h >= 1:
        p = t % (2 * h)
        base = t - p
        if fwd:
            m = base + h - 1
            a = np.where(p >= h, (i > m) & (i <= t), (i > t) & (i <= m))
        else:
            m = base + h
            a = np.where(p < h, (i >= t) & (i < m), (i >= m) & (i < t))
        mats.append(a)
        h //= 2
    return jnp.asarray(np.concatenate(mats, axis=0).astype(np.float32), dtype=BF16)


def _gla_kernel(*refs, chunk, has_state, emit_state):
    it = iter(refs)
    af_ref, ab_ref, q_ref, k_ref, v_ref, laf_ref, lab_ref, r_ref, g_ref = (next(it) for _ in range(9))
    sf_ref, sb_ref = (next(it), next(it)) if has_state else (None, None)
    o_ref = next(it)
    nsf_ref, nsb_ref = (next(it), next(it)) if emit_state else (None, None)
    accf_ref, accb_ref, stf_ref, stb_ref = (next(it) for _ in range(4))

    seq = q_ref.shape[0]
    n_chunks = seq // chunk
    n_lev = chunk.bit_length() - 1
    ti = lax.broadcasted_iota(jnp.int32, (chunk, chunk), 0)
    si = lax.broadcasted_iota(jnp.int32, (chunk, chunk), 1)
    split = ti ^ si
    scale = B_DK ** -0.5

    def chunk_step(c, fwd, a_ref, la_ref, acc_ref, st_ref):
        rows = c * chunk if isinstance(c, int) else pl.multiple_of(c * chunk, chunk)
        rows = pl.ds(rows, chunk)
        order = (ti > si) if fwd else (ti < si)
        q = q_ref[rows, :] * scale
        k = k_ref[rows, :]
        vb = v_ref[rows, :].astype(BF16)
        la = la_ref[rows, :]
        la_hi = la.astype(BF16)
        la_lo = (la - la_hi.astype(F32)).astype(BF16)
        r2 = jnp.dot(a_ref[...], jnp.concatenate([la_hi, la_lo], axis=1), preferred_element_type=F32)
        e = jnp.exp(r2[:, :B_DK] + r2[:, B_DK:])
        st = st_ref[...]
        inter = _dot_nt((q * e[0:chunk]).astype(BF16), st.astype(BF16))
        ku = (k * e[chunk:2 * chunk]).astype(BF16)
        scores = jnp.where(ti == si, _dot_nt(q.astype(BF16), k.astype(BF16)), 0.0)
        for lev in range(n_lev):
            half = chunk >> (lev + 1)
            f = e[(2 + lev) * chunk:(3 + lev) * chunk]
            sc = _dot_nt((q * f).astype(BF16), (k * f).astype(BF16))
            scores = jnp.where(order & (split >= half) & (split < 2 * half), sc, scores)
        acc_ref[rows, :] = inter + jnp.dot(scores.astype(BF16), vb, preferred_element_type=F32)
        total = e[chunk - 1:chunk] if fwd else e[0:1]
        st_ref[...] = st * total + _dot_tn(vb, ku)

    for s0_ref, st_ref in ((sf_ref, stf_ref), (sb_ref, stb_ref)):
        st_ref[...] = s0_ref[0, 0, 0].T if has_state else jnp.zeros(st_ref.shape, F32)

    def body(cc, carry):
        chunk_step(cc, True, af_ref, laf_ref, accf_ref, stf_ref)
        chunk_step(n_chunks - 1 - cc, False, ab_ref, lab_ref, accb_ref, stb_ref)
        return carry

    if n_chunks == 1:
        body(0, 0)
    else:
        lax.fori_loop(0, n_chunks, body, 0)
    if emit_state:
        nsf_ref[0, 0, 0] = stf_ref[...].T
        nsb_ref[0, 0, 0] = stb_ref[...].T

    def epilogue(r, carry):
        rows = pl.ds(pl.multiple_of(r * ROW_CHUNK, ROW_CHUNK), ROW_CHUNK)
        gate = r_ref[rows, :]
        o = accf_ref[rows, :] + accb_ref[rows, :]
        o_ref[rows, :] = ((_rms(o) * g_ref[...]) * (gate * jax.nn.sigmoid(gate))).astype(BF16)
        return carry

    lax.fori_loop(0, seq // ROW_CHUNK, epilogue, 0)


def gla_bidirectional(proj, gates, bnorm_g, *, row_off, batch, seq, states=None, emit_state=False):
    chunk = min(256, seq)
    assert row_off % seq == 0 and seq % chunk == 0 and chunk & (chunk - 1) == 0
    rb = lambda b: (row_off + b * seq) // seq
    q_off = (2 * A_QK_W + A_V_W) // B_DK
    k_off = q_off + B_HEADS
    v_off = (2 * A_QK_W + A_V_W + 2 * B_QK_W) // B_DV
    r_off = v_off + B_HEADS
    a_f, a_b = _gla_level_matrices(chunk, True), _gla_level_matrices(chunk, False)
    whole = lambda b, h: (0, 0)
    in_specs = [pl.BlockSpec(a_f.shape, whole), pl.BlockSpec(a_b.shape, whole),
                pl.BlockSpec((seq, B_DK), lambda b, h: (rb(b), q_off + h)),
                pl.BlockSpec((seq, B_DK), lambda b, h: (rb(b), k_off + h)),
                pl.BlockSpec((seq, B_DV), lambda b, h: (rb(b), v_off + h)),
                pl.BlockSpec((seq, B_DK), lambda b, h: (rb(b), h)),
                pl.BlockSpec((seq, B_DK), lambda b, h: (rb(b), B_HEADS + h)),
                pl.BlockSpec((seq, B_DV), lambda b, h: (rb(b), r_off + h)),
                pl.BlockSpec((1, B_DV), whole)]
    args = [a_f, a_b, proj, proj, proj, gates, gates, proj, bnorm_g.reshape(1, B_DV)]
    if states is not None:
        s_f, s_b, e = states
        sspec = pl.BlockSpec((1, 1, 1, B_DK, B_DV), lambda b, h: (b, e, h, 0, 0))
        in_specs += [sspec, sspec]
        args += [s_f, s_b]
    out_specs = [pl.BlockSpec((seq, B_DV), lambda b, h: (b, h))]
    out_shape = [jax.ShapeDtypeStruct((batch * seq, B_V_W), BF16)]
    if emit_state:
        nspec = pl.BlockSpec((1, 1, 1, B_DK, B_DV), lambda b, h: (b, 0, h, 0, 0))
        out_specs += [nspec, nspec]
        out_shape += [jax.ShapeDtypeStruct((batch, 1, B_HEADS, B_DK, B_DV), F32)] * 2
    return pl.pallas_call(
        functools.partial(_gla_kernel, chunk=chunk, has_state=states is not None, emit_state=emit_state),
        grid=(batch, B_HEADS),
        in_specs=in_specs, out_specs=out_specs, out_shape=out_shape,
        scratch_shapes=[pltpu.VMEM((seq, B_DV), F32)] * 2 + [pltpu.VMEM((B_DV, B_DK), F32)] * 2,
        compiler_params=_cparams("arbitrary", "arbitrary"),
        name="gla_state" if states is not None else "gla",
    )(*args)


def _sink_attn_kernel(sink_ref, q_ref, k_ref, v_ref, o_ref, nk_ref, nv_ref):
    seq = q_ref.shape[0]
    scale = C_DH ** -0.5
    scores, sinks, vals = [], [], []
    for hk in range(C_KV_HEADS):
        kcols = slice(hk * C_DH, (hk + 1) * C_DH)
        k, v = k_ref[:, kcols], v_ref[:, kcols]
        nk_ref[0, 0, hk] = k
        nv_ref[0, 0, hk] = v
        vals.append(v.astype(BF16))
        q4 = jnp.concatenate([q_ref[:, (hk * C_GROUP + g) * C_DH:(hk * C_GROUP + g + 1) * C_DH]
                              for g in range(C_GROUP)], axis=0).astype(BF16)
        scores.append(_dot_nt(q4, k.astype(BF16)) * scale)
        sinks += [jnp.broadcast_to(sink_ref[hk, 0:1, g:g + 1], (seq, 1)) for g in range(C_GROUP)]
    (p,) = _softmax_parts([jnp.concatenate(scores, axis=0)], extra_logit=jnp.concatenate(sinks, axis=0))
    p = p.astype(BF16)
    rows_per_kv = C_GROUP * seq
    for hk in range(C_KV_HEADS):
        o = jnp.dot(p[hk * rows_per_kv:(hk + 1) * rows_per_kv], vals[hk], preferred_element_type=F32)
        for g in range(C_GROUP):
            h = hk * C_GROUP + g
            o_ref[:, h * C_DH:(h + 1) * C_DH] = o[g * seq:(g + 1) * seq].astype(BF16)


def sink_attention_context(proj, sink, *, batch, seq):
    sink3 = sink.reshape(C_KV_HEADS, 1, C_GROUP)
    kvspec = pl.BlockSpec((1, 1, C_KV_HEADS, seq, C_DH), lambda b: (b, 0, 0, 0, 0))
    return pl.pallas_call(
        _sink_attn_kernel,
        grid=(batch,),
        in_specs=[pl.BlockSpec(sink3.shape, lambda b: (0, 0, 0)),
                  pl.BlockSpec((seq, C_Q_W), lambda b: (b, 0)),
                  pl.BlockSpec((seq, C_KV_W), lambda b: (b, C_Q_W // C_KV_W)),
                  pl.BlockSpec((seq, C_KV_W), lambda b: (b, C_Q_W // C_KV_W + 1))],
        out_specs=[pl.BlockSpec((seq, C_Q_W), lambda b: (b, 0)), kvspec, kvspec],
        out_shape=[jax.ShapeDtypeStruct((batch * seq, C_Q_W), BF16)]
        + [jax.ShapeDtypeStruct((batch, 1, C_KV_HEADS, seq, C_DH), F32)] * 2,
        compiler_params=_cparams("arbitrary"),
        name="sink_attn",
    )(sink3, proj, proj, proj)


def _window_attn_kernel(sink_ref, q_ref, k_ref, v_ref, ck_ref, cv_ref, o_ref, *, band, hkb):
    tq = q_ref.shape[0]
    seq = k_ref.shape[0]
    scale = C_DH ** -0.5
    i = pl.program_id(2)
    start = pl.multiple_of(jnp.clip(i * tq - C_WINDOW, 0, seq - band), C_WINDOW)
    qpos = i * tq + lax.broadcasted_iota(jnp.int32, (tq, band), 0)
    kpos = start + lax.broadcasted_iota(jnp.int32, (tq, band), 1)
    valid = jnp.abs(qpos - kpos) <= C_WINDOW
    valid4 = jnp.concatenate([valid] * C_GROUP, axis=0)
    gw = C_GROUP * C_DH
    s_ctx, s_band, sinks, vbs = [], [], [], []
    for hk in range(hkb):
        kcols = slice(hk * C_DH, (hk + 1) * C_DH)
        kb = k_ref[pl.ds(start, band), kcols].astype(BF16)
        vbs.append(v_ref[pl.ds(start, band), kcols].astype(BF16))
        q4 = jnp.concatenate([q_ref[:, hk * gw + g * C_DH:hk * gw + (g + 1) * C_DH] for g in range(C_GROUP)],
                             axis=0).astype(BF16)
        sinks += [jnp.broadcast_to(sink_ref[hk, 0:1, g:g + 1], (tq, 1)) for g in range(C_GROUP)]
        s_band.append(jnp.where(valid4, _dot_nt(q4, kb) * scale, -jnp.inf))
        s_ctx.append(_dot_nt(q4, ck_ref[0, 0, hk].astype(BF16)) * scale)
    p_ctx, p_band = _softmax_parts([jnp.concatenate(s_ctx, axis=0), jnp.concatenate(s_band, axis=0)],
                                   extra_logit=jnp.concatenate(sinks, axis=0))
    p_ctx, p_band = p_ctx.astype(BF16), p_band.astype(BF16)
    rows_per_kv = C_GROUP * tq
    for hk in range(hkb):
        rows = slice(hk * rows_per_kv, (hk + 1) * rows_per_kv)
        o = (jnp.dot(p_ctx[rows], cv_ref[0, 0, hk].astype(BF16), preferred_element_type=F32)
             + jnp.dot(p_band[rows], vbs[hk], preferred_element_type=F32))
        for g in range(C_GROUP):
            o_ref[:, hk * gw + g * C_DH:hk * gw + (g + 1) * C_DH] = o[g * tq:(g + 1) * tq].astype(BF16)


def window_attention_latent(proj, sink, ctx_k, ctx_v, e, *, row_off, batch, seq, tq=128, hkb=4):
    tq = min(tq, seq)
    band = min(tq + 2 * C_WINDOW, seq)
    assert row_off % seq == 0 and seq % tq == 0 and tq % C_WINDOW == 0 and C_KV_HEADS % hkb == 0
    past = ctx_k.shape[3]
    qw, kw = hkb * C_GROUP * C_DH, hkb * C_DH
    rb = lambda b: (row_off + b * seq) // seq
    cspec = pl.BlockSpec((1, 1, hkb, past, C_DH), lambda b, hg, i: (b, e, hg, 0, 0))
    sink3 = sink.reshape(C_KV_HEADS, 1, C_GROUP)
    return pl.pallas_call(
        functools.partial(_window_attn_kernel, band=band, hkb=hkb),
        grid=(batch, C_KV_HEADS // hkb, seq // tq),
        in_specs=[pl.BlockSpec((hkb, 1, C_GROUP), lambda b, hg, i: (hg, 0, 0)),
                  pl.BlockSpec((tq, qw), lambda b, hg, i: ((row_off + b * seq) // tq + i, hg)),
                  pl.BlockSpec((seq, kw), lambda b, hg, i: (rb(b), C_Q_W // kw + hg)),
                  pl.BlockSpec((seq, kw), lambda b, hg, i: (rb(b), (C_Q_W + C_KV_W) // kw + hg)),
                  cspec, cspec],
        out_specs=pl.BlockSpec((tq, qw), lambda b, hg, i: (b * (seq // tq) + i, hg)),
        out_shape=jax.ShapeDtypeStruct((batch * seq, C_Q_W), BF16),
        compiler_params=_cparams("arbitrary", "arbitrary", "arbitrary"),
        name="window_attn",
    )(sink3, proj, proj, proj, ctx_k, ctx_v)


def kernel(x_prompt, x_sample, cache_a_k, cache_a_v, state_b_fwd, state_b_bwd, cache_c_k, cache_c_v, c, c_ctx, ada_w, ada_b, norm_g, ffn_w_in, ffn_w_out, ab_w_in, ab_w_out, a_lambda, a_subln_g, b_alpha_w, b_alpha_b, b_norm_g, c_w_in, c_w_out, c_sink, final_g):
    bp, sp, d = x_prompt.shape
    bs, ss, _ = x_sample.shape
    depth = ada_w.shape[0]
    mp, ms = bp * sp, bs * ss
    rows_info = (mp, ss)

    cond = jnp.concatenate([c_ctx[None, :], c, jnp.zeros((COND_ROWS - 1 - bs, d), F32)], axis=0)
    mods = adaln(cond, ada_w, ada_b).reshape(depth, COND_ROWS, N_MOD, d)
    rope_a = _rope_tables(ss, A_DH, 2) + (2 * A_QK_W,)
    rope_c = _rope_tables(ss, C_DH, 1) + (C_Q_W + C_KV_W,)
    ab_w_in, ab_w_out, c_w_in, c_w_out = (w.astype(BF16) for w in (ab_w_in, ab_w_out, c_w_in, c_w_out))

    a_k, a_v, b_f, b_b, c_k, c_v = [], [], [], [], [], []
    xs = [x_prompt.reshape(mp, d), x_sample.reshape(ms, d)]
    for l in range(depth):
        mod = mods[l]
        x = ffn_half_step(xs, mod[:, 0:3], norm_g[l, 0], ffn_w_in, ffn_w_out, (l, 0), rows_info)
        if l % 2 == 0:
            e = l // 2
            lam_init = 0.8 - 0.6 * math.exp(-0.3 * l)
            zeros = jnp.zeros((B_RANK, B_QK_W), F32)
            w_alpha = jnp.concatenate([jnp.concatenate([b_alpha_w[e, 0], zeros], axis=1),
                                       jnp.concatenate([zeros, b_alpha_w[e, 1]], axis=1)], axis=0)
            gate_w = (ab_w_in[e][:, AB_MAIN:], w_alpha, b_alpha_b[e].reshape(-1))
            proj, gates = mixer_in_proj(x, mod[:, 3:5], norm_g[l, 1], ab_w_in, e, AB_MAIN, rows_info, rope_a,
                                        gates=gate_w)
            attn_p, ak, av = diff_attention(proj, a_lambda[e], a_subln_g[e], lam_init, row_off=0, batch=bp,
                                            seq=sp, tq=sp, hb=A_HEADS, emit_kv=True)
            (attn_s,) = diff_attention(proj, a_lambda[e], a_subln_g[e], lam_init, row_off=mp, batch=bs,
                                       seq=ss, tq=min(256, ss), hb=2, ctx=(cache_a_k, cache_a_v, e))
            gla_p, sf, sb = gla_bidirectional(proj, gates, b_norm_g[e], row_off=0, batch=bp, seq=sp,
                                              emit_state=True)
            (gla_s,) = gla_bidirectional(proj, gates, b_norm_g[e], row_off=mp, batch=bs, seq=ss,
                                         states=(state_b_fwd, state_b_bwd, e))
            a_k.append(ak), a_v.append(av), b_f.append(sf), b_b.append(sb)
            x = mixer_out_proj([attn_p, gla_p], [attn_s, gla_s], ab_w_out, e, x, mod[:, 5:6], rows_info)
        else:
            o = l // 2
            proj = mixer_in_proj(x, mod[:, 3:5], norm_g[l, 1], c_w_in, o, c_w_in.shape[2], rows_info, rope_c)
            mix_p, ck, cv = sink_attention_context(proj, c_sink[o], batch=bp, seq=sp)
            mix_s = window_attention_latent(proj, c_sink[o], cache_c_k, cache_c_v, o, row_off=mp, batch=bs, seq=ss)
            c_k.append(ck), c_v.append(cv)
            x = mixer_out_proj([mix_p], [mix_s], c_w_out, o, x, mod[:, 5:6], rows_info)
        ffn2 = functools.partial(ffn_half_step, [x], mod[:, 6:9], norm_g[l, 2], ffn_w_in, ffn_w_out, (l, 1), rows_info)
        if l < depth - 1:
            xs = [ffn2()]

    y_prompt = ffn2(final_g=final_g, row_range=(0, mp)).reshape(bp, sp, d)
    y_sample = ffn2(final_g=final_g, row_range=(mp, ms)).reshape(bs, ss, d)
    cat = lambda parts: parts[0] if len(parts) == 1 else jnp.concatenate(parts, axis=1)
    return (y_prompt, y_sample, cat(a_k), cat(a_v), cat(b_f), cat(b_b), cat(c_k), cat(c_v))
```

```python
import functools
import math

import numpy as np
import jax
import jax.numpy as jnp
from jax import lax
from jax.experimental import pallas as pl
from jax.experimental.pallas import tpu as pltpu

F32 = jnp.float32
BF16 = jnp.bfloat16

EPS = 1e-6
GRID_W = 64
ROPE_BASE = 10000.0
N_MOD = 9
A_HEADS, A_DH, A_DV = 8, 64, 128
B_HEADS, B_DK, B_DV, B_RANK, B_TAU = 4, 128, 256, 16, 16.0
C_HEADS, C_KV_HEADS, C_DH, C_WINDOW = 16, 4, 128, 128
A_QK_W = A_HEADS * 2 * A_DH
A_V_W = A_HEADS * A_DV
B_QK_W = B_HEADS * B_DK
B_V_W = B_HEADS * B_DV
AB_MAIN = 2 * A_QK_W + A_V_W + 2 * B_QK_W + 2 * B_V_W
C_GROUP = C_HEADS // C_KV_HEADS
C_Q_W = C_HEADS * C_DH
C_KV_W = C_KV_HEADS * C_DH

V7X_VMEM_BYTES = 64 * 1024 * 1024
VMEM_LIMIT_BYTES = V7X_VMEM_BYTES - 8 * 1024 * 1024
COND_ROWS = 8
ROW_CHUNK = 128


def _tile(n, pref, align=128):
    if n <= pref:
        return n
    t = (pref // align) * align
    while n % t:
        t -= align
    assert t > 0, (n, pref)
    return t


def _cparams(*sem):
    return pltpu.CompilerParams(dimension_semantics=sem, vmem_limit_bytes=VMEM_LIMIT_BYTES)


def _bdot(a, b):
    return jnp.dot(a.astype(BF16), b.astype(BF16), preferred_element_type=F32)


def _dot_nt(a, b):
    return lax.dot_general(a, b, (((1,), (1,)), ((), ())), preferred_element_type=F32)


def _dot_tn(a, b):
    return lax.dot_general(a, b, (((0,), (0,)), ((), ())), preferred_element_type=F32)


def _rms(x):
    return x * lax.rsqrt(jnp.mean(x * x, axis=-1, keepdims=True) + EPS)


def _adaln_kernel(cond_ref, w_ref, b_ref, o_ref):
    c = cond_ref[...]
    o_ref[0] = _bdot(c * jax.nn.sigmoid(c), w_ref[0]) + b_ref[0]


def adaln(cond, ada_w, ada_b):
    depth, d, n = ada_w.shape
    tn = _tile(n, 1024)
    return pl.pallas_call(
        _adaln_kernel,
        grid=(depth, n // tn),
        in_specs=[pl.BlockSpec((COND_ROWS, d), lambda l, j: (0, 0)),
                  pl.BlockSpec((1, d, tn), lambda l, j: (l, 0, j)),
                  pl.BlockSpec((1, 1, tn), lambda l, j: (l, 0, j))],
        out_specs=pl.BlockSpec((1, COND_ROWS, tn), lambda l, j: (l, 0, j)),
        out_shape=jax.ShapeDtypeStruct((depth, COND_ROWS, n), F32),
        compiler_params=_cparams("arbitrary", "arbitrary"),
        name="adaln",
    )(cond, ada_w, ada_b.reshape(depth, 1, n))


def _norm_modulate(x_ref, g_ref, mod_ref, h_ref):
    g = g_ref[...]
    shift = mod_ref[0, 0:1, :]
    scale1 = 1.0 + mod_ref[0, 1:2, :]

    def body(r, carry):
        rows = pl.ds(pl.multiple_of(r * ROW_CHUNK, ROW_CHUNK), ROW_CHUNK)
        h_ref[rows, :] = ((_rms(x_ref[rows, :]) * g) * scale1 + shift).astype(BF16)
        return carry

    lax.fori_loop(0, x_ref.shape[0] // ROW_CHUNK, body, 0)


def _row_tile(rows_info):
    n_prompt_rows, lat_rows = rows_info
    return _tile(math.gcd(n_prompt_rows, lat_rows), 1024, align=ROW_CHUNK)


def _cond_row(i, tm, n_prompt_rows, lat_rows):
    return jnp.maximum((i * tm - n_prompt_rows) // lat_rows + 1, 0)


def _ffn_kernel(x_ref, mod_ref, g_ref, wg_ref, wu_ref, wo_ref, *rest, row_split, n_chunk, final):
    if final:
        fg_ref, o_ref, h_ref = rest
    else:
        o_ref, h_ref = rest
    j = pl.program_id(1)

    @pl.when(j == 0)
    def _():
        _norm_modulate(x_ref, g_ref, mod_ref, h_ref)
        o_ref[...] = jnp.zeros(o_ref.shape, F32)

    tm, d = h_ref.shape
    rt = tm // row_split
    wg, wu, wo = (w_ref[...].astype(BF16) for w_ref in (wg_ref, wu_ref, wo_ref))
    for r in range(row_split):
        rows = slice(r * rt, (r + 1) * rt)
        h = h_ref[rows, :]
        gate = jnp.dot(h, wg, preferred_element_type=F32)
        up = jnp.dot(h, wu, preferred_element_type=F32)
        act = (gate * jax.nn.sigmoid(gate) * up).astype(BF16)
        for c in range(d // n_chunk):
            cols = slice(c * n_chunk, (c + 1) * n_chunk)
            o_ref[rows, cols] += jnp.dot(act, wo[:, cols], preferred_element_type=F32)

    @pl.when(j == pl.num_programs(1) - 1)
    def _():
        half_gate = 0.5 * mod_ref[0, 2:3, :]

        def body(r, carry):
            rows = pl.ds(pl.multiple_of(r * ROW_CHUNK, ROW_CHUNK), ROW_CHUNK)
            xn = x_ref[rows, :] + half_gate * o_ref[rows, :]
            if final:
                xn = _rms(xn) * fg_ref[...]
            o_ref[rows, :] = xn
            return carry

        lax.fori_loop(0, tm // ROW_CHUNK, body, 0)


def ffn_half_step(x, mod3, g, w_in, w_out, widx, rows_info, *, stream_row0=0, x_row0=0, n_rows=None,
                  final_g=None):
    d = x.shape[1]
    f = w_out.shape[-2]
    n_prompt_rows, lat_rows = rows_info
    tm, tf, n_chunk = _row_tile(rows_info), _tile(f, 256), _tile(d, 512)
    m = x.shape[0] if n_rows is None else n_rows
    assert x_row0 % tm == 0 and stream_row0 % tm == 0 and m % tm == 0
    tx, ts = x_row0 // tm, stream_row0 // tm
    nf = f // tf
    final = final_g is not None
    l, k = widx
    in_specs = [
        pl.BlockSpec((tm, d), lambda i, j: (tx + i, 0)),
        pl.BlockSpec((1, 3, d), lambda i, j: (_cond_row(ts + i, tm, n_prompt_rows, lat_rows), 0, 0)),
        pl.BlockSpec((1, d), lambda i, j: (0, 0)),
        pl.BlockSpec((None, None, d, tf), lambda i, j: (l, k, 0, j)),
        pl.BlockSpec((None, None, d, tf), lambda i, j: (l, k, 0, nf + j)),
        pl.BlockSpec((None, None, tf, d), lambda i, j: (l, k, j, 0))]
    args = [x, mod3, g.reshape(1, d), w_in, w_in, w_out]
    if final:
        in_specs.append(pl.BlockSpec((1, d), lambda i, j: (0, 0)))
        args.append(final_g.reshape(1, d))
    return pl.pallas_call(
        functools.partial(_ffn_kernel, row_split=max(1, tm // 512), n_chunk=n_chunk, final=final),
        grid=(m // tm, nf),
        in_specs=in_specs,
        out_specs=pl.BlockSpec((tm, d), lambda i, j: (i, 0)),
        out_shape=jax.ShapeDtypeStruct((m, d), F32),
        scratch_shapes=[pltpu.VMEM((tm, d), BF16)],
        compiler_params=_cparams("arbitrary", "arbitrary"),
        name="ffn_final" if final else "ffn",
    )(*args)


def _log_sigmoid(z):
    return jnp.minimum(z, 0.0) - jnp.log(1.0 + jnp.exp(-jnp.abs(z)))


def _proj_kernel(*refs, n_x, n_prompt_tiles, n_rope_cols, rope_shift, tn, with_gates):
    x_refs, refs = refs[:n_x], refs[n_x:]
    mod_ref, g_ref, w_ref, cos_ref, sa_ref, sb_ref = refs[:6]
    if with_gates:
        wl_ref, wa_ref, ba_ref, o_ref, gates_ref = refs[6:]
    else:
        (o_ref,) = refs[6:]

    def project(x_ref, rotary):
        x = x_ref[...]
        h = (((_rms(x) * g_ref[...]) * (1.0 + mod_ref[0, 1:2, :])) + mod_ref[0, 0:1, :]).astype(BF16)
        if with_gates:
            low = jnp.dot(h, wl_ref[...], preferred_element_type=F32)
            z = jnp.dot(low.astype(BF16), wa_ref[...].astype(BF16), preferred_element_type=F32) + ba_ref[...]
            gates_ref[...] = _log_sigmoid(z) * (1.0 / B_TAU)
        for c in range(o_ref.shape[1] // tn):
            acc = jnp.dot(h, w_ref[:, c * tn:(c + 1) * tn], preferred_element_type=F32)
            if rotary and c * tn < n_rope_cols:
                cos, sa, sb = cos_ref[...], sa_ref[...], sb_ref[...]
                for s in range(tn // 128):
                    a = acc[:, s * 128:(s + 1) * 128]
                    o_ref[:, c * tn + s * 128:c * tn + (s + 1) * 128] = (
                        a * cos + pltpu.roll(a, 128 - rope_shift, 1) * sa + pltpu.roll(a, rope_shift, 1) * sb)
            else:
                o_ref[:, c * tn:(c + 1) * tn] = acc

    i = pl.program_id(0)
    pl.when(i < n_prompt_tiles)(lambda: project(x_refs[0], False))
    pl.when(i >= n_prompt_tiles)(lambda: project(x_refs[-1], True))


def _rope_tables(n, head_dim, n_sub):
    assert head_dim * n_sub == 128
    rows = n // GRID_W
    row = jnp.repeat(jnp.arange(rows, dtype=F32), GRID_W)
    col = jnp.tile(jnp.arange(GRID_W, dtype=F32), rows)
    d_axis = head_dim // 2
    shift = d_axis // 2
    inv = ROPE_BASE ** (-jnp.arange(0, d_axis, 2, dtype=F32) / d_axis)
    lane = jnp.arange(128)
    sub = lane % head_dim
    is_col = (sub // d_axis) == 1
    within = sub % d_axis
    freq = within % shift
    second = (within // shift) == 1
    pos = jnp.where(is_col[None, :], col[:, None], row[:, None])
    ang = pos * inv[freq][None, :]
    cos, sin = jnp.cos(ang), jnp.sin(ang)
    sin_a = jnp.where(second[None, :], 0.0, -sin)
    sin_b = jnp.where(second[None, :], sin, 0.0)
    return cos, sin_a, sin_b, shift


def _stream_specs(xs, tm, npt):
    d = xs[0].shape[1]
    if len(xs) == 1:
        return [pl.BlockSpec((tm, d), lambda i: (i, 0))]
    return [pl.BlockSpec((tm, d), lambda i: (jnp.minimum(i, npt - 1), 0)),
            pl.BlockSpec((tm, d), lambda i: (jnp.maximum(i - npt, 0), 0))]


def mixer_in_proj(xs, mod2, g, w, e, n_cols, rows_info, rope, gates=None):
    m, d = sum(x.shape[0] for x in xs), xs[0].shape[1]
    n_prompt_rows, lat_rows = rows_info
    cos, sin_a, sin_b, rope_shift, n_rope_cols = rope
    tm = _tile(math.gcd(n_prompt_rows, lat_rows), 256, align=ROW_CHUNK)
    tn = _tile(math.gcd(n_cols, n_rope_cols), 512)
    npt, lat_tiles = n_prompt_rows // tm, lat_rows // tm
    tab = lambda i: (jnp.maximum(i - npt, 0) % lat_tiles, 0)
    once = dict(pipeline_mode=pl.Buffered(1))
    in_specs = _stream_specs(xs, tm, npt) + [
        pl.BlockSpec((1, 2, d), lambda i: (_cond_row(i, tm, n_prompt_rows, lat_rows), 0, 0)),
        pl.BlockSpec((1, d), lambda i: (0, 0)),
        pl.BlockSpec((None, d, n_cols), lambda i: (e, 0, 0), **once),
        pl.BlockSpec((tm, 128), tab), pl.BlockSpec((tm, 128), tab), pl.BlockSpec((tm, 128), tab)]
    args = list(xs) + [mod2, g.reshape(1, d), w, cos, sin_a, sin_b]
    out_specs = pl.BlockSpec((tm, n_cols), lambda i: (i, 0))
    out_shape = jax.ShapeDtypeStruct((m, n_cols), F32)
    if gates is not None:
        w_low, w_alpha, b_alpha = gates
        n_gate = w_alpha.shape[1]
        in_specs += [pl.BlockSpec(w_low.shape, lambda i: (0, 0)),
                     pl.BlockSpec(w_alpha.shape, lambda i: (0, 0)),
                     pl.BlockSpec((1, n_gate), lambda i: (0, 0))]
        args += [w_low, w_alpha, b_alpha.reshape(1, n_gate)]
        out_specs = [out_specs, pl.BlockSpec((tm, n_gate), lambda i: (i, 0))]
        out_shape = [out_shape, jax.ShapeDtypeStruct((m, n_gate), F32)]
    return pl.pallas_call(
        functools.partial(_proj_kernel, n_x=len(xs), n_prompt_tiles=npt, n_rope_cols=n_rope_cols,
                          rope_shift=rope_shift, tn=tn, with_gates=gates is not None),
        grid=(m // tm,),
        in_specs=in_specs, out_specs=out_specs, out_shape=out_shape,
        compiler_params=_cparams("arbitrary"),
        name="mixer_in_proj_gated" if gates is not None else "mixer_in_proj",
    )(*args)


def _out_proj_kernel(*refs, n_parts, n_x, n_prompt_tiles, tn):
    parts_p, parts_s = refs[:n_parts], refs[n_parts:2 * n_parts]
    w_refs = refs[2 * n_parts:3 * n_parts]
    x_refs = refs[3 * n_parts:3 * n_parts + n_x]
    gate_ref, o_ref = refs[3 * n_parts + n_x:]
    i = pl.program_id(0)

    def run(parts, x_ref):
        for c in range(o_ref.shape[1] // tn):
            cols = slice(c * tn, (c + 1) * tn)
            acc = None
            for p_ref, w_ref in zip(parts, w_refs):
                t = jnp.dot(p_ref[...], w_ref[:, cols], preferred_element_type=F32)
                acc = t if acc is None else acc + t
            o_ref[:, cols] = x_ref[:, cols] + gate_ref[0, :, cols] * acc

    pl.when(i < n_prompt_tiles)(lambda: run(parts_p, x_refs[0]))
    pl.when(i >= n_prompt_tiles)(lambda: run(parts_s, x_refs[-1]))


def mixer_out_proj(parts_p, parts_s, w, e, xs, gate, rows_info):
    m, d = sum(x.shape[0] for x in xs), xs[0].shape[1]
    n_prompt_rows, lat_rows = rows_info
    tm = _tile(math.gcd(n_prompt_rows, lat_rows), 512, align=ROW_CHUNK)
    npt = n_prompt_rows // tm
    n_parts = len(parts_p)
    in_specs, w_specs, off = [], [], 0
    for p in parts_p:
        in_specs.append(pl.BlockSpec((tm, p.shape[1]), lambda i: (jnp.minimum(i, npt - 1), 0)))
    for p in parts_s:
        kp = p.shape[1]
        in_specs.append(pl.BlockSpec((tm, kp), lambda i: (jnp.maximum(i - npt, 0), 0)))
        assert off % kp == 0
        w_specs.append(pl.BlockSpec((None, kp, d), functools.partial(lambda i, rb: (e, rb, 0), rb=off // kp),
                                    pipeline_mode=pl.Buffered(1)))
        off += kp
    in_specs += w_specs
    in_specs += _stream_specs(xs, tm, npt)
    in_specs += [pl.BlockSpec((1, 1, d), lambda i: (_cond_row(i, tm, n_prompt_rows, lat_rows), 0, 0))]
    return pl.pallas_call(
        functools.partial(_out_proj_kernel, n_parts=n_parts, n_x=len(xs), n_prompt_tiles=npt, tn=_tile(d, 512)),
        grid=(m // tm,),
        in_specs=in_specs,
        out_specs=pl.BlockSpec((tm, d), lambda i: (i, 0)),
        out_shape=jax.ShapeDtypeStruct((m, d), F32),
        compiler_params=_cparams("arbitrary"),
        name="mixer_out_proj",
    )(*parts_p, *parts_s, *([w] * n_parts), *xs, gate)


def _softmax_parts(parts, extra_logit=None, weight=None):
    m = functools.reduce(jnp.maximum, [jnp.max(p, axis=-1, keepdims=True) for p in parts])
    if extra_logit is not None:
        m = jnp.maximum(m, extra_logit)
    es = [jnp.exp(p - m) for p in parts]
    den = functools.reduce(jnp.add, [jnp.sum(e, axis=-1, keepdims=True) for e in es])
    if extra_logit is not None:
        den = den + jnp.exp(extra_logit - m)
    inv = 1.0 / den if weight is None else weight / den
    return [e * inv for e in es]


def _diff_attn_kernel(*refs, hb, lam_init, has_ctx, emit_kv):
    it = iter(refs)
    lam_ref, g_ref, q_ref, k_ref, v_ref = (next(it) for _ in range(5))
    ck_ref, cv_ref = (next(it), next(it)) if has_ctx else (None, None)
    o_ref = next(it)
    nk_ref, nv_ref = (next(it), next(it)) if emit_kv else (None, None)

    lp = lam_ref[...]
    lam = (jnp.exp(jnp.sum(lp[0:1] * lp[1:2], axis=-1, keepdims=True))
           - jnp.exp(jnp.sum(lp[2:3] * lp[3:4], axis=-1, keepdims=True)) + lam_init)
    scale = A_DH ** -0.5
    assert math.frexp(scale)[0] == 0.5
    first = lax.broadcasted_iota(jnp.int32, (1, 2 * A_DH), 1) < A_DH
    tq = q_ref.shape[0]
    heads = [slice(hh * 128, (hh + 1) * 128) for hh in range(hb)]
    q1, q2, keys, vals = [], [], [], []
    for hh, cols in enumerate(heads):
        q, k, v = q_ref[:, cols] * scale, k_ref[:, cols], v_ref[:, cols]
        q1.append(jnp.where(first, q, 0.0).astype(BF16))
        q2.append(jnp.where(first, 0.0, q).astype(BF16))
        keys.append(([ck_ref[0, 0, hh].astype(BF16)] if has_ctx else []) + [k.astype(BF16)])
        vals.append(([cv_ref[0, 0, hh].astype(BF16)] if has_ctx else []) + [v.astype(BF16)])
        if emit_kv:
            nk_ref[0, 0, hh] = k
            nv_ref[0, 0, hh] = v
    parts = range(len(keys[0]))
    stack = lambda qs: [jnp.concatenate([_dot_nt(qs[hh], keys[hh][p]) for hh in range(hb)], axis=0) for p in parts]
    p1 = _softmax_parts(stack(q1))
    p2 = _softmax_parts(stack(q2), weight=lam)
    pd = [(a - b).astype(BF16) for a, b in zip(p1, p2)]
    outs = []
    for hh in range(hb):
        o = None
        for p in parts:
            t = jnp.dot(pd[p][hh * tq:(hh + 1) * tq], vals[hh][p], preferred_element_type=F32)
            o = t if o is None else o + t
        outs.append(o)
    y = ((_rms(jnp.concatenate(outs, axis=0)) * g_ref[...]) * (1.0 - lam_init)).astype(BF16)
    for hh, cols in enumerate(heads):
        o_ref[:, cols] = y[hh * tq:(hh + 1) * tq]


def diff_attention(proj, lam_p, subln_g, lam_init, *, row_off, batch, seq, tq, hb, ctx=None, emit_kv=False):
    width = hb * 128
    n_hg = A_HEADS // hb
    assert row_off % seq == 0 and seq % tq == 0
    qrow = lambda b, hg, i: ((row_off + b * seq) // tq + i, hg)
    krow = lambda b, hg, i: ((row_off + b * seq) // seq, A_QK_W // width + hg)
    vrow = lambda b, hg, i: ((row_off + b * seq) // seq, 2 * A_QK_W // width + hg)
    in_specs = [pl.BlockSpec(lam_p.shape, lambda b, hg, i: (0, 0)),
                pl.BlockSpec((1, A_DV), lambda b, hg, i: (0, 0)),
                pl.BlockSpec((tq, width), qrow),
                pl.BlockSpec((seq, width), krow),
                pl.BlockSpec((seq, width), vrow)]
    args = [lam_p, subln_g.reshape(1, A_DV), proj, proj, proj]
    if ctx is not None:
        ctx_k, ctx_v, e = ctx
        past = ctx_k.shape[3]
        cspec = lambda b, hg, i: (b, e, hg, 0, 0)
        in_specs += [pl.BlockSpec((1, 1, hb, past, 2 * A_DH), cspec), pl.BlockSpec((1, 1, hb, past, A_DV), cspec)]
        args += [ctx_k, ctx_v]
    out_specs = [pl.BlockSpec((tq, width), lambda b, hg, i: (b * (seq // tq) + i, hg))]
    out_shape = [jax.ShapeDtypeStruct((batch * seq, A_V_W), BF16)]
    if emit_kv:
        assert tq == seq
        kvspec = pl.BlockSpec((1, 1, hb, seq, 128), lambda b, hg, i: (b, 0, hg, 0, 0))
        out_specs += [kvspec, kvspec]
        out_shape += [jax.ShapeDtypeStruct((batch, 1, A_HEADS, seq, 128), F32)] * 2
    return pl.pallas_call(
        functools.partial(_diff_attn_kernel, hb=hb, lam_init=lam_init, has_ctx=ctx is not None, emit_kv=emit_kv),
        grid=(batch, n_hg, seq // tq),
        in_specs=in_specs, out_specs=out_specs, out_shape=out_shape,
        compiler_params=_cparams("arbitrary", "arbitrary", "arbitrary"),
        name="diff_attn_ctx" if ctx is not None else "diff_attn",
    )(*args)


def _gla_level_matrices(chunk, fwd):
    t = np.arange(chunk)[:, None]
    i = np.arange(chunk)[None, :]
    mats = [i <= t, i > t] if fwd else [i >= t, i < t]
    h = chunk // 2
    while h >= 1:
        p = t % (2 * h)
        base = t - p
        if fwd:
            m = base + h - 1
            a = np.where(p >= h, (i > m) & (i <= t), (i > t) & (i <= m))
        else:
            m = base + h
            a = np.where(p < h, (i >= t) & (i < m), (i >= m) & (i < t))
        mats.append(a)
        h //= 2
    return jnp.asarray(np.concatenate(mats, axis=0).astype(np.float32), dtype=BF16)


def _gla_kernel(*refs, chunk, has_state, emit_state):
    it = iter(refs)
    af_ref, ab_ref, q_ref, k_ref, v_ref, laf_ref, lab_ref, r_ref, g_ref = (next(it) for _ in range(9))
    sf_ref, sb_ref = (next(it), next(it)) if has_state else (None, None)
    o_ref = next(it)
    nsf_ref, nsb_ref = (next(it), next(it)) if emit_state else (None, None)
    accf_ref, accb_ref, stf_ref, stb_ref = (next(it) for _ in range(4))

    seq = q_ref.shape[0]
    n_chunks = seq // chunk
    n_lev = chunk.bit_length() - 1
    ti = lax.broadcasted_iota(jnp.int32, (chunk, chunk), 0)
    si = lax.broadcasted_iota(jnp.int32, (chunk, chunk), 1)
    split = ti ^ si
    scale = B_DK ** -0.5

    def chunk_step(c, fwd, a_ref, la_ref, acc_ref, st_ref):
        rows = c * chunk if isinstance(c, int) else pl.multiple_of(c * chunk, chunk)
        rows = pl.ds(rows, chunk)
        order = (ti > si) if fwd else (ti < si)
        q = q_ref[rows, :] * scale
        k = k_ref[rows, :]
        vb = v_ref[rows, :].astype(BF16)
        la = la_ref[rows, :]
        la_hi = la.astype(BF16)
        la_lo = (la - la_hi.astype(F32)).astype(BF16)
        r2 = jnp.dot(a_ref[...], jnp.concatenate([la_hi, la_lo], axis=1), preferred_element_type=F32)
        e = jnp.exp(r2[:, :B_DK] + r2[:, B_DK:])
        st = st_ref[...]
        inter = _dot_nt((q * e[0:chunk]).astype(BF16), st.astype(BF16))
        ku = (k * e[chunk:2 * chunk]).astype(BF16)
        scores = jnp.where(ti == si, _dot_nt(q.astype(BF16), k.astype(BF16)), 0.0)
        for lev in range(n_lev):
            half = chunk >> (lev + 1)
            f = e[(2 + lev) * chunk:(3 + lev) * chunk]
            sc = _dot_nt((q * f).astype(BF16), (k * f).astype(BF16))
            scores = jnp.where(order & (split >= half) & (split < 2 * half), sc, scores)
        acc_ref[rows, :] = inter + jnp.dot(scores.astype(BF16), vb, preferred_element_type=F32)
        total = e[chunk - 1:chunk] if fwd else e[0:1]
        st_ref[...] = st * total + _dot_tn(vb, ku)

    for s0_ref, st_ref in ((sf_ref, stf_ref), (sb_ref, stb_ref)):
        st_ref[...] = s0_ref[0, 0, 0].T if has_state else jnp.zeros(st_ref.shape, F32)

    def body(cc, carry):
        chunk_step(cc, True, af_ref, laf_ref, accf_ref, stf_ref)
        chunk_step(n_chunks - 1 - cc, False, ab_ref, lab_ref, accb_ref, stb_ref)
        return carry

    if n_chunks == 1:
        body(0, 0)
    else:
        lax.fori_loop(0, n_chunks, body, 0)
    if emit_state:
        nsf_ref[0, 0, 0] = stf_ref[...].T
        nsb_ref[0, 0, 0] = stb_ref[...].T

    def epilogue(r, carry):
        rows = pl.ds(pl.multiple_of(r * ROW_CHUNK, ROW_CHUNK), ROW_CHUNK)
        gate = r_ref[rows, :]
        o = accf_ref[rows, :] + accb_ref[rows, :]
        o_ref[rows, :] = ((_rms(o) * g_ref[...]) * (gate * jax.nn.sigmoid(gate))).astype(BF16)
        return carry

    lax.fori_loop(0, seq // ROW_CHUNK, epilogue, 0)


def gla_bidirectional(proj, gates, bnorm_g, *, row_off, batch, seq, states=None, emit_state=False):
    chunk = min(256, seq)
    assert row_off % seq == 0 and seq % chunk == 0 and chunk & (chunk - 1) == 0
    rb = lambda b: (row_off + b * seq) // seq
    q_off = (2 * A_QK_W + A_V_W) // B_DK
    k_off = q_off + B_HEADS
    v_off = (2 * A_QK_W + A_V_W + 2 * B_QK_W) // B_DV
    r_off = v_off + B_HEADS
    a_f, a_b = _gla_level_matrices(chunk, True), _gla_level_matrices(chunk, False)
    whole = lambda b, h: (0, 0)
    in_specs = [pl.BlockSpec(a_f.shape, whole), pl.BlockSpec(a_b.shape, whole),
                pl.BlockSpec((seq, B_DK), lambda b, h: (rb(b), q_off + h)),
                pl.BlockSpec((seq, B_DK), lambda b, h: (rb(b), k_off + h)),
                pl.BlockSpec((seq, B_DV), lambda b, h: (rb(b), v_off + h)),
                pl.BlockSpec((seq, B_DK), lambda b, h: (rb(b), h)),
                pl.BlockSpec((seq, B_DK), lambda b, h: (rb(b), B_HEADS + h)),
                pl.BlockSpec((seq, B_DV), lambda b, h: (rb(b), r_off + h)),
                pl.BlockSpec((1, B_DV), whole)]
    args = [a_f, a_b, proj, proj, proj, gates, gates, proj, bnorm_g.reshape(1, B_DV)]
    if states is not None:
        s_f, s_b, e = states
        sspec = pl.BlockSpec((1, 1, 1, B_DK, B_DV), lambda b, h: (b, e, h, 0, 0))
        in_specs += [sspec, sspec]
        args += [s_f, s_b]
    out_specs = [pl.BlockSpec((seq, B_DV), lambda b, h: (b, h))]
    out_shape = [jax.ShapeDtypeStruct((batch * seq, B_V_W), BF16)]
    if emit_state:
        nspec = pl.BlockSpec((1, 1, 1, B_DK, B_DV), lambda b, h: (b, 0, h, 0, 0))
        out_specs += [nspec, nspec]
        out_shape += [jax.ShapeDtypeStruct((batch, 1, B_HEADS, B_DK, B_DV), F32)] * 2
    return pl.pallas_call(
        functools.partial(_gla_kernel, chunk=chunk, has_state=states is not None, emit_state=emit_state),
        grid=(batch, B_HEADS),
        in_specs=in_specs, out_specs=out_specs, out_shape=out_shape,
        scratch_shapes=[pltpu.VMEM((seq, B_DV), F32)] * 2 + [pltpu.VMEM((B_DV, B_DK), F32)] * 2,
        compiler_params=_cparams("arbitrary", "arbitrary"),
        name="gla_state" if states is not None else "gla",
    )(*args)


def _sink_attn_kernel(sink_ref, q_ref, k_ref, v_ref, o_ref, nk_ref, nv_ref):
    seq = q_ref.shape[0]
    scale = C_DH ** -0.5
    scores, sinks, vals = [], [], []
    for hk in range(C_KV_HEADS):
        kcols = slice(hk * C_DH, (hk + 1) * C_DH)
        k, v = k_ref[:, kcols], v_ref[:, kcols]
        nk_ref[0, 0, hk] = k
        nv_ref[0, 0, hk] = v
        vals.append(v.astype(BF16))
        q4 = jnp.concatenate([q_ref[:, (hk * C_GROUP + g) * C_DH:(hk * C_GROUP + g + 1) * C_DH]
                              for g in range(C_GROUP)], axis=0).astype(BF16)
        scores.append(_dot_nt(q4, k.astype(BF16)) * scale)
        sinks += [jnp.broadcast_to(sink_ref[hk, 0:1, g:g + 1], (seq, 1)) for g in range(C_GROUP)]
    (p,) = _softmax_parts([jnp.concatenate(scores, axis=0)], extra_logit=jnp.concatenate(sinks, axis=0))
    p = p.astype(BF16)
    rows_per_kv = C_GROUP * seq
    for hk in range(C_KV_HEADS):
        o = jnp.dot(p[hk * rows_per_kv:(hk + 1) * rows_per_kv], vals[hk], preferred_element_type=F32)
        for g in range(C_GROUP):
            h = hk * C_GROUP + g
            o_ref[:, h * C_DH:(h + 1) * C_DH] = o[g * seq:(g + 1) * seq].astype(BF16)


def sink_attention_context(proj, sink, *, batch, seq):
    sink3 = sink.reshape(C_KV_HEADS, 1, C_GROUP)
    kvspec = pl.BlockSpec((1, 1, C_KV_HEADS, seq, C_DH), lambda b: (b, 0, 0, 0, 0))
    return pl.pallas_call(
        _sink_attn_kernel,
        grid=(batch,),
        in_specs=[pl.BlockSpec(sink3.shape, lambda b: (0, 0, 0)),
                  pl.BlockSpec((seq, C_Q_W), lambda b: (b, 0)),
                  pl.BlockSpec((seq, C_KV_W), lambda b: (b, C_Q_W // C_KV_W)),
                  pl.BlockSpec((seq, C_KV_W), lambda b: (b, C_Q_W // C_KV_W + 1))],
        out_specs=[pl.BlockSpec((seq, C_Q_W), lambda b: (b, 0)), kvspec, kvspec],
        out_shape=[jax.ShapeDtypeStruct((batch * seq, C_Q_W), BF16)]
        + [jax.ShapeDtypeStruct((batch, 1, C_KV_HEADS, seq, C_DH), F32)] * 2,
        compiler_params=_cparams("arbitrary"),
        name="sink_attn",
    )(sink3, proj, proj, proj)


def _window_attn_kernel(sink_ref, q_ref, k_ref, v_ref, ck_ref, cv_ref, o_ref, *, band, hkb):
    tq = q_ref.shape[0]
    seq = k_ref.shape[0]
    scale = C_DH ** -0.5
    i = pl.program_id(2)
    start = pl.multiple_of(jnp.clip(i * tq - C_WINDOW, 0, seq - band), C_WINDOW)
    qpos = i * tq + lax.broadcasted_iota(jnp.int32, (tq, band), 0)
    kpos = start + lax.broadcasted_iota(jnp.int32, (tq, band), 1)
    valid = jnp.abs(qpos - kpos) <= C_WINDOW
    valid4 = jnp.concatenate([valid] * C_GROUP, axis=0)
    gw = C_GROUP * C_DH
    s_ctx, s_band, sinks, vbs = [], [], [], []
    for hk in range(hkb):
        kcols = slice(hk * C_DH, (hk + 1) * C_DH)
        kb = k_ref[pl.ds(start, band), kcols].astype(BF16)
        vbs.append(v_ref[pl.ds(start, band), kcols].astype(BF16))
        q4 = jnp.concatenate([q_ref[:, hk * gw + g * C_DH:hk * gw + (g + 1) * C_DH] for g in range(C_GROUP)],
                             axis=0).astype(BF16)
        sinks += [jnp.broadcast_to(sink_ref[hk, 0:1, g:g + 1], (tq, 1)) for g in range(C_GROUP)]
        s_band.append(jnp.where(valid4, _dot_nt(q4, kb) * scale, -jnp.inf))
        s_ctx.append(_dot_nt(q4, ck_ref[0, 0, hk].astype(BF16)) * scale)
    p_ctx, p_band = _softmax_parts([jnp.concatenate(s_ctx, axis=0), jnp.concatenate(s_band, axis=0)],
                                   extra_logit=jnp.concatenate(sinks, axis=0))
    p_ctx, p_band = p_ctx.astype(BF16), p_band.astype(BF16)
    rows_per_kv = C_GROUP * tq
    for hk in range(hkb):
        rows = slice(hk * rows_per_kv, (hk + 1) * rows_per_kv)
        o = (jnp.dot(p_ctx[rows], cv_ref[0, 0, hk].astype(BF16), preferred_element_type=F32)
             + jnp.dot(p_band[rows], vbs[hk], preferred_element_type=F32))
        for g in range(C_GROUP):
            o_ref[:, hk * gw + g * C_DH:hk * gw + (g + 1) * C_DH] = o[g * tq:(g + 1) * tq].astype(BF16)


def window_attention_latent(proj, sink, ctx_k, ctx_v, e, *, row_off, batch, seq, tq=128, hkb=4):
    tq = min(tq, seq)
    band = min(tq + 2 * C_WINDOW, seq)
    assert row_off % seq == 0 and seq % tq == 0 and tq % C_WINDOW == 0 and C_KV_HEADS % hkb == 0
    past = ctx_k.shape[3]
    qw, kw = hkb * C_GROUP * C_DH, hkb * C_DH
    rb = lambda b: (row_off + b * seq) // seq
    cspec = pl.BlockSpec((1, 1, hkb, past, C_DH), lambda b, hg, i: (b, e, hg, 0, 0))
    sink3 = sink.reshape(C_KV_HEADS, 1, C_GROUP)
    return pl.pallas_call(
        functools.partial(_window_attn_kernel, band=band, hkb=hkb),
        grid=(batch, C_KV_HEADS // hkb, seq // tq),
        in_specs=[pl.BlockSpec((hkb, 1, C_GROUP), lambda b, hg, i: (hg, 0, 0)),
                  pl.BlockSpec((tq, qw), lambda b, hg, i: ((row_off + b * seq) // tq + i, hg)),
                  pl.BlockSpec((seq, kw), lambda b, hg, i: (rb(b), C_Q_W // kw + hg)),
                  pl.BlockSpec((seq, kw), lambda b, hg, i: (rb(b), (C_Q_W + C_KV_W) // kw + hg)),
                  cspec, cspec],
        out_specs=pl.BlockSpec((tq, qw), lambda b, hg, i: (b * (seq // tq) + i, hg)),
        out_shape=jax.ShapeDtypeStruct((batch * seq, C_Q_W), BF16),
        compiler_params=_cparams("arbitrary", "arbitrary", "arbitrary"),
        name="window_attn",
    )(sink3, proj, proj, proj, ctx_k, ctx_v)


def kernel(x_prompt, x_sample, cache_a_k, cache_a_v, state_b_fwd, state_b_bwd, cache_c_k, cache_c_v, c, c_ctx, ada_w, ada_b, norm_g, ffn_w_in, ffn_w_out, ab_w_in, ab_w_out, a_lambda, a_subln_g, b_alpha_w, b_alpha_b, b_norm_g, c_w_in, c_w_out, c_sink, final_g):
    bp, sp, d = x_prompt.shape
    bs, ss, _ = x_sample.shape
    depth = ada_w.shape[0]
    mp, ms = bp * sp, bs * ss
    rows_info = (mp, ss)

    cond = jnp.concatenate([c_ctx[None, :], c, jnp.zeros((COND_ROWS - 1 - bs, d), F32)], axis=0)
    mods = adaln(cond, ada_w, ada_b).reshape(depth, COND_ROWS, N_MOD, d)
    rope_a = _rope_tables(ss, A_DH, 2) + (2 * A_QK_W,)
    rope_c = _rope_tables(ss, C_DH, 1) + (C_Q_W + C_KV_W,)
    ab_w_in, ab_w_out, c_w_in, c_w_out = (w.astype(BF16) for w in (ab_w_in, ab_w_out, c_w_in, c_w_out))

    a_k, a_v, b_f, b_b, c_k, c_v = [], [], [], [], [], []
    xs = [x_prompt.reshape(mp, d), x_sample.reshape(ms, d)]
    for l in range(depth):
        mod = mods[l]
        ffn1 = functools.partial(ffn_half_step, mod3=mod[:, 0:3], g=norm_g[l, 0], w_in=ffn_w_in, w_out=ffn_w_out,
                                 widx=(l, 0), rows_info=rows_info)
        xs = [ffn1(xs[0])] if len(xs) == 1 else [ffn1(xs[0]), ffn1(xs[1], stream_row0=mp)]
        if l % 2 == 0:
            e = l // 2
            lam_init = 0.8 - 0.6 * math.exp(-0.3 * l)
            zeros = jnp.zeros((B_RANK, B_QK_W), F32)
            w_alpha = jnp.concatenate([jnp.concatenate([b_alpha_w[e, 0], zeros], axis=1),
                                       jnp.concatenate([zeros, b_alpha_w[e, 1]], axis=1)], axis=0)
            gate_w = (ab_w_in[e][:, AB_MAIN:], w_alpha, b_alpha_b[e].reshape(-1))
            proj, gates = mixer_in_proj(xs, mod[:, 3:5], norm_g[l, 1], ab_w_in, e, AB_MAIN, rows_info, rope_a,
                                        gates=gate_w)
            attn_p, ak, av = diff_attention(proj, a_lambda[e], a_subln_g[e], lam_init, row_off=0, batch=bp,
                                            seq=sp, tq=sp, hb=A_HEADS, emit_kv=True)
            (attn_s,) = diff_attention(proj, a_lambda[e], a_subln_g[e], lam_init, row_off=mp, batch=bs,
                                       seq=ss, tq=min(256, ss), hb=2, ctx=(cache_a_k, cache_a_v, e))
            gla_p, sf, sb = gla_bidirectional(proj, gates, b_norm_g[e], row_off=0, batch=bp, seq=sp,
                                              emit_state=True)
            (gla_s,) = gla_bidirectional(proj, gates, b_norm_g[e], row_off=mp, batch=bs, seq=ss,
                                         states=(state_b_fwd, state_b_bwd, e))
            a_k.append(ak), a_v.append(av), b_f.append(sf), b_b.append(sb)
            x = mixer_out_proj([attn_p, gla_p], [attn_s, gla_s], ab_w_out, e, xs, mod[:, 5:6], rows_info)
        else:
            o = l // 2
            proj = mixer_in_proj(xs, mod[:, 3:5], norm_g[l, 1], c_w_in, o, c_w_in.shape[2], rows_info, rope_c)
            mix_p, ck, cv = sink_attention_context(proj, c_sink[o], batch=bp, seq=sp)
            mix_s = window_attention_latent(proj, c_sink[o], cache_c_k, cache_c_v, o, row_off=mp, batch=bs, seq=ss)
            c_k.append(ck), c_v.append(cv)
            x = mixer_out_proj([mix_p], [mix_s], c_w_out, o, xs, mod[:, 5:6], rows_info)
        ffn2 = functools.partial(ffn_half_step, x, mod[:, 6:9], norm_g[l, 2], ffn_w_in, ffn_w_out, (l, 1), rows_info)
        if l < depth - 1:
            xs = [ffn2()]

    y_prompt = ffn2(n_rows=mp, final_g=final_g).reshape(bp, sp, d)
    y_sample = ffn2(stream_row0=mp, x_row0=mp, n_rows=ms, final_g=final_g).reshape(bs, ss, d)
    cat = lambda parts: parts[0] if len(parts) == 1 else jnp.concatenate(parts, axis=1)
    return (y_prompt, y_sample, cat(a_k), cat(a_v), cat(b_f), cat(b_b), cat(c_k), cat(c_v))
```

```python
import functools
import math

import numpy as np
import jax
import jax.numpy as jnp
from jax import lax
from jax.experimental import pallas as pl
from jax.experimental.pallas import tpu as pltpu

F32 = jnp.float32
BF16 = jnp.bfloat16

EPS = 1e-6
GRID_W = 64
ROPE_BASE = 10000.0
N_MOD = 9
A_HEADS, A_DH, A_DV = 8, 64, 128
B_HEADS, B_DK, B_DV, B_RANK, B_TAU = 4, 128, 256, 16, 16.0
C_HEADS, C_KV_HEADS, C_DH, C_WINDOW = 16, 4, 128, 128
A_QK_W = A_HEADS * 2 * A_DH
A_V_W = A_HEADS * A_DV
B_QK_W = B_HEADS * B_DK
B_V_W = B_HEADS * B_DV
AB_MAIN = 2 * A_QK_W + A_V_W + 2 * B_QK_W + 2 * B_V_W
C_GROUP = C_HEADS // C_KV_HEADS
C_Q_W = C_HEADS * C_DH
C_KV_W = C_KV_HEADS * C_DH

V7X_VMEM_BYTES = 64 * 1024 * 1024
VMEM_LIMIT_BYTES = V7X_VMEM_BYTES - 8 * 1024 * 1024
COND_ROWS = 8
ROW_CHUNK = 128


def _tile(n, pref, align=128):
    if n <= pref:
        return n
    t = (pref // align) * align
    while n % t:
        t -= align
    assert t > 0, (n, pref)
    return t


def _cparams(*sem):
    return pltpu.CompilerParams(dimension_semantics=sem, vmem_limit_bytes=VMEM_LIMIT_BYTES)


def _bdot(a, b):
    return jnp.dot(a.astype(BF16), b.astype(BF16), preferred_element_type=F32)


def _dot_nt(a, b):
    return lax.dot_general(a, b, (((1,), (1,)), ((), ())), preferred_element_type=F32)


def _dot_tn(a, b):
    return lax.dot_general(a, b, (((0,), (0,)), ((), ())), preferred_element_type=F32)


def _rms(x):
    return x * lax.rsqrt(jnp.mean(x * x, axis=-1, keepdims=True) + EPS)


def _adaln_kernel(cond_ref, w_ref, b_ref, o_ref):
    c = cond_ref[...]
    o_ref[0] = _bdot(c * jax.nn.sigmoid(c), w_ref[0]) + b_ref[0]


def adaln(cond, ada_w, ada_b):
    depth, d, n = ada_w.shape
    tn = _tile(n, 1024)
    return pl.pallas_call(
        _adaln_kernel,
        grid=(depth, n // tn),
        in_specs=[pl.BlockSpec((COND_ROWS, d), lambda l, j: (0, 0)),
                  pl.BlockSpec((1, d, tn), lambda l, j: (l, 0, j)),
                  pl.BlockSpec((1, 1, tn), lambda l, j: (l, 0, j))],
        out_specs=pl.BlockSpec((1, COND_ROWS, tn), lambda l, j: (l, 0, j)),
        out_shape=jax.ShapeDtypeStruct((depth, COND_ROWS, n), F32),
        compiler_params=_cparams("arbitrary", "arbitrary"),
        name="adaln",
    )(cond, ada_w, ada_b.reshape(depth, 1, n))


def _norm_modulate(x_ref, g_ref, mod_ref, h_ref):
    g = g_ref[...]
    shift = mod_ref[0, 0:1, :]
    scale1 = 1.0 + mod_ref[0, 1:2, :]

    def body(r, carry):
        rows = pl.ds(pl.multiple_of(r * ROW_CHUNK, ROW_CHUNK), ROW_CHUNK)
        h_ref[rows, :] = ((_rms(x_ref[rows, :]) * g) * scale1 + shift).astype(BF16)
        return carry

    lax.fori_loop(0, x_ref.shape[0] // ROW_CHUNK, body, 0)


def _row_tile(rows_info):
    n_prompt_rows, lat_rows = rows_info
    return _tile(math.gcd(n_prompt_rows, lat_rows), 1024, align=ROW_CHUNK)


def _cond_row(i, tm, n_prompt_rows, lat_rows):
    return jnp.maximum((i * tm - n_prompt_rows) // lat_rows + 1, 0)


def _ffn_kernel(x_ref, mod_ref, g_ref, wg_ref, wu_ref, wo_ref, *rest, row_split, n_chunk, final):
    if final:
        fg_ref, o_ref, h_ref = rest
    else:
        o_ref, h_ref = rest
    j = pl.program_id(1)

    @pl.when(j == 0)
    def _():
        _norm_modulate(x_ref, g_ref, mod_ref, h_ref)
        o_ref[...] = jnp.zeros(o_ref.shape, F32)

    tm, d = h_ref.shape
    rt = tm // row_split
    wg, wu, wo = (w_ref[...].astype(BF16) for w_ref in (wg_ref, wu_ref, wo_ref))
    for r in range(row_split):
        rows = slice(r * rt, (r + 1) * rt)
        h = h_ref[rows, :]
        gate = jnp.dot(h, wg, preferred_element_type=F32)
        up = jnp.dot(h, wu, preferred_element_type=F32)
        act = (gate * jax.nn.sigmoid(gate) * up).astype(BF16)
        for c in range(d // n_chunk):
            cols = slice(c * n_chunk, (c + 1) * n_chunk)
            o_ref[rows, cols] += jnp.dot(act, wo[:, cols], preferred_element_type=F32)

    @pl.when(j == pl.num_programs(1) - 1)
    def _():
        half_gate = 0.5 * mod_ref[0, 2:3, :]

        def body(r, carry):
            rows = pl.ds(pl.multiple_of(r * ROW_CHUNK, ROW_CHUNK), ROW_CHUNK)
            xn = x_ref[rows, :] + half_gate * o_ref[rows, :]
            if final:
                xn = _rms(xn) * fg_ref[...]
            o_ref[rows, :] = xn
            return carry

        lax.fori_loop(0, tm // ROW_CHUNK, body, 0)


def ffn_half_step(x, mod3, g, w_in, w_out, widx, rows_info, *, stream_row0=0, x_row0=0, n_rows=None,
                  final_g=None):
    d = x.shape[1]
    f = w_out.shape[-2]
    n_prompt_rows, lat_rows = rows_info
    tm, tf, n_chunk = _row_tile(rows_info), _tile(f, 256), _tile(d, 512)
    m = x.shape[0] if n_rows is None else n_rows
    assert x_row0 % tm == 0 and stream_row0 % tm == 0 and m % tm == 0
    tx, ts = x_row0 // tm, stream_row0 // tm
    nf = f // tf
    final = final_g is not None
    l, k = widx
    in_specs = [
        pl.BlockSpec((tm, d), lambda i, j: (tx + i, 0)),
        pl.BlockSpec((1, 3, d), lambda i, j: (_cond_row(ts + i, tm, n_prompt_rows, lat_rows), 0, 0)),
        pl.BlockSpec((1, d), lambda i, j: (0, 0)),
        pl.BlockSpec((None, None, d, tf), lambda i, j: (l, k, 0, j)),
        pl.BlockSpec((None, None, d, tf), lambda i, j: (l, k, 0, nf + j)),
        pl.BlockSpec((None, None, tf, d), lambda i, j: (l, k, j, 0))]
    args = [x, mod3, g.reshape(1, d), w_in, w_in, w_out]
    if final:
        in_specs.append(pl.BlockSpec((1, d), lambda i, j: (0, 0)))
        args.append(final_g.reshape(1, d))
    return pl.pallas_call(
        functools.partial(_ffn_kernel, row_split=max(1, tm // 512), n_chunk=n_chunk, final=final),
        grid=(m // tm, nf),
        in_specs=in_specs,
        out_specs=pl.BlockSpec((tm, d), lambda i, j: (i, 0)),
        out_shape=jax.ShapeDtypeStruct((m, d), F32),
        scratch_shapes=[pltpu.VMEM((tm, d), BF16)],
        compiler_params=_cparams("arbitrary", "arbitrary"),
        name="ffn_final" if final else "ffn",
    )(*args)


def _log_sigmoid(z):
    return jnp.minimum(z, 0.0) - jnp.log(1.0 + jnp.exp(-jnp.abs(z)))


def _proj_kernel(*refs, n_x, n_prompt_tiles, n_rope_cols, rope_shift, tn, with_gates):
    x_refs, refs = refs[:n_x], refs[n_x:]
    mod_ref, g_ref, w_ref, cos_ref, sa_ref, sb_ref = refs[:6]
    if with_gates:
        wl_ref, wa_ref, ba_ref, o_ref, gates_ref = refs[6:]
    else:
        (o_ref,) = refs[6:]

    def project(x_ref, rotary):
        x = x_ref[...]
        h = (((_rms(x) * g_ref[...]) * (1.0 + mod_ref[0, 1:2, :])) + mod_ref[0, 0:1, :]).astype(BF16)
        if with_gates:
            low = jnp.dot(h, wl_ref[...], preferred_element_type=F32)
            z = jnp.dot(low.astype(BF16), wa_ref[...].astype(BF16), preferred_element_type=F32) + ba_ref[...]
            gates_ref[...] = _log_sigmoid(z) * (1.0 / B_TAU)
        for c in range(o_ref.shape[1] // tn):
            acc = jnp.dot(h, w_ref[:, c * tn:(c + 1) * tn], preferred_element_type=F32)
            if rotary and c * tn < n_rope_cols:
                cos, sa, sb = cos_ref[...], sa_ref[...], sb_ref[...]
                for s in range(tn // 128):
                    a = acc[:, s * 128:(s + 1) * 128]
                    o_ref[:, c * tn + s * 128:c * tn + (s + 1) * 128] = (
                        a * cos + pltpu.roll(a, 128 - rope_shift, 1) * sa + pltpu.roll(a, rope_shift, 1) * sb)
            else:
                o_ref[:, c * tn:(c + 1) * tn] = acc

    i = pl.program_id(0)
    pl.when(i < n_prompt_tiles)(lambda: project(x_refs[0], False))
    pl.when(i >= n_prompt_tiles)(lambda: project(x_refs[-1], True))


def _rope_tables(n, head_dim, n_sub):
    assert head_dim * n_sub == 128
    rows = n // GRID_W
    row = jnp.repeat(jnp.arange(rows, dtype=F32), GRID_W)
    col = jnp.tile(jnp.arange(GRID_W, dtype=F32), rows)
    d_axis = head_dim // 2
    shift = d_axis // 2
    inv = ROPE_BASE ** (-jnp.arange(0, d_axis, 2, dtype=F32) / d_axis)
    lane = jnp.arange(128)
    sub = lane % head_dim
    is_col = (sub // d_axis) == 1
    within = sub % d_axis
    freq = within % shift
    second = (within // shift) == 1
    pos = jnp.where(is_col[None, :], col[:, None], row[:, None])
    ang = pos * inv[freq][None, :]
    cos, sin = jnp.cos(ang), jnp.sin(ang)
    sin_a = jnp.where(second[None, :], 0.0, -sin)
    sin_b = jnp.where(second[None, :], sin, 0.0)
    return cos, sin_a, sin_b, shift


def _stream_specs(xs, tm, npt):
    d = xs[0].shape[1]
    if len(xs) == 1:
        return [pl.BlockSpec((tm, d), lambda i: (i, 0))]
    return [pl.BlockSpec((tm, d), lambda i: (jnp.minimum(i, npt - 1), 0)),
            pl.BlockSpec((tm, d), lambda i: (jnp.maximum(i - npt, 0), 0))]


def mixer_in_proj(xs, mod2, g, w, e, n_cols, rows_info, rope, gates=None):
    m, d = sum(x.shape[0] for x in xs), xs[0].shape[1]
    n_prompt_rows, lat_rows = rows_info
    cos, sin_a, sin_b, rope_shift, n_rope_cols = rope
    tm = _tile(math.gcd(n_prompt_rows, lat_rows), 256, align=ROW_CHUNK)
    tn = _tile(math.gcd(n_cols, n_rope_cols), 512)
    npt, lat_tiles = n_prompt_rows // tm, lat_rows // tm
    tab = lambda i: (jnp.maximum(i - npt, 0) % lat_tiles, 0)
    once = dict(pipeline_mode=pl.Buffered(1))
    in_specs = _stream_specs(xs, tm, npt) + [
        pl.BlockSpec((1, 2, d), lambda i: (_cond_row(i, tm, n_prompt_rows, lat_rows), 0, 0)),
        pl.BlockSpec((1, d), lambda i: (0, 0)),
        pl.BlockSpec((None, d, n_cols), lambda i: (e, 0, 0), **once),
        pl.BlockSpec((tm, 128), tab), pl.BlockSpec((tm, 128), tab), pl.BlockSpec((tm, 128), tab)]
    args = list(xs) + [mod2, g.reshape(1, d), w, cos, sin_a, sin_b]
    out_specs = pl.BlockSpec((tm, n_cols), lambda i: (i, 0))
    out_shape = jax.ShapeDtypeStruct((m, n_cols), F32)
    if gates is not None:
        w_low, w_alpha, b_alpha = gates
        n_gate = w_alpha.shape[1]
        in_specs += [pl.BlockSpec(w_low.shape, lambda i: (0, 0)),
                     pl.BlockSpec(w_alpha.shape, lambda i: (0, 0)),
                     pl.BlockSpec((1, n_gate), lambda i: (0, 0))]
        args += [w_low, w_alpha, b_alpha.reshape(1, n_gate)]
        out_specs = [out_specs, pl.BlockSpec((tm, n_gate), lambda i: (i, 0))]
        out_shape = [out_shape, jax.ShapeDtypeStruct((m, n_gate), F32)]
    return pl.pallas_call(
        functools.partial(_proj_kernel, n_x=len(xs), n_prompt_tiles=npt, n_rope_cols=n_rope_cols,
                          rope_shift=rope_shift, tn=tn, with_gates=gates is not None),
        grid=(m // tm,),
        in_specs=in_specs, out_specs=out_specs, out_shape=out_shape,
        compiler_params=_cparams("arbitrary"),
        name="mixer_in_proj_gated" if gates is not None else "mixer_in_proj",
    )(*args)


def _out_proj_kernel(*refs, n_parts, n_x, n_prompt_tiles, tn):
    parts_p, parts_s = refs[:n_parts], refs[n_parts:2 * n_parts]
    w_refs = refs[2 * n_parts:3 * n_parts]
    x_refs = refs[3 * n_parts:3 * n_parts + n_x]
    gate_ref, o_ref = refs[3 * n_parts + n_x:]
    i = pl.program_id(0)

    def run(parts, x_ref):
        for c in range(o_ref.shape[1] // tn):
            cols = slice(c * tn, (c + 1) * tn)
            acc = None
            for p_ref, w_ref in zip(parts, w_refs):
                t = jnp.dot(p_ref[...], w_ref[:, cols], preferred_element_type=F32)
                acc = t if acc is None else acc + t
            o_ref[:, cols] = x_ref[:, cols] + gate_ref[0, :, cols] * acc

    pl.when(i < n_prompt_tiles)(lambda: run(parts_p, x_refs[0]))
    pl.when(i >= n_prompt_tiles)(lambda: run(parts_s, x_refs[-1]))


def mixer_out_proj(parts_p, parts_s, w, e, xs, gate, rows_info):
    m, d = sum(x.shape[0] for x in xs), xs[0].shape[1]
    n_prompt_rows, lat_rows = rows_info
    tm = _tile(math.gcd(n_prompt_rows, lat_rows), 512, align=ROW_CHUNK)
    npt = n_prompt_rows // tm
    n_parts = len(parts_p)
    in_specs, w_specs, off = [], [], 0
    for p in parts_p:
        in_specs.append(pl.BlockSpec((tm, p.shape[1]), lambda i: (jnp.minimum(i, npt - 1), 0)))
    for p in parts_s:
        kp = p.shape[1]
        in_specs.append(pl.BlockSpec((tm, kp), lambda i: (jnp.maximum(i - npt, 0), 0)))
        assert off % kp == 0
        w_specs.append(pl.BlockSpec((None, kp, d), functools.partial(lambda i, rb: (e, rb, 0), rb=off // kp),
                                    pipeline_mode=pl.Buffered(1)))
        off += kp
    in_specs += w_specs
    in_specs += _stream_specs(xs, tm, npt)
    in_specs += [pl.BlockSpec((1, 1, d), lambda i: (_cond_row(i, tm, n_prompt_rows, lat_rows), 0, 0))]
    return pl.pallas_call(
        functools.partial(_out_proj_kernel, n_parts=n_parts, n_x=len(xs), n_prompt_tiles=npt, tn=_tile(d, 512)),
        grid=(m // tm,),
        in_specs=in_specs,
        out_specs=pl.BlockSpec((tm, d), lambda i: (i, 0)),
        out_shape=jax.ShapeDtypeStruct((m, d), F32),
        compiler_params=_cparams("arbitrary"),
        name="mixer_out_proj",
    )(*parts_p, *parts_s, *([w] * n_parts), *xs, gate)


def _softmax_parts(parts, extra_logit=None, weight=None, axis=-1):
    m = functools.reduce(jnp.maximum, [jnp.max(p, axis=axis, keepdims=True) for p in parts])
    if extra_logit is not None:
        m = jnp.maximum(m, extra_logit)
    es = [jnp.exp(p - m) for p in parts]
    den = functools.reduce(jnp.add, [jnp.sum(e, axis=axis, keepdims=True) for e in es])
    if extra_logit is not None:
        den = den + jnp.exp(extra_logit - m)
    inv = 1.0 / den if weight is None else weight / den
    return [e * inv for e in es]


def _diff_attn_kernel(*refs, hb, lam_init, has_ctx, emit_kv):
    it = iter(refs)
    lam_ref, g_ref, q_ref, k_ref, v_ref = (next(it) for _ in range(5))
    ck_ref, cv_ref = (next(it), next(it)) if has_ctx else (None, None)
    o_ref = next(it)
    nk_ref, nv_ref = (next(it), next(it)) if emit_kv else (None, None)

    lp = lam_ref[...]
    lam = (jnp.exp(jnp.sum(lp[0:1] * lp[1:2], axis=-1, keepdims=True))
           - jnp.exp(jnp.sum(lp[2:3] * lp[3:4], axis=-1, keepdims=True)) + lam_init)
    scale = A_DH ** -0.5
    assert math.frexp(scale)[0] == 0.5
    first = lax.broadcasted_iota(jnp.int32, (1, 2 * A_DH), 1) < A_DH
    tq = q_ref.shape[0]
    heads = [slice(hh * 128, (hh + 1) * 128) for hh in range(hb)]
    q1, q2, keys, vals = [], [], [], []
    for hh, cols in enumerate(heads):
        q, k, v = q_ref[:, cols] * scale, k_ref[:, cols], v_ref[:, cols]
        q1.append(jnp.where(first, q, 0.0).astype(BF16))
        q2.append(jnp.where(first, 0.0, q).astype(BF16))
        keys.append(([ck_ref[0, 0, hh].astype(BF16)] if has_ctx else []) + [k.astype(BF16)])
        vals.append(([cv_ref[0, 0, hh].astype(BF16)] if has_ctx else []) + [v.astype(BF16)])
        if emit_kv:
            nk_ref[0, 0, hh] = k
            nv_ref[0, 0, hh] = v
    parts = range(len(keys[0]))
    stack = lambda qs: [jnp.concatenate([_dot_nt(qs[hh], keys[hh][p]) for hh in range(hb)], axis=0) for p in parts]
    p1 = _softmax_parts(stack(q1))
    p2 = _softmax_parts(stack(q2), weight=lam)
    pd = [(a - b).astype(BF16) for a, b in zip(p1, p2)]
    outs = []
    for hh in range(hb):
        o = None
        for p in parts:
            t = jnp.dot(pd[p][hh * tq:(hh + 1) * tq], vals[hh][p], preferred_element_type=F32)
            o = t if o is None else o + t
        outs.append(o)
    y = ((_rms(jnp.concatenate(outs, axis=0)) * g_ref[...]) * (1.0 - lam_init)).astype(BF16)
    for hh, cols in enumerate(heads):
        o_ref[:, cols] = y[hh * tq:(hh + 1) * tq]


def diff_attention(proj, lam_p, subln_g, lam_init, *, row_off, batch, seq, tq, hb, ctx=None, emit_kv=False):
    width = hb * 128
    n_hg = A_HEADS // hb
    assert row_off % seq == 0 and seq % tq == 0
    qrow = lambda b, hg, i: ((row_off + b * seq) // tq + i, hg)
    krow = lambda b, hg, i: ((row_off + b * seq) // seq, A_QK_W // width + hg)
    vrow = lambda b, hg, i: ((row_off + b * seq) // seq, 2 * A_QK_W // width + hg)
    in_specs = [pl.BlockSpec(lam_p.shape, lambda b, hg, i: (0, 0)),
                pl.BlockSpec((1, A_DV), lambda b, hg, i: (0, 0)),
                pl.BlockSpec((tq, width), qrow),
                pl.BlockSpec((seq, width), krow),
                pl.BlockSpec((seq, width), vrow)]
    args = [lam_p, subln_g.reshape(1, A_DV), proj, proj, proj]
    if ctx is not None:
        ctx_k, ctx_v, e = ctx
        past = ctx_k.shape[3]
        cspec = lambda b, hg, i: (b, e, hg, 0, 0)
        in_specs += [pl.BlockSpec((1, 1, hb, past, 2 * A_DH), cspec), pl.BlockSpec((1, 1, hb, past, A_DV), cspec)]
        args += [ctx_k, ctx_v]
    out_specs = [pl.BlockSpec((tq, width), lambda b, hg, i: (b * (seq // tq) + i, hg))]
    out_shape = [jax.ShapeDtypeStruct((batch * seq, A_V_W), BF16)]
    if emit_kv:
        assert tq == seq
        kvspec = pl.BlockSpec((1, 1, hb, seq, 128), lambda b, hg, i: (b, 0, hg, 0, 0))
        out_specs += [kvspec, kvspec]
        out_shape += [jax.ShapeDtypeStruct((batch, 1, A_HEADS, seq, 128), F32)] * 2
    return pl.pallas_call(
        functools.partial(_diff_attn_kernel, hb=hb, lam_init=lam_init, has_ctx=ctx is not None, emit_kv=emit_kv),
        grid=(batch, n_hg, seq // tq),
        in_specs=in_specs, out_specs=out_specs, out_shape=out_shape,
        compiler_params=_cparams("arbitrary", "arbitrary", "arbitrary"),
        name="diff_attn_ctx" if ctx is not None else "diff_attn",
    )(*args)


def _gla_level_matrices(chunk, fwd):
    t = np.arange(chunk)[:, None]
    i = np.arange(chunk)[None, :]
    mats = [i <= t, i > t] if fwd else [i >= t, i < t]
    h = chunk // 2
    while h >= 1:
        p = t % (2 * h)
        base = t - p
        if fwd:
            m = base + h - 1
            a = np.where(p >= h, (i > m) & (i <= t), (i > t) & (i <= m))
        else:
            m = base + h
            a = np.where(p < h, (i >= t) & (i < m), (i >= m) & (i < t))
        mats.append(a)
        h //= 2
    return jnp.asarray(np.concatenate(mats, axis=0).astype(np.float32), dtype=BF16)


def _gla_kernel(*refs, chunk, has_state, emit_state):
    it = iter(refs)
    af_ref, ab_ref, q_ref, k_ref, v_ref, laf_ref, lab_ref, r_ref, g_ref = (next(it) for _ in range(9))
    sf_ref, sb_ref = (next(it), next(it)) if has_state else (None, None)
    o_ref = next(it)
    nsf_ref, nsb_ref = (next(it), next(it)) if emit_state else (None, None)
    accf_ref, accb_ref, stf_ref, stb_ref = (next(it) for _ in range(4))

    seq = q_ref.shape[0]
    n_chunks = seq // chunk
    n_lev = chunk.bit_length() - 1
    ti = lax.broadcasted_iota(jnp.int32, (chunk, chunk), 0)
    si = lax.broadcasted_iota(jnp.int32, (chunk, chunk), 1)
    split = ti ^ si
    scale = B_DK ** -0.5

    def chunk_step(c, fwd, a_ref, la_ref, acc_ref, st_ref):
        rows = c * chunk if isinstance(c, int) else pl.multiple_of(c * chunk, chunk)
        rows = pl.ds(rows, chunk)
        order = (ti > si) if fwd else (ti < si)
        q = q_ref[rows, :] * scale
        k = k_ref[rows, :]
        vb = v_ref[rows, :].astype(BF16)
        la = la_ref[rows, :]
        la_hi = la.astype(BF16)
        la_lo = (la - la_hi.astype(F32)).astype(BF16)
        r2 = jnp.dot(a_ref[...], jnp.concatenate([la_hi, la_lo], axis=1), preferred_element_type=F32)
        e = jnp.exp(r2[:, :B_DK] + r2[:, B_DK:])
        st = st_ref[...]
        inter = _dot_nt((q * e[0:chunk]).astype(BF16), st.astype(BF16))
        ku = (k * e[chunk:2 * chunk]).astype(BF16)
        scores = jnp.where(ti == si, _dot_nt(q.astype(BF16), k.astype(BF16)), 0.0)
        for lev in range(n_lev):
            half = chunk >> (lev + 1)
            f = e[(2 + lev) * chunk:(3 + lev) * chunk]
            sc = _dot_nt((q * f).astype(BF16), (k * f).astype(BF16))
            scores = jnp.where(order & (split >= half) & (split < 2 * half), sc, scores)
        acc_ref[rows, :] = inter + jnp.dot(scores.astype(BF16), vb, preferred_element_type=F32)
        total = e[chunk - 1:chunk] if fwd else e[0:1]
        st_ref[...] = st * total + _dot_tn(vb, ku)

    for s0_ref, st_ref in ((sf_ref, stf_ref), (sb_ref, stb_ref)):
        st_ref[...] = s0_ref[0, 0, 0].T if has_state else jnp.zeros(st_ref.shape, F32)

    def body(cc, carry):
        chunk_step(cc, True, af_ref, laf_ref, accf_ref, stf_ref)
        chunk_step(n_chunks - 1 - cc, False, ab_ref, lab_ref, accb_ref, stb_ref)
        return carry

    if n_chunks == 1:
        body(0, 0)
    else:
        lax.fori_loop(0, n_chunks, body, 0)
    if emit_state:
        nsf_ref[0, 0, 0] = stf_ref[...].T
        nsb_ref[0, 0, 0] = stb_ref[...].T

    def epilogue(r, carry):
        rows = pl.ds(pl.multiple_of(r * ROW_CHUNK, ROW_CHUNK), ROW_CHUNK)
        gate = r_ref[rows, :]
        o = accf_ref[rows, :] + accb_ref[rows, :]
        o_ref[rows, :] = ((_rms(o) * g_ref[...]) * (gate * jax.nn.sigmoid(gate))).astype(BF16)
        return carry

    lax.fori_loop(0, seq // ROW_CHUNK, epilogue, 0)


def gla_bidirectional(proj, gates, bnorm_g, *, row_off, batch, seq, states=None, emit_state=False):
    chunk = min(256, seq)
    assert row_off % seq == 0 and seq % chunk == 0 and chunk & (chunk - 1) == 0
    rb = lambda b: (row_off + b * seq) // seq
    q_off = (2 * A_QK_W + A_V_W) // B_DK
    k_off = q_off + B_HEADS
    v_off = (2 * A_QK_W + A_V_W + 2 * B_QK_W) // B_DV
    r_off = v_off + B_HEADS
    a_f, a_b = _gla_level_matrices(chunk, True), _gla_level_matrices(chunk, False)
    whole = lambda b, h: (0, 0)
    in_specs = [pl.BlockSpec(a_f.shape, whole), pl.BlockSpec(a_b.shape, whole),
                pl.BlockSpec((seq, B_DK), lambda b, h: (rb(b), q_off + h)),
                pl.BlockSpec((seq, B_DK), lambda b, h: (rb(b), k_off + h)),
                pl.BlockSpec((seq, B_DV), lambda b, h: (rb(b), v_off + h)),
                pl.BlockSpec((seq, B_DK), lambda b, h: (rb(b), h)),
                pl.BlockSpec((seq, B_DK), lambda b, h: (rb(b), B_HEADS + h)),
                pl.BlockSpec((seq, B_DV), lambda b, h: (rb(b), r_off + h)),
                pl.BlockSpec((1, B_DV), whole)]
    args = [a_f, a_b, proj, proj, proj, gates, gates, proj, bnorm_g.reshape(1, B_DV)]
    if states is not None:
        s_f, s_b, e = states
        sspec = pl.BlockSpec((1, 1, 1, B_DK, B_DV), lambda b, h: (b, e, h, 0, 0))
        in_specs += [sspec, sspec]
        args += [s_f, s_b]
    out_specs = [pl.BlockSpec((seq, B_DV), lambda b, h: (b, h))]
    out_shape = [jax.ShapeDtypeStruct((batch * seq, B_V_W), BF16)]
    if emit_state:
        nspec = pl.BlockSpec((1, 1, 1, B_DK, B_DV), lambda b, h: (b, 0, h, 0, 0))
        out_specs += [nspec, nspec]
        out_shape += [jax.ShapeDtypeStruct((batch, 1, B_HEADS, B_DK, B_DV), F32)] * 2
    return pl.pallas_call(
        functools.partial(_gla_kernel, chunk=chunk, has_state=states is not None, emit_state=emit_state),
        grid=(batch, B_HEADS),
        in_specs=in_specs, out_specs=out_specs, out_shape=out_shape,
        scratch_shapes=[pltpu.VMEM((seq, B_DV), F32)] * 2 + [pltpu.VMEM((B_DV, B_DK), F32)] * 2,
        compiler_params=_cparams("arbitrary", "arbitrary"),
        name="gla_state" if states is not None else "gla",
    )(*args)


def _sink_attn_kernel(sink_ref, q_ref, k_ref, v_ref, o_ref, nk_ref, nv_ref):
    seq = q_ref.shape[0]
    scale = C_DH ** -0.5
    scores, sinks, vals = [], [], []
    for hk in range(C_KV_HEADS):
        kcols = slice(hk * C_DH, (hk + 1) * C_DH)
        k, v = k_ref[:, kcols], v_ref[:, kcols]
        nk_ref[0, 0, hk] = k
        nv_ref[0, 0, hk] = v
        vals.append(v.T.astype(BF16))
        q4 = jnp.concatenate([q_ref[:, (hk * C_GROUP + g) * C_DH:(hk * C_GROUP + g + 1) * C_DH]
                              for g in range(C_GROUP)], axis=0).astype(BF16)
        scores.append(_dot_nt(k.astype(BF16), q4) * scale)
        sinks += [jnp.broadcast_to(sink_ref[hk, 0:1, g:g + 1], (1, seq)) for g in range(C_GROUP)]
    (p,) = _softmax_parts([jnp.concatenate(scores, axis=1)], extra_logit=jnp.concatenate(sinks, axis=1), axis=0)
    p = p.astype(BF16)
    cols_per_kv = C_GROUP * seq
    for hk in range(C_KV_HEADS):
        o = jnp.dot(vals[hk], p[:, hk * cols_per_kv:(hk + 1) * cols_per_kv], preferred_element_type=F32).T
        for g in range(C_GROUP):
            h = hk * C_GROUP + g
            o_ref[:, h * C_DH:(h + 1) * C_DH] = o[g * seq:(g + 1) * seq].astype(BF16)


def sink_attention_context(proj, sink, *, batch, seq):
    sink3 = sink.reshape(C_KV_HEADS, 1, C_GROUP)
    kvspec = pl.BlockSpec((1, 1, C_KV_HEADS, seq, C_DH), lambda b: (b, 0, 0, 0, 0))
    return pl.pallas_call(
        _sink_attn_kernel,
        grid=(batch,),
        in_specs=[pl.BlockSpec(sink3.shape, lambda b: (0, 0, 0)),
                  pl.BlockSpec((seq, C_Q_W), lambda b: (b, 0)),
                  pl.BlockSpec((seq, C_KV_W), lambda b: (b, C_Q_W // C_KV_W)),
                  pl.BlockSpec((seq, C_KV_W), lambda b: (b, C_Q_W // C_KV_W + 1))],
        out_specs=[pl.BlockSpec((seq, C_Q_W), lambda b: (b, 0)), kvspec, kvspec],
        out_shape=[jax.ShapeDtypeStruct((batch * seq, C_Q_W), BF16)]
        + [jax.ShapeDtypeStruct((batch, 1, C_KV_HEADS, seq, C_DH), F32)] * 2,
        compiler_params=_cparams("arbitrary"),
        name="sink_attn",
    )(sink3, proj, proj, proj)


def _window_attn_kernel(sink_ref, q_ref, k_ref, v_ref, ck_ref, cv_ref, o_ref, *, band, hkb):
    tq = q_ref.shape[0]
    seq = k_ref.shape[0]
    scale = C_DH ** -0.5
    i = pl.program_id(2)
    start = pl.multiple_of(jnp.clip(i * tq - C_WINDOW, 0, seq - band), C_WINDOW)
    kpos = start + lax.broadcasted_iota(jnp.int32, (band, tq), 0)
    qpos = i * tq + lax.broadcasted_iota(jnp.int32, (band, tq), 1)
    valid = jnp.abs(qpos - kpos) <= C_WINDOW
    valid4 = jnp.concatenate([valid] * C_GROUP, axis=1)
    gw = C_GROUP * C_DH
    s_ctx, s_band, sinks, vbs, cvs = [], [], [], [], []
    for hk in range(hkb):
        kcols = slice(hk * C_DH, (hk + 1) * C_DH)
        kb = k_ref[pl.ds(start, band), kcols].astype(BF16)
        vbs.append(v_ref[pl.ds(start, band), kcols].T.astype(BF16))
        cvs.append(cv_ref[0, 0, hk].T.astype(BF16))
        q4 = jnp.concatenate([q_ref[:, hk * gw + g * C_DH:hk * gw + (g + 1) * C_DH] for g in range(C_GROUP)],
                             axis=0).astype(BF16)
        sinks += [jnp.broadcast_to(sink_ref[hk, 0:1, g:g + 1], (1, tq)) for g in range(C_GROUP)]
        s_band.append(jnp.where(valid4, _dot_nt(kb, q4) * scale, -jnp.inf))
        s_ctx.append(_dot_nt(ck_ref[0, 0, hk].astype(BF16), q4) * scale)
    p_ctx, p_band = _softmax_parts([jnp.concatenate(s_ctx, axis=1), jnp.concatenate(s_band, axis=1)],
                                   extra_logit=jnp.concatenate(sinks, axis=1), axis=0)
    p_ctx, p_band = p_ctx.astype(BF16), p_band.astype(BF16)
    cols_per_kv = C_GROUP * tq
    for hk in range(hkb):
        cols = slice(hk * cols_per_kv, (hk + 1) * cols_per_kv)
        o = (jnp.dot(cvs[hk], p_ctx[:, cols], preferred_element_type=F32)
             + jnp.dot(vbs[hk], p_band[:, cols], preferred_element_type=F32)).T
        for g in range(C_GROUP):
            o_ref[:, hk * gw + g * C_DH:hk * gw + (g + 1) * C_DH] = o[g * tq:(g + 1) * tq].astype(BF16)


def window_attention_latent(proj, sink, ctx_k, ctx_v, e, *, row_off, batch, seq, tq=128, hkb=4):
    tq = min(tq, seq)
    band = min(tq + 2 * C_WINDOW, seq)
    assert row_off % seq == 0 and seq % tq == 0 and tq % C_WINDOW == 0 and C_KV_HEADS % hkb == 0
    past = ctx_k.shape[3]
    qw, kw = hkb * C_GROUP * C_DH, hkb * C_DH
    rb = lambda b: (row_off + b * seq) // seq
    cspec = pl.BlockSpec((1, 1, hkb, past, C_DH), lambda b, hg, i: (b, e, hg, 0, 0))
    sink3 = sink.reshape(C_KV_HEADS, 1, C_GROUP)
    return pl.pallas_call(
        functools.partial(_window_attn_kernel, band=band, hkb=hkb),
        grid=(batch, C_KV_HEADS // hkb, seq // tq),
        in_specs=[pl.BlockSpec((hkb, 1, C_GROUP), lambda b, hg, i: (hg, 0, 0)),
                  pl.BlockSpec((tq, qw), lambda b, hg, i: ((row_off + b * seq) // tq + i, hg)),
                  pl.BlockSpec((seq, kw), lambda b, hg, i: (rb(b), C_Q_W // kw + hg)),
                  pl.BlockSpec((seq, kw), lambda b, hg, i: (rb(b), (C_Q_W + C_KV_W) // kw + hg)),
                  cspec, cspec],
        out_specs=pl.BlockSpec((tq, qw), lambda b, hg, i: (b * (seq // tq) + i, hg)),
        out_shape=jax.ShapeDtypeStruct((batch * seq, C_Q_W), BF16),
        compiler_params=_cparams("arbitrary", "arbitrary", "arbitrary"),
        name="window_attn",
    )(sink3, proj, proj, proj, ctx_k, ctx_v)


def kernel(x_prompt, x_sample, cache_a_k, cache_a_v, state_b_fwd, state_b_bwd, cache_c_k, cache_c_v, c, c_ctx, ada_w, ada_b, norm_g, ffn_w_in, ffn_w_out, ab_w_in, ab_w_out, a_lambda, a_subln_g, b_alpha_w, b_alpha_b, b_norm_g, c_w_in, c_w_out, c_sink, final_g):
    bp, sp, d = x_prompt.shape
    bs, ss, _ = x_sample.shape
    depth = ada_w.shape[0]
    mp, ms = bp * sp, bs * ss
    rows_info = (mp, ss)

    cond = jnp.concatenate([c_ctx[None, :], c, jnp.zeros((COND_ROWS - 1 - bs, d), F32)], axis=0)
    mods = adaln(cond, ada_w, ada_b).reshape(depth, COND_ROWS, N_MOD, d)
    rope_a = _rope_tables(ss, A_DH, 2) + (2 * A_QK_W,)
    rope_c = _rope_tables(ss, C_DH, 1) + (C_Q_W + C_KV_W,)
    ab_w_in, ab_w_out, c_w_in, c_w_out = (w.astype(BF16) for w in (ab_w_in, ab_w_out, c_w_in, c_w_out))

    a_k, a_v, b_f, b_b, c_k, c_v = [], [], [], [], [], []
    xs = [x_prompt.reshape(mp, d), x_sample.reshape(ms, d)]
    for l in range(depth):
        mod = mods[l]
        ffn1 = functools.partial(ffn_half_step, mod3=mod[:, 0:3], g=norm_g[l, 0], w_in=ffn_w_in, w_out=ffn_w_out,
                                 widx=(l, 0), rows_info=rows_info)
        xs = [ffn1(xs[0])] if len(xs) == 1 else [ffn1(xs[0]), ffn1(xs[1], stream_row0=mp)]
        if l % 2 == 0:
            e = l // 2
            lam_init = 0.8 - 0.6 * math.exp(-0.3 * l)
            zeros = jnp.zeros((B_RANK, B_QK_W), F32)
            w_alpha = jnp.concatenate([jnp.concatenate([b_alpha_w[e, 0], zeros], axis=1),
                                       jnp.concatenate([zeros, b_alpha_w[e, 1]], axis=1)], axis=0)
            gate_w = (ab_w_in[e][:, AB_MAIN:], w_alpha, b_alpha_b[e].reshape(-1))
            proj, gates = mixer_in_proj(xs, mod[:, 3:5], norm_g[l, 1], ab_w_in, e, AB_MAIN, rows_info, rope_a,
                                        gates=gate_w)
            attn_p, ak, av = diff_attention(proj, a_lambda[e], a_subln_g[e], lam_init, row_off=0, batch=bp,
                                            seq=sp, tq=sp, hb=A_HEADS, emit_kv=True)
            (attn_s,) = diff_attention(proj, a_lambda[e], a_subln_g[e], lam_init, row_off=mp, batch=bs,
                                       seq=ss, tq=min(256, ss), hb=2, ctx=(cache_a_k, cache_a_v, e))
            gla_p, sf, sb = gla_bidirectional(proj, gates, b_norm_g[e], row_off=0, batch=bp, seq=sp,
                                              emit_state=True)
            (gla_s,) = gla_bidirectional(proj, gates, b_norm_g[e], row_off=mp, batch=bs, seq=ss,
                                         states=(state_b_fwd, state_b_bwd, e))
            a_k.append(ak), a_v.append(av), b_f.append(sf), b_b.append(sb)
            x = mixer_out_proj([attn_p, gla_p], [attn_s, gla_s], ab_w_out, e, xs, mod[:, 5:6], rows_info)
        else:
            o = l // 2
            proj = mixer_in_proj(xs, mod[:, 3:5], norm_g[l, 1], c_w_in, o, c_w_in.shape[2], rows_info, rope_c)
            mix_p, ck, cv = sink_attention_context(proj, c_sink[o], batch=bp, seq=sp)
            mix_s = window_attention_latent(proj, c_sink[o], cache_c_k, cache_c_v, o, row_off=mp, batch=bs, seq=ss)
            c_k.append(ck), c_v.append(cv)
            x = mixer_out_proj([mix_p], [mix_s], c_w_out, o, xs, mod[:, 5:6], rows_info)
        ffn2 = functools.partial(ffn_half_step, x, mod[:, 6:9], norm_g[l, 2], ffn_w_in, ffn_w_out, (l, 1), rows_info)
        if l < depth - 1:
            xs = [ffn2()]

    y_prompt = ffn2(n_rows=mp, final_g=final_g).reshape(bp, sp, d)
    y_sample = ffn2(stream_row0=mp, x_row0=mp, n_rows=ms, final_g=final_g).reshape(bs, ss, d)
    cat = lambda parts: parts[0] if len(parts) == 1 else jnp.concatenate(parts, axis=1)
    return (y_prompt, y_sample, cat(a_k), cat(a_v), cat(b_f), cat(b_b), cat(c_k), cat(c_v))
```

```python
import functools
import math

import numpy as np
import jax
import jax.numpy as jnp
from jax import lax
from jax.experimental import pallas as pl
from jax.experimental.pallas import tpu as pltpu

F32 = jnp.float32
BF16 = jnp.bfloat16

EPS = 1e-6
GRID_W = 64
ROPE_BASE = 10000.0
N_MOD = 9
A_HEADS, A_DH, A_DV = 8, 64, 128
B_HEADS, B_DK, B_DV, B_RANK, B_TAU = 4, 128, 256, 16, 16.0
C_HEADS, C_KV_HEADS, C_DH, C_WINDOW = 16, 4, 128, 128
A_QK_W = A_HEADS * 2 * A_DH
A_V_W = A_HEADS * A_DV
B_QK_W = B_HEADS * B_DK
B_V_W = B_HEADS * B_DV
AB_MAIN = 2 * A_QK_W + A_V_W + 2 * B_QK_W + 2 * B_V_W
C_GROUP = C_HEADS // C_KV_HEADS
C_Q_W = C_HEADS * C_DH
C_KV_W = C_KV_HEADS * C_DH

V7X_VMEM_BYTES = 64 * 1024 * 1024
VMEM_LIMIT_BYTES = V7X_VMEM_BYTES - 8 * 1024 * 1024
COND_ROWS = 8
ROW_CHUNK = 128


def _tile(n, pref, align=128):
    if n <= pref:
        return n
    t = (pref // align) * align
    while n % t:
        t -= align
    assert t > 0, (n, pref)
    return t


def _cparams(*sem):
    return pltpu.CompilerParams(dimension_semantics=sem, vmem_limit_bytes=VMEM_LIMIT_BYTES)


def _bdot(a, b):
    return jnp.dot(a.astype(BF16), b.astype(BF16), preferred_element_type=F32)


def _dot_nt(a, b):
    return lax.dot_general(a, b, (((1,), (1,)), ((), ())), preferred_element_type=F32)


def _dot_tn(a, b):
    return lax.dot_general(a, b, (((0,), (0,)), ((), ())), preferred_element_type=F32)


def _rms(x):
    return x * lax.rsqrt(jnp.mean(x * x, axis=-1, keepdims=True) + EPS)


def _adaln_kernel(cond_ref, w_ref, b_ref, o_ref):
    c = cond_ref[...]
    o_ref[0] = _bdot(c * jax.nn.sigmoid(c), w_ref[0]) + b_ref[0]


def adaln(cond, ada_w, ada_b):
    depth, d, n = ada_w.shape
    tn = _tile(n, 1024)
    return pl.pallas_call(
        _adaln_kernel,
        grid=(depth, n // tn),
        in_specs=[pl.BlockSpec((COND_ROWS, d), lambda l, j: (0, 0)),
                  pl.BlockSpec((1, d, tn), lambda l, j: (l, 0, j)),
                  pl.BlockSpec((1, 1, tn), lambda l, j: (l, 0, j))],
        out_specs=pl.BlockSpec((1, COND_ROWS, tn), lambda l, j: (l, 0, j)),
        out_shape=jax.ShapeDtypeStruct((depth, COND_ROWS, n), F32),
        compiler_params=_cparams("arbitrary", "arbitrary"),
        name="adaln",
    )(cond, ada_w, ada_b.reshape(depth, 1, n))


def _norm_modulate(x_ref, g_ref, mod_ref, h_ref):
    g = g_ref[...]
    shift = mod_ref[0, 0:1, :]
    scale1 = 1.0 + mod_ref[0, 1:2, :]

    def body(r, carry):
        rows = pl.ds(pl.multiple_of(r * ROW_CHUNK, ROW_CHUNK), ROW_CHUNK)
        h_ref[rows, :] = ((_rms(x_ref[rows, :]) * g) * scale1 + shift).astype(BF16)
        return carry

    lax.fori_loop(0, x_ref.shape[0] // ROW_CHUNK, body, 0)


def _row_tile(rows_info):
    n_prompt_rows, lat_rows = rows_info
    return _tile(math.gcd(n_prompt_rows, lat_rows), 1024, align=ROW_CHUNK)


def _cond_row(i, tm, n_prompt_rows, lat_rows):
    return jnp.maximum((i * tm - n_prompt_rows) // lat_rows + 1, 0)


def _ffn_kernel(x_ref, mod_ref, g_ref, wg_ref, wu_ref, wo_ref, *rest, row_split, n_chunk, final):
    if final:
        fg_ref, o_ref, h_ref = rest
    else:
        o_ref, h_ref = rest
    j = pl.program_id(1)

    @pl.when(j == 0)
    def _():
        _norm_modulate(x_ref, g_ref, mod_ref, h_ref)
        o_ref[...] = jnp.zeros(o_ref.shape, F32)

    tm, d = h_ref.shape
    rt = tm // row_split
    wg, wu, wo = (w_ref[...].astype(BF16) for w_ref in (wg_ref, wu_ref, wo_ref))
    for r in range(row_split):
        rows = slice(r * rt, (r + 1) * rt)
        h = h_ref[rows, :]
        gate = jnp.dot(h, wg, preferred_element_type=F32)
        up = jnp.dot(h, wu, preferred_element_type=F32)
        act = (gate * jax.nn.sigmoid(gate) * up).astype(BF16)
        for c in range(d // n_chunk):
            cols = slice(c * n_chunk, (c + 1) * n_chunk)
            o_ref[rows, cols] += jnp.dot(act, wo[:, cols], preferred_element_type=F32)

    @pl.when(j == pl.num_programs(1) - 1)
    def _():
        half_gate = 0.5 * mod_ref[0, 2:3, :]

        def body(r, carry):
            rows = pl.ds(pl.multiple_of(r * ROW_CHUNK, ROW_CHUNK), ROW_CHUNK)
            xn = x_ref[rows, :] + half_gate * o_ref[rows, :]
            if final:
                xn = _rms(xn) * fg_ref[...]
            o_ref[rows, :] = xn
            return carry

        lax.fori_loop(0, tm // ROW_CHUNK, body, 0)


def ffn_half_step(x, mod3, g, w_in, w_out, widx, rows_info, *, stream_row0=0, x_row0=0, n_rows=None,
                  final_g=None):
    d = x.shape[1]
    f = w_out.shape[-2]
    n_prompt_rows, lat_rows = rows_info
    tm, tf, n_chunk = _row_tile(rows_info), _tile(f, 256), _tile(d, 512)
    m = x.shape[0] if n_rows is None else n_rows
    assert x_row0 % tm == 0 and stream_row0 % tm == 0 and m % tm == 0
    tx, ts = x_row0 // tm, stream_row0 // tm
    nf = f // tf
    final = final_g is not None
    l, k = widx
    in_specs = [
        pl.BlockSpec((tm, d), lambda i, j: (tx + i, 0)),
        pl.BlockSpec((1, 3, d), lambda i, j: (_cond_row(ts + i, tm, n_prompt_rows, lat_rows), 0, 0)),
        pl.BlockSpec((1, d), lambda i, j: (0, 0)),
        pl.BlockSpec((None, None, d, tf), lambda i, j: (l, k, 0, j)),
        pl.BlockSpec((None, None, d, tf), lambda i, j: (l, k, 0, nf + j)),
        pl.BlockSpec((None, None, tf, d), lambda i, j: (l, k, j, 0))]
    args = [x, mod3, g.reshape(1, d), w_in, w_in, w_out]
    if final:
        in_specs.append(pl.BlockSpec((1, d), lambda i, j: (0, 0)))
        args.append(final_g.reshape(1, d))
    return pl.pallas_call(
        functools.partial(_ffn_kernel, row_split=max(1, tm // 512), n_chunk=n_chunk, final=final),
        grid=(m // tm, nf),
        in_specs=in_specs,
        out_specs=pl.BlockSpec((tm, d), lambda i, j: (i, 0)),
        out_shape=jax.ShapeDtypeStruct((m, d), F32),
        scratch_shapes=[pltpu.VMEM((tm, d), BF16)],
        compiler_params=_cparams("arbitrary", "arbitrary"),
        name="ffn_final" if final else "ffn",
    )(*args)


def _log_sigmoid(z):
    return jnp.minimum(z, 0.0) - jnp.log(1.0 + jnp.exp(-jnp.abs(z)))


def _proj_kernel(*refs, n_x, n_prompt_tiles, n_rope_cols, rope_shift, tn, with_gates):
    x_refs, refs = refs[:n_x], refs[n_x:]
    mod_ref, g_ref, w_ref, cos_ref, sa_ref, sb_ref = refs[:6]
    if with_gates:
        wl_ref, wa_ref, ba_ref, o_ref, gates_ref = refs[6:]
    else:
        (o_ref,) = refs[6:]

    def project(x_ref, rotary):
        x = x_ref[...]
        h = (((_rms(x) * g_ref[...]) * (1.0 + mod_ref[0, 1:2, :])) + mod_ref[0, 0:1, :]).astype(BF16)
        if with_gates:
            low = jnp.dot(h, wl_ref[...], preferred_element_type=F32)
            z = jnp.dot(low.astype(BF16), wa_ref[...].astype(BF16), preferred_element_type=F32) + ba_ref[...]
            gates_ref[...] = _log_sigmoid(z) * (1.0 / B_TAU)
        for c in range(o_ref.shape[1] // tn):
            acc = jnp.dot(h, w_ref[:, c * tn:(c + 1) * tn], preferred_element_type=F32)
            if rotary and c * tn < n_rope_cols:
                cos, sa, sb = cos_ref[...], sa_ref[...], sb_ref[...]
                for s in range(tn // 128):
                    a = acc[:, s * 128:(s + 1) * 128]
                    o_ref[:, c * tn + s * 128:c * tn + (s + 1) * 128] = (
                        a * cos + pltpu.roll(a, 128 - rope_shift, 1) * sa + pltpu.roll(a, rope_shift, 1) * sb)
            else:
                o_ref[:, c * tn:(c + 1) * tn] = acc

    i = pl.program_id(0)
    pl.when(i < n_prompt_tiles)(lambda: project(x_refs[0], False))
    pl.when(i >= n_prompt_tiles)(lambda: project(x_refs[-1], True))


def _rope_tables(n, head_dim, n_sub):
    assert head_dim * n_sub == 128
    rows = n // GRID_W
    row = jnp.repeat(jnp.arange(rows, dtype=F32), GRID_W)
    col = jnp.tile(jnp.arange(GRID_W, dtype=F32), rows)
    d_axis = head_dim // 2
    shift = d_axis // 2
    inv = ROPE_BASE ** (-jnp.arange(0, d_axis, 2, dtype=F32) / d_axis)
    lane = jnp.arange(128)
    sub = lane % head_dim
    is_col = (sub // d_axis) == 1
    within = sub % d_axis
    freq = within % shift
    second = (within // shift) == 1
    pos = jnp.where(is_col[None, :], col[:, None], row[:, None])
    ang = pos * inv[freq][None, :]
    cos, sin = jnp.cos(ang), jnp.sin(ang)
    sin_a = jnp.where(second[None, :], 0.0, -sin)
    sin_b = jnp.where(second[None, :], sin, 0.0)
    return cos, sin_a, sin_b, shift


def _stream_specs(xs, tm, npt):
    d = xs[0].shape[1]
    if len(xs) == 1:
        return [pl.BlockSpec((tm, d), lambda i: (i, 0))]
    return [pl.BlockSpec((tm, d), lambda i: (jnp.minimum(i, npt - 1), 0)),
            pl.BlockSpec((tm, d), lambda i: (jnp.maximum(i - npt, 0), 0))]


def mixer_in_proj(xs, mod2, g, w, e, n_cols, rows_info, rope, gates=None):
    m, d = sum(x.shape[0] for x in xs), xs[0].shape[1]
    n_prompt_rows, lat_rows = rows_info
    cos, sin_a, sin_b, rope_shift, n_rope_cols = rope
    tm = _tile(math.gcd(n_prompt_rows, lat_rows), 256, align=ROW_CHUNK)
    tn = _tile(math.gcd(n_cols, n_rope_cols), 512)
    npt, lat_tiles = n_prompt_rows // tm, lat_rows // tm
    tab = lambda i: (jnp.maximum(i - npt, 0) % lat_tiles, 0)
    once = dict(pipeline_mode=pl.Buffered(1))
    in_specs = _stream_specs(xs, tm, npt) + [
        pl.BlockSpec((1, 2, d), lambda i: (_cond_row(i, tm, n_prompt_rows, lat_rows), 0, 0)),
        pl.BlockSpec((1, d), lambda i: (0, 0)),
        pl.BlockSpec((None, d, n_cols), lambda i: (e, 0, 0), **once),
        pl.BlockSpec((tm, 128), tab), pl.BlockSpec((tm, 128), tab), pl.BlockSpec((tm, 128), tab)]
    args = list(xs) + [mod2, g.reshape(1, d), w, cos, sin_a, sin_b]
    out_specs = pl.BlockSpec((tm, n_cols), lambda i: (i, 0))
    out_shape = jax.ShapeDtypeStruct((m, n_cols), F32)
    if gates is not None:
        w_low, w_alpha, b_alpha = gates
        n_gate = w_alpha.shape[1]
        in_specs += [pl.BlockSpec(w_low.shape, lambda i: (0, 0)),
                     pl.BlockSpec(w_alpha.shape, lambda i: (0, 0)),
                     pl.BlockSpec((1, n_gate), lambda i: (0, 0))]
        args += [w_low, w_alpha, b_alpha.reshape(1, n_gate)]
        out_specs = [out_specs, pl.BlockSpec((tm, n_gate), lambda i: (i, 0))]
        out_shape = [out_shape, jax.ShapeDtypeStruct((m, n_gate), F32)]
    return pl.pallas_call(
        functools.partial(_proj_kernel, n_x=len(xs), n_prompt_tiles=npt, n_rope_cols=n_rope_cols,
                          rope_shift=rope_shift, tn=tn, with_gates=gates is not None),
        grid=(m // tm,),
        in_specs=in_specs, out_specs=out_specs, out_shape=out_shape,
        compiler_params=_cparams("arbitrary"),
        name="mixer_in_proj_gated" if gates is not None else "mixer_in_proj",
    )(*args)


def _out_proj_kernel(*refs, n_parts, n_x, n_prompt_tiles, tn):
    parts_p, parts_s = refs[:n_parts], refs[n_parts:2 * n_parts]
    w_refs = refs[2 * n_parts:3 * n_parts]
    x_refs = refs[3 * n_parts:3 * n_parts + n_x]
    gate_ref, o_ref = refs[3 * n_parts + n_x:]
    i = pl.program_id(0)

    def run(parts, x_ref):
        for c in range(o_ref.shape[1] // tn):
            cols = slice(c * tn, (c + 1) * tn)
            acc = None
            for p_ref, w_ref in zip(parts, w_refs):
                t = jnp.dot(p_ref[...], w_ref[:, cols], preferred_element_type=F32)
                acc = t if acc is None else acc + t
            o_ref[:, cols] = x_ref[:, cols] + gate_ref[0, :, cols] * acc

    pl.when(i < n_prompt_tiles)(lambda: run(parts_p, x_refs[0]))
    pl.when(i >= n_prompt_tiles)(lambda: run(parts_s, x_refs[-1]))


def mixer_out_proj(parts_p, parts_s, w, e, xs, gate, rows_info):
    m, d = sum(x.shape[0] for x in xs), xs[0].shape[1]
    n_prompt_rows, lat_rows = rows_info
    tm = _tile(math.gcd(n_prompt_rows, lat_rows), 512, align=ROW_CHUNK)
    npt = n_prompt_rows // tm
    n_parts = len(parts_p)
    in_specs, w_specs, off = [], [], 0
    for p in parts_p:
        in_specs.append(pl.BlockSpec((tm, p.shape[1]), lambda i: (jnp.minimum(i, npt - 1), 0)))
    for p in parts_s:
        kp = p.shape[1]
        in_specs.append(pl.BlockSpec((tm, kp), lambda i: (jnp.maximum(i - npt, 0), 0)))
        assert off % kp == 0
        w_specs.append(pl.BlockSpec((None, kp, d), functools.partial(lambda i, rb: (e, rb, 0), rb=off // kp),
                                    pipeline_mode=pl.Buffered(1)))
        off += kp
    in_specs += w_specs
    in_specs += _stream_specs(xs, tm, npt)
    in_specs += [pl.BlockSpec((1, 1, d), lambda i: (_cond_row(i, tm, n_prompt_rows, lat_rows), 0, 0))]
    return pl.pallas_call(
        functools.partial(_out_proj_kernel, n_parts=n_parts, n_x=len(xs), n_prompt_tiles=npt, tn=_tile(d, 512)),
        grid=(m // tm,),
        in_specs=in_specs,
        out_specs=pl.BlockSpec((tm, d), lambda i: (i, 0)),
        out_shape=jax.ShapeDtypeStruct((m, d), F32),
        compiler_params=_cparams("arbitrary"),
        name="mixer_out_proj",
    )(*parts_p, *parts_s, *([w] * n_parts), *xs, gate)


def _softmax_parts(parts, extra_logit=None, weight=None, axis=-1):
    m = functools.reduce(jnp.maximum, [jnp.max(p, axis=axis, keepdims=True) for p in parts])
    if extra_logit is not None:
        m = jnp.maximum(m, extra_logit)
    es = [jnp.exp(p - m) for p in parts]
    den = functools.reduce(jnp.add, [jnp.sum(e, axis=axis, keepdims=True) for e in es])
    if extra_logit is not None:
        den = den + jnp.exp(extra_logit - m)
    inv = 1.0 / den if weight is None else weight / den
    return [e * inv for e in es]


def _diff_attn_kernel(*refs, hb, lam_init, has_ctx, emit_kv):
    it = iter(refs)
    lam_ref, g_ref, q_ref, k_ref, v_ref = (next(it) for _ in range(5))
    ck_ref, cv_ref = (next(it), next(it)) if has_ctx else (None, None)
    o_ref = next(it)
    nk_ref, nv_ref = (next(it), next(it)) if emit_kv else (None, None)

    lp = lam_ref[...]
    lam = (jnp.exp(jnp.sum(lp[0:1] * lp[1:2], axis=-1, keepdims=True))
           - jnp.exp(jnp.sum(lp[2:3] * lp[3:4], axis=-1, keepdims=True)) + lam_init)
    scale = A_DH ** -0.5
    assert math.frexp(scale)[0] == 0.5
    first = lax.broadcasted_iota(jnp.int32, (1, 2 * A_DH), 1) < A_DH
    tq = q_ref.shape[0]
    heads = [slice(hh * 128, (hh + 1) * 128) for hh in range(hb)]
    q1, q2, keys, vals = [], [], [], []
    for hh, cols in enumerate(heads):
        q, k, v = q_ref[:, cols] * scale, k_ref[:, cols], v_ref[:, cols]
        q1.append(jnp.where(first, q, 0.0).astype(BF16))
        q2.append(jnp.where(first, 0.0, q).astype(BF16))
        keys.append(([ck_ref[0, 0, hh].astype(BF16)] if has_ctx else []) + [k.astype(BF16)])
        vals.append(([cv_ref[0, 0, hh].astype(BF16)] if has_ctx else []) + [v.astype(BF16)])
        if emit_kv:
            nk_ref[0, 0, hh] = k
            nv_ref[0, 0, hh] = v
    parts = range(len(keys[0]))
    stack = lambda qs: [jnp.concatenate([_dot_nt(qs[hh], keys[hh][p]) for hh in range(hb)], axis=0) for p in parts]
    p1 = _softmax_parts(stack(q1))
    p2 = _softmax_parts(stack(q2), weight=lam)
    pd = [(a - b).astype(BF16) for a, b in zip(p1, p2)]
    outs = []
    for hh in range(hb):
        o = None
        for p in parts:
            t = jnp.dot(pd[p][hh * tq:(hh + 1) * tq], vals[hh][p], preferred_element_type=F32)
            o = t if o is None else o + t
        outs.append(o)
    y = ((_rms(jnp.concatenate(outs, axis=0)) * g_ref[...]) * (1.0 - lam_init)).astype(BF16)
    for hh, cols in enumerate(heads):
        o_ref[:, cols] = y[hh * tq:(hh + 1) * tq]


def diff_attention(proj, lam_p, subln_g, lam_init, *, row_off, batch, seq, tq, hb, ctx=None, emit_kv=False):
    width = hb * 128
    n_hg = A_HEADS // hb
    assert row_off % seq == 0 and seq % tq == 0
    qrow = lambda b, hg, i: ((row_off + b * seq) // tq + i, hg)
    krow = lambda b, hg, i: ((row_off + b * seq) // seq, A_QK_W // width + hg)
    vrow = lambda b, hg, i: ((row_off + b * seq) // seq, 2 * A_QK_W // width + hg)
    in_specs = [pl.BlockSpec(lam_p.shape, lambda b, hg, i: (0, 0)),
                pl.BlockSpec((1, A_DV), lambda b, hg, i: (0, 0)),
                pl.BlockSpec((tq, width), qrow),
                pl.BlockSpec((seq, width), krow),
                pl.BlockSpec((seq, width), vrow)]
    args = [lam_p, subln_g.reshape(1, A_DV), proj, proj, proj]
    if ctx is not None:
        ctx_k, ctx_v, e = ctx
        past = ctx_k.shape[3]
        cspec = lambda b, hg, i: (b, e, hg, 0, 0)
        in_specs += [pl.BlockSpec((1, 1, hb, past, 2 * A_DH), cspec), pl.BlockSpec((1, 1, hb, past, A_DV), cspec)]
        args += [ctx_k, ctx_v]
    out_specs = [pl.BlockSpec((tq, width), lambda b, hg, i: (b * (seq // tq) + i, hg))]
    out_shape = [jax.ShapeDtypeStruct((batch * seq, A_V_W), BF16)]
    if emit_kv:
        assert tq == seq
        kvspec = pl.BlockSpec((1, 1, hb, seq, 128), lambda b, hg, i: (b, 0, hg, 0, 0))
        out_specs += [kvspec, kvspec]
        out_shape += [jax.ShapeDtypeStruct((batch, 1, A_HEADS, seq, 128), F32)] * 2
    return pl.pallas_call(
        functools.partial(_diff_attn_kernel, hb=hb, lam_init=lam_init, has_ctx=ctx is not None, emit_kv=emit_kv),
        grid=(batch, n_hg, seq // tq),
        in_specs=in_specs, out_specs=out_specs, out_shape=out_shape,
        compiler_params=_cparams("arbitrary", "arbitrary", "arbitrary"),
        name="diff_attn_ctx" if ctx is not None else "diff_attn",
    )(*args)


GLA_FINE_HALF = 2


def _gla_level_matrices(chunk, fwd):
    t = np.arange(chunk)[:, None]
    i = np.arange(chunk)[None, :]
    mats = [i <= t] if fwd else [i >= t]
    h = GLA_FINE_HALF
    while h >= 1:
        p = t % (2 * h)
        base = t - p
        if fwd:
            m = base + h - 1
            a = np.where(p >= h, (i > m) & (i <= t), (i > t) & (i <= m))
        else:
            m = base + h
            a = np.where(p < h, (i >= t) & (i < m), (i >= m) & (i < t))
        mats.append(a)
        h //= 2
    return jnp.asarray(np.concatenate(mats, axis=0).astype(np.float32), dtype=BF16)


def _gla_kernel(*refs, chunk, has_state, emit_state):
    it = iter(refs)
    af_ref, ab_ref, q_ref, k_ref, v_ref, laf_ref, lab_ref, r_ref, g_ref = (next(it) for _ in range(9))
    sf_ref, sb_ref = (next(it), next(it)) if has_state else (None, None)
    o_ref = next(it)
    nsf_ref, nsb_ref = (next(it), next(it)) if emit_state else (None, None)
    accf_ref, accb_ref, stf_ref, stb_ref = (next(it) for _ in range(4))

    seq = q_ref.shape[0]
    n_chunks = seq // chunk
    n_lev = chunk.bit_length() - 1
    ti = lax.broadcasted_iota(jnp.int32, (chunk, chunk), 0)
    si = lax.broadcasted_iota(jnp.int32, (chunk, chunk), 1)
    split = ti ^ si
    scale = B_DK ** -0.5

    def chunk_step(c, fwd, a_ref, la_ref, acc_ref, st_ref):
        rows = c * chunk if isinstance(c, int) else pl.multiple_of(c * chunk, chunk)
        rows = pl.ds(rows, chunk)
        order = (ti > si) if fwd else (ti < si)
        q = q_ref[rows, :] * scale
        k = k_ref[rows, :]
        vb = v_ref[rows, :].astype(BF16)
        la = la_ref[rows, :]
        la_hi = la.astype(BF16)
        la_lo = (la - la_hi.astype(F32)).astype(BF16)
        r2 = jnp.dot(a_ref[...], jnp.concatenate([la_hi, la_lo], axis=1), preferred_element_type=F32)
        r = r2[:, :B_DK] + r2[:, B_DK:]
        c = r[0:chunk]
        e_in = jnp.exp(c)
        total = e_in[chunk - 1:chunk] if fwd else e_in[0:1]
        st = st_ref[...]
        inter = _dot_nt((q * e_in).astype(BF16), st.astype(BF16))
        ku = (k * jnp.exp((c[chunk - 1:chunk] if fwd else c[0:1]) - c)).astype(BF16)
        scores = jnp.where(ti == si, _dot_nt(q.astype(BF16), k.astype(BF16)), 0.0)
        for lev in range(n_lev):
            half = chunk >> (lev + 1)
            if half > GLA_FINE_HALF:
                c3 = c.reshape(chunk // (2 * half), 2 * half, B_DK)
                pivot = c3[:, half - 1:half, :] if fwd else c3[:, half:half + 1, :]
                f = jnp.exp(-jnp.abs(c3 - pivot)).reshape(chunk, B_DK)
            else:
                fine = (GLA_FINE_HALF // half).bit_length()
                f = jnp.exp(r[fine * chunk:(fine + 1) * chunk])
            sc = _dot_nt((q * f).astype(BF16), (k * f).astype(BF16))
            scores = jnp.where(order & (split >= half) & (split < 2 * half), sc, scores)
        acc_ref[rows, :] = inter + jnp.dot(scores.astype(BF16), vb, preferred_element_type=F32)
        st_ref[...] = st * total + _dot_tn(vb, ku)

    for s0_ref, st_ref in ((sf_ref, stf_ref), (sb_ref, stb_ref)):
        st_ref[...] = s0_ref[0, 0, 0].T if has_state else jnp.zeros(st_ref.shape, F32)

    def body(cc, carry):
        chunk_step(cc, True, af_ref, laf_ref, accf_ref, stf_ref)
        chunk_step(n_chunks - 1 - cc, False, ab_ref, lab_ref, accb_ref, stb_ref)
        return carry

    if n_chunks == 1:
        body(0, 0)
    else:
        lax.fori_loop(0, n_chunks, body, 0)
    if emit_state:
        nsf_ref[0, 0, 0] = stf_ref[...].T
        nsb_ref[0, 0, 0] = stb_ref[...].T

    def epilogue(r, carry):
        rows = pl.ds(pl.multiple_of(r * ROW_CHUNK, ROW_CHUNK), ROW_CHUNK)
        gate = r_ref[rows, :]
        o = accf_ref[rows, :] + accb_ref[rows, :]
        o_ref[rows, :] = ((_rms(o) * g_ref[...]) * (gate * jax.nn.sigmoid(gate))).astype(BF16)
        return carry

    lax.fori_loop(0, seq // ROW_CHUNK, epilogue, 0)


def gla_bidirectional(proj, gates, bnorm_g, *, row_off, batch, seq, states=None, emit_state=False):
    chunk = min(256, seq)
    assert row_off % seq == 0 and seq % chunk == 0 and chunk & (chunk - 1) == 0
    rb = lambda b: (row_off + b * seq) // seq
    q_off = (2 * A_QK_W + A_V_W) // B_DK
    k_off = q_off + B_HEADS
    v_off = (2 * A_QK_W + A_V_W + 2 * B_QK_W) // B_DV
    r_off = v_off + B_HEADS
    a_f, a_b = _gla_level_matrices(chunk, True), _gla_level_matrices(chunk, False)
    whole = lambda b, h: (0, 0)
    in_specs = [pl.BlockSpec(a_f.shape, whole), pl.BlockSpec(a_b.shape, whole),
                pl.BlockSpec((seq, B_DK), lambda b, h: (rb(b), q_off + h)),
                pl.BlockSpec((seq, B_DK), lambda b, h: (rb(b), k_off + h)),
                pl.BlockSpec((seq, B_DV), lambda b, h: (rb(b), v_off + h)),
                pl.BlockSpec((seq, B_DK), lambda b, h: (rb(b), h)),
                pl.BlockSpec((seq, B_DK), lambda b, h: (rb(b), B_HEADS + h)),
                pl.BlockSpec((seq, B_DV), lambda b, h: (rb(b), r_off + h)),
                pl.BlockSpec((1, B_DV), whole)]
    args = [a_f, a_b, proj, proj, proj, gates, gates, proj, bnorm_g.reshape(1, B_DV)]
    if states is not None:
        s_f, s_b, e = states
        sspec = pl.BlockSpec((1, 1, 1, B_DK, B_DV), lambda b, h: (b, e, h, 0, 0))
        in_specs += [sspec, sspec]
        args += [s_f, s_b]
    out_specs = [pl.BlockSpec((seq, B_DV), lambda b, h: (b, h))]
    out_shape = [jax.ShapeDtypeStruct((batch * seq, B_V_W), BF16)]
    if emit_state:
        nspec = pl.BlockSpec((1, 1, 1, B_DK, B_DV), lambda b, h: (b, 0, h, 0, 0))
        out_specs += [nspec, nspec]
        out_shape += [jax.ShapeDtypeStruct((batch, 1, B_HEADS, B_DK, B_DV), F32)] * 2
    return pl.pallas_call(
        functools.partial(_gla_kernel, chunk=chunk, has_state=states is not None, emit_state=emit_state),
        grid=(batch, B_HEADS),
        in_specs=in_specs, out_specs=out_specs, out_shape=out_shape,
        scratch_shapes=[pltpu.VMEM((seq, B_DV), F32)] * 2 + [pltpu.VMEM((B_DV, B_DK), F32)] * 2,
        compiler_params=_cparams("arbitrary", "arbitrary"),
        name="gla_state" if states is not None else "gla",
    )(*args)


def _sink_attn_kernel(sink_ref, q_ref, k_ref, v_ref, o_ref, nk_ref, nv_ref):
    seq = q_ref.shape[0]
    scale = C_DH ** -0.5
    scores, sinks, vals = [], [], []
    for hk in range(C_KV_HEADS):
        kcols = slice(hk * C_DH, (hk + 1) * C_DH)
        k, v = k_ref[:, kcols], v_ref[:, kcols]
        nk_ref[0, 0, hk] = k
        nv_ref[0, 0, hk] = v
        vals.append(v.T.astype(BF16))
        q4 = jnp.concatenate([q_ref[:, (hk * C_GROUP + g) * C_DH:(hk * C_GROUP + g + 1) * C_DH]
                              for g in range(C_GROUP)], axis=0).astype(BF16)
        scores.append(_dot_nt(k.astype(BF16), q4) * scale)
        sinks += [jnp.broadcast_to(sink_ref[hk, 0:1, g:g + 1], (1, seq)) for g in range(C_GROUP)]
    (p,) = _softmax_parts([jnp.concatenate(scores, axis=1)], extra_logit=jnp.concatenate(sinks, axis=1), axis=0)
    p = p.astype(BF16)
    cols_per_kv = C_GROUP * seq
    for hk in range(C_KV_HEADS):
        o = jnp.dot(vals[hk], p[:, hk * cols_per_kv:(hk + 1) * cols_per_kv], preferred_element_type=F32).T
        for g in range(C_GROUP):
            h = hk * C_GROUP + g
            o_ref[:, h * C_DH:(h + 1) * C_DH] = o[g * seq:(g + 1) * seq].astype(BF16)


def sink_attention_context(proj, sink, *, batch, seq):
    sink3 = sink.reshape(C_KV_HEADS, 1, C_GROUP)
    kvspec = pl.BlockSpec((1, 1, C_KV_HEADS, seq, C_DH), lambda b: (b, 0, 0, 0, 0))
    return pl.pallas_call(
        _sink_attn_kernel,
        grid=(batch,),
        in_specs=[pl.BlockSpec(sink3.shape, lambda b: (0, 0, 0)),
                  pl.BlockSpec((seq, C_Q_W), lambda b: (b, 0)),
                  pl.BlockSpec((seq, C_KV_W), lambda b: (b, C_Q_W // C_KV_W)),
                  pl.BlockSpec((seq, C_KV_W), lambda b: (b, C_Q_W // C_KV_W + 1))],
        out_specs=[pl.BlockSpec((seq, C_Q_W), lambda b: (b, 0)), kvspec, kvspec],
        out_shape=[jax.ShapeDtypeStruct((batch * seq, C_Q_W), BF16)]
        + [jax.ShapeDtypeStruct((batch, 1, C_KV_HEADS, seq, C_DH), F32)] * 2,
        compiler_params=_cparams("arbitrary"),
        name="sink_attn",
    )(sink3, proj, proj, proj)


def _window_attn_kernel(sink_ref, q_ref, k_ref, v_ref, ck_ref, cv_ref, o_ref, *, band, hkb):
    tq = q_ref.shape[0]
    seq = k_ref.shape[0]
    scale = C_DH ** -0.5
    i = pl.program_id(2)
    start = pl.multiple_of(jnp.clip(i * tq - C_WINDOW, 0, seq - band), C_WINDOW)
    kpos = start + lax.broadcasted_iota(jnp.int32, (band, tq), 0)
    qpos = i * tq + lax.broadcasted_iota(jnp.int32, (band, tq), 1)
    valid = jnp.abs(qpos - kpos) <= C_WINDOW
    valid4 = jnp.concatenate([valid] * C_GROUP, axis=1)
    gw = C_GROUP * C_DH
    s_ctx, s_band, sinks, vbs, cvs = [], [], [], [], []
    for hk in range(hkb):
        kcols = slice(hk * C_DH, (hk + 1) * C_DH)
        kb = k_ref[pl.ds(start, band), kcols].astype(BF16)
        vbs.append(v_ref[pl.ds(start, band), kcols].T.astype(BF16))
        cvs.append(cv_ref[0, 0, hk].T.astype(BF16))
        q4 = jnp.concatenate([q_ref[:, hk * gw + g * C_DH:hk * gw + (g + 1) * C_DH] for g in range(C_GROUP)],
                             axis=0).astype(BF16)
        sinks += [jnp.broadcast_to(sink_ref[hk, 0:1, g:g + 1], (1, tq)) for g in range(C_GROUP)]
        s_band.append(jnp.where(valid4, _dot_nt(kb, q4) * scale, -jnp.inf))
        s_ctx.append(_dot_nt(ck_ref[0, 0, hk].astype(BF16), q4) * scale)
    p_ctx, p_band = _softmax_parts([jnp.concatenate(s_ctx, axis=1), jnp.concatenate(s_band, axis=1)],
                                   extra_logit=jnp.concatenate(sinks, axis=1), axis=0)
    p_ctx, p_band = p_ctx.astype(BF16), p_band.astype(BF16)
    cols_per_kv = C_GROUP * tq
    for hk in range(hkb):
        cols = slice(hk * cols_per_kv, (hk + 1) * cols_per_kv)
        o = (jnp.dot(cvs[hk], p_ctx[:, cols], preferred_element_type=F32)
             + jnp.dot(vbs[hk], p_band[:, cols], preferred_element_type=F32)).T
        for g in range(C_GROUP):
            o_ref[:, hk * gw + g * C_DH:hk * gw + (g + 1) * C_DH] = o[g * tq:(g + 1) * tq].astype(BF16)


def window_attention_latent(proj, sink, ctx_k, ctx_v, e, *, row_off, batch, seq, tq=128, hkb=4):
    tq = min(tq, seq)
    band = min(tq + 2 * C_WINDOW, seq)
    assert row_off % seq == 0 and seq % tq == 0 and tq % C_WINDOW == 0 and C_KV_HEADS % hkb == 0
    past = ctx_k.shape[3]
    qw, kw = hkb * C_GROUP * C_DH, hkb * C_DH
    rb = lambda b: (row_off + b * seq) // seq
    cspec = pl.BlockSpec((1, 1, hkb, past, C_DH), lambda b, hg, i: (b, e, hg, 0, 0))
    sink3 = sink.reshape(C_KV_HEADS, 1, C_GROUP)
    return pl.pallas_call(
        functools.partial(_window_attn_kernel, band=band, hkb=hkb),
        grid=(batch, C_KV_HEADS // hkb, seq // tq),
        in_specs=[pl.BlockSpec((hkb, 1, C_GROUP), lambda b, hg, i: (hg, 0, 0)),
                  pl.BlockSpec((tq, qw), lambda b, hg, i: ((row_off + b * seq) // tq + i, hg)),
                  pl.BlockSpec((seq, kw), lambda b, hg, i: (rb(b), C_Q_W // kw + hg)),
                  pl.BlockSpec((seq, kw), lambda b, hg, i: (rb(b), (C_Q_W + C_KV_W) // kw + hg)),
                  cspec, cspec],
        out_specs=pl.BlockSpec((tq, qw), lambda b, hg, i: (b * (seq // tq) + i, hg)),
        out_shape=jax.ShapeDtypeStruct((batch * seq, C_Q_W), BF16),
        compiler_params=_cparams("arbitrary", "arbitrary", "arbitrary"),
        name="window_attn",
    )(sink3, proj, proj, proj, ctx_k, ctx_v)


def kernel(x_prompt, x_sample, cache_a_k, cache_a_v, state_b_fwd, state_b_bwd, cache_c_k, cache_c_v, c, c_ctx, ada_w, ada_b, norm_g, ffn_w_in, ffn_w_out, ab_w_in, ab_w_out, a_lambda, a_subln_g, b_alpha_w, b_alpha_b, b_norm_g, c_w_in, c_w_out, c_sink, final_g):
    bp, sp, d = x_prompt.shape
    bs, ss, _ = x_sample.shape
    depth = ada_w.shape[0]
    mp, ms = bp * sp, bs * ss
    rows_info = (mp, ss)

    cond = jnp.concatenate([c_ctx[None, :], c, jnp.zeros((COND_ROWS - 1 - bs, d), F32)], axis=0)
    mods = adaln(cond, ada_w, ada_b).reshape(depth, COND_ROWS, N_MOD, d)
    rope_a = _rope_tables(ss, A_DH, 2) + (2 * A_QK_W,)
    rope_c = _rope_tables(ss, C_DH, 1) + (C_Q_W + C_KV_W,)
    ab_w_in, ab_w_out, c_w_in, c_w_out = (w.astype(BF16) for w in (ab_w_in, ab_w_out, c_w_in, c_w_out))

    a_k, a_v, b_f, b_b, c_k, c_v = [], [], [], [], [], []
    xs = [x_prompt.reshape(mp, d), x_sample.reshape(ms, d)]
    for l in range(depth):
        mod = mods[l]
        ffn1 = functools.partial(ffn_half_step, mod3=mod[:, 0:3], g=norm_g[l, 0], w_in=ffn_w_in, w_out=ffn_w_out,
                                 widx=(l, 0), rows_info=rows_info)
        xs = [ffn1(xs[0])] if len(xs) == 1 else [ffn1(xs[0]), ffn1(xs[1], stream_row0=mp)]
        if l % 2 == 0:
            e = l // 2
            lam_init = 0.8 - 0.6 * math.exp(-0.3 * l)
            zeros = jnp.zeros((B_RANK, B_QK_W), F32)
            w_alpha = jnp.concatenate([jnp.concatenate([b_alpha_w[e, 0], zeros], axis=1),
                                       jnp.concatenate([zeros, b_alpha_w[e, 1]], axis=1)], axis=0)
            gate_w = (ab_w_in[e][:, AB_MAIN:], w_alpha, b_alpha_b[e].reshape(-1))
            proj, gates = mixer_in_proj(xs, mod[:, 3:5], norm_g[l, 1], ab_w_in, e, AB_MAIN, rows_info, rope_a,
                                        gates=gate_w)
            attn_p, ak, av = diff_attention(proj, a_lambda[e], a_subln_g[e], lam_init, row_off=0, batch=bp,
                                            seq=sp, tq=sp, hb=A_HEADS, emit_kv=True)
            (attn_s,) = diff_attention(proj, a_lambda[e], a_subln_g[e], lam_init, row_off=mp, batch=bs,
                                       seq=ss, tq=min(256, ss), hb=2, ctx=(cache_a_k, cache_a_v, e))
            gla_p, sf, sb = gla_bidirectional(proj, gates, b_norm_g[e], row_off=0, batch=bp, seq=sp,
                                              emit_state=True)
            (gla_s,) = gla_bidirectional(proj, gates, b_norm_g[e], row_off=mp, batch=bs, seq=ss,
                                         states=(state_b_fwd, state_b_bwd, e))
            a_k.append(ak), a_v.append(av), b_f.append(sf), b_b.append(sb)
            x = mixer_out_proj([attn_p, gla_p], [attn_s, gla_s], ab_w_out, e, xs, mod[:, 5:6], rows_info)
        else:
            o = l // 2
            proj = mixer_in_proj(xs, mod[:, 3:5], norm_g[l, 1], c_w_in, o, c_w_in.shape[2], rows_info, rope_c)
            mix_p, ck, cv = sink_attention_context(proj, c_sink[o], batch=bp, seq=sp)
            mix_s = window_attention_latent(proj, c_sink[o], cache_c_k, cache_c_v, o, row_off=mp, batch=bs, seq=ss)
            c_k.append(ck), c_v.append(cv)
            x = mixer_out_proj([mix_p], [mix_s], c_w_out, o, xs, mod[:, 5:6], rows_info)
        ffn2 = functools.partial(ffn_half_step, x, mod[:, 6:9], norm_g[l, 2], ffn_w_in, ffn_w_out, (l, 1), rows_info)
        if l < depth - 1:
            xs = [ffn2()]

    y_prompt = ffn2(n_rows=mp, final_g=final_g).reshape(bp, sp, d)
    y_sample = ffn2(stream_row0=mp, x_row0=mp, n_rows=ms, final_g=final_g).reshape(bs, ss, d)
    cat = lambda parts: parts[0] if len(parts) == 1 else jnp.concatenate(parts, axis=1)
    return (y_prompt, y_sample, cat(a_k), cat(a_v), cat(b_f), cat(b_b), cat(c_k), cat(c_v))
```

```python
import functools
import math

import numpy as np
import jax
import jax.numpy as jnp
from jax import lax
from jax.experimental import pallas as pl
from jax.experimental.pallas import tpu as pltpu

F32 = jnp.float32
BF16 = jnp.bfloat16

EPS = 1e-6
GRID_W = 64
ROPE_BASE = 10000.0
N_MOD = 9
A_HEADS, A_DH, A_DV = 8, 64, 128
B_HEADS, B_DK, B_DV, B_RANK, B_TAU = 4, 128, 256, 16, 16.0
C_HEADS, C_KV_HEADS, C_DH, C_WINDOW = 16, 4, 128, 128
A_QK_W = A_HEADS * 2 * A_DH
A_V_W = A_HEADS * A_DV
B_QK_W = B_HEADS * B_DK
B_V_W = B_HEADS * B_DV
AB_MAIN = 2 * A_QK_W + A_V_W + 2 * B_QK_W + 2 * B_V_W
C_GROUP = C_HEADS // C_KV_HEADS
C_Q_W = C_HEADS * C_DH
C_KV_W = C_KV_HEADS * C_DH

V7X_VMEM_BYTES = 64 * 1024 * 1024
VMEM_LIMIT_BYTES = V7X_VMEM_BYTES - 8 * 1024 * 1024
COND_ROWS = 8
ROW_CHUNK = 128


def _tile(n, pref, align=128):
    if n <= pref:
        return n
    t = (pref // align) * align
    while n % t:
        t -= align
    assert t > 0, (n, pref)
    return t


def _cparams(*sem):
    return pltpu.CompilerParams(dimension_semantics=sem, vmem_limit_bytes=VMEM_LIMIT_BYTES)


def _bdot(a, b):
    return jnp.dot(a.astype(BF16), b.astype(BF16), preferred_element_type=F32)


def _dot_nt(a, b):
    return lax.dot_general(a, b, (((1,), (1,)), ((), ())), preferred_element_type=F32)


def _dot_tn(a, b):
    return lax.dot_general(a, b, (((0,), (0,)), ((), ())), preferred_element_type=F32)


def _rms(x):
    return x * lax.rsqrt(jnp.mean(x * x, axis=-1, keepdims=True) + EPS)


def _adaln_kernel(cond_ref, w_ref, b_ref, o_ref):
    c = cond_ref[...]
    o_ref[0] = _bdot(c * jax.nn.sigmoid(c), w_ref[0]) + b_ref[0]


def adaln(cond, ada_w, ada_b):
    depth, d, n = ada_w.shape
    tn = _tile(n, 1024)
    return pl.pallas_call(
        _adaln_kernel,
        grid=(depth, n // tn),
        in_specs=[pl.BlockSpec((COND_ROWS, d), lambda l, j: (0, 0)),
                  pl.BlockSpec((1, d, tn), lambda l, j: (l, 0, j)),
                  pl.BlockSpec((1, 1, tn), lambda l, j: (l, 0, j))],
        out_specs=pl.BlockSpec((1, COND_ROWS, tn), lambda l, j: (l, 0, j)),
        out_shape=jax.ShapeDtypeStruct((depth, COND_ROWS, n), F32),
        compiler_params=_cparams("arbitrary", "arbitrary"),
        name="adaln",
    )(cond, ada_w, ada_b.reshape(depth, 1, n))


def _norm_modulate(x_ref, g_ref, mod_ref, h_ref):
    g = g_ref[...]
    shift = mod_ref[0, 0:1, :]
    scale1 = 1.0 + mod_ref[0, 1:2, :]

    def body(r, carry):
        rows = pl.ds(pl.multiple_of(r * ROW_CHUNK, ROW_CHUNK), ROW_CHUNK)
        h_ref[rows, :] = ((_rms(x_ref[rows, :]) * g) * scale1 + shift).astype(BF16)
        return carry

    lax.fori_loop(0, x_ref.shape[0] // ROW_CHUNK, body, 0)


def _row_tile(rows_info):
    n_prompt_rows, lat_rows = rows_info
    return _tile(math.gcd(n_prompt_rows, lat_rows), 1024, align=ROW_CHUNK)


def _cond_row(i, tm, n_prompt_rows, lat_rows):
    return jnp.maximum((i * tm - n_prompt_rows) // lat_rows + 1, 0)


def _ffn_kernel(x_ref, mod_ref, g_ref, wg_ref, wu_ref, wo_ref, *rest, row_split, n_chunk, final):
    if final:
        fg_ref, o_ref, h_ref = rest
    else:
        o_ref, h_ref = rest
    j = pl.program_id(1)

    @pl.when(j == 0)
    def _():
        _norm_modulate(x_ref, g_ref, mod_ref, h_ref)
        o_ref[...] = jnp.zeros(o_ref.shape, F32)

    tm, d = h_ref.shape
    rt = tm // row_split
    wg, wu, wo = (w_ref[...].astype(BF16) for w_ref in (wg_ref, wu_ref, wo_ref))
    for r in range(row_split):
        rows = slice(r * rt, (r + 1) * rt)
        h = h_ref[rows, :]
        gate = jnp.dot(h, wg, preferred_element_type=F32)
        up = jnp.dot(h, wu, preferred_element_type=F32)
        act = (gate * jax.nn.sigmoid(gate) * up).astype(BF16)
        for c in range(d // n_chunk):
            cols = slice(c * n_chunk, (c + 1) * n_chunk)
            o_ref[rows, cols] += jnp.dot(act, wo[:, cols], preferred_element_type=F32)

    @pl.when(j == pl.num_programs(1) - 1)
    def _():
        half_gate = 0.5 * mod_ref[0, 2:3, :]

        def body(r, carry):
            rows = pl.ds(pl.multiple_of(r * ROW_CHUNK, ROW_CHUNK), ROW_CHUNK)
            xn = x_ref[rows, :] + half_gate * o_ref[rows, :]
            if final:
                xn = _rms(xn) * fg_ref[...]
            o_ref[rows, :] = xn
            return carry

        lax.fori_loop(0, tm // ROW_CHUNK, body, 0)


def ffn_half_step(x, mod3, g, w_in, w_out, widx, rows_info, *, stream_row0=0, x_row0=0, n_rows=None,
                  final_g=None):
    d = x.shape[1]
    f = w_out.shape[-2]
    n_prompt_rows, lat_rows = rows_info
    tm, tf, n_chunk = _row_tile(rows_info), _tile(f, 256), _tile(d, 512)
    m = x.shape[0] if n_rows is None else n_rows
    assert x_row0 % tm == 0 and stream_row0 % tm == 0 and m % tm == 0
    tx, ts = x_row0 // tm, stream_row0 // tm
    nf = f // tf
    final = final_g is not None
    l, k = widx
    in_specs = [
        pl.BlockSpec((tm, d), lambda i, j: (tx + i, 0)),
        pl.BlockSpec((1, 3, d), lambda i, j: (_cond_row(ts + i, tm, n_prompt_rows, lat_rows), 0, 0)),
        pl.BlockSpec((1, d), lambda i, j: (0, 0)),
        pl.BlockSpec((None, None, d, tf), lambda i, j: (l, k, 0, j)),
        pl.BlockSpec((None, None, d, tf), lambda i, j: (l, k, 0, nf + j)),
        pl.BlockSpec((None, None, tf, d), lambda i, j: (l, k, j, 0))]
    args = [x, mod3, g.reshape(1, d), w_in, w_in, w_out]
    if final:
        in_specs.append(pl.BlockSpec((1, d), lambda i, j: (0, 0)))
        args.append(final_g.reshape(1, d))
    return pl.pallas_call(
        functools.partial(_ffn_kernel, row_split=max(1, tm // 512), n_chunk=n_chunk, final=final),
        grid=(m // tm, nf),
        in_specs=in_specs,
        out_specs=pl.BlockSpec((tm, d), lambda i, j: (i, 0)),
        out_shape=jax.ShapeDtypeStruct((m, d), F32),
        scratch_shapes=[pltpu.VMEM((tm, d), BF16)],
        compiler_params=_cparams("arbitrary", "arbitrary"),
        name="ffn_final" if final else "ffn",
    )(*args)


def _log_sigmoid(z):
    return jnp.minimum(z, 0.0) - jnp.log(1.0 + jnp.exp(-jnp.abs(z)))


def _proj_kernel(*refs, n_x, n_prompt_tiles, n_rope_cols, rope_shift, tn, with_gates):
    x_refs, refs = refs[:n_x], refs[n_x:]
    mod_ref, g_ref, w_ref, cos_ref, sa_ref, sb_ref = refs[:6]
    if with_gates:
        wl_ref, wa_ref, ba_ref, o_ref, gates_ref = refs[6:]
    else:
        (o_ref,) = refs[6:]

    def project(x_ref, rotary):
        x = x_ref[...]
        h = (((_rms(x) * g_ref[...]) * (1.0 + mod_ref[0, 1:2, :])) + mod_ref[0, 0:1, :]).astype(BF16)
        if with_gates:
            low = jnp.dot(h, wl_ref[:, 0:wa_ref.shape[0]], preferred_element_type=F32)
            z = jnp.dot(low.astype(BF16), wa_ref[...].astype(BF16), preferred_element_type=F32) + ba_ref[...]
            gates_ref[...] = _log_sigmoid(z) * (1.0 / B_TAU)
        for c in range(o_ref.shape[1] // tn):
            acc = jnp.dot(h, w_ref[:, c * tn:(c + 1) * tn], preferred_element_type=F32)
            if rotary and c * tn < n_rope_cols:
                cos, sa, sb = cos_ref[...], sa_ref[...], sb_ref[...]
                for s in range(tn // 128):
                    a = acc[:, s * 128:(s + 1) * 128]
                    o_ref[:, c * tn + s * 128:c * tn + (s + 1) * 128] = (
                        a * cos + pltpu.roll(a, 128 - rope_shift, 1) * sa + pltpu.roll(a, rope_shift, 1) * sb)
            else:
                o_ref[:, c * tn:(c + 1) * tn] = acc

    i = pl.program_id(0)
    pl.when(i < n_prompt_tiles)(lambda: project(x_refs[0], False))
    pl.when(i >= n_prompt_tiles)(lambda: project(x_refs[-1], True))


def _rope_tables(n, head_dim, n_sub):
    assert head_dim * n_sub == 128
    rows = n // GRID_W
    row = jnp.repeat(jnp.arange(rows, dtype=F32), GRID_W)
    col = jnp.tile(jnp.arange(GRID_W, dtype=F32), rows)
    d_axis = head_dim // 2
    shift = d_axis // 2
    inv = ROPE_BASE ** (-jnp.arange(0, d_axis, 2, dtype=F32) / d_axis)
    lane = jnp.arange(128)
    sub = lane % head_dim
    is_col = (sub // d_axis) == 1
    within = sub % d_axis
    freq = within % shift
    second = (within // shift) == 1
    pos = jnp.where(is_col[None, :], col[:, None], row[:, None])
    ang = pos * inv[freq][None, :]
    cos, sin = jnp.cos(ang), jnp.sin(ang)
    sin_a = jnp.where(second[None, :], 0.0, -sin)
    sin_b = jnp.where(second[None, :], sin, 0.0)
    return cos, sin_a, sin_b, shift


def _stream_specs(xs, tm, npt):
    d = xs[0].shape[1]
    if len(xs) == 1:
        return [pl.BlockSpec((tm, d), lambda i: (i, 0))]
    return [pl.BlockSpec((tm, d), lambda i: (jnp.minimum(i, npt - 1), 0)),
            pl.BlockSpec((tm, d), lambda i: (jnp.maximum(i - npt, 0), 0))]


def mixer_in_proj(xs, mod2, g, w, e, n_cols, rows_info, rope, gates=None):
    m, d = sum(x.shape[0] for x in xs), xs[0].shape[1]
    n_prompt_rows, lat_rows = rows_info
    cos, sin_a, sin_b, rope_shift, n_rope_cols = rope
    tm = _tile(math.gcd(n_prompt_rows, lat_rows), 256, align=ROW_CHUNK)
    tn = _tile(math.gcd(n_cols, n_rope_cols), 512)
    npt, lat_tiles = n_prompt_rows // tm, lat_rows // tm
    tab = lambda i: (jnp.maximum(i - npt, 0) % lat_tiles, 0)
    once = dict(pipeline_mode=pl.Buffered(1))
    in_specs = _stream_specs(xs, tm, npt) + [
        pl.BlockSpec((1, 2, d), lambda i: (_cond_row(i, tm, n_prompt_rows, lat_rows), 0, 0)),
        pl.BlockSpec((1, d), lambda i: (0, 0)),
        pl.BlockSpec((None, d, n_cols), lambda i: (e, 0, 0), **once),
        pl.BlockSpec((tm, 128), tab), pl.BlockSpec((tm, 128), tab), pl.BlockSpec((tm, 128), tab)]
    args = list(xs) + [mod2, g.reshape(1, d), w, cos, sin_a, sin_b]
    out_specs = pl.BlockSpec((tm, n_cols), lambda i: (i, 0))
    out_shape = jax.ShapeDtypeStruct((m, n_cols), F32)
    if gates is not None:
        low_col0, w_alpha, b_alpha = gates
        n_gate = w_alpha.shape[1]
        assert low_col0 % 128 == 0 and w_alpha.shape[0] <= 128
        in_specs += [pl.BlockSpec((None, d, 128), lambda i: (e, 0, low_col0 // 128), **once),
                     pl.BlockSpec(w_alpha.shape, lambda i: (0, 0)),
                     pl.BlockSpec((1, n_gate), lambda i: (0, 0))]
        args += [w, w_alpha, b_alpha.reshape(1, n_gate)]
        out_specs = [out_specs, pl.BlockSpec((tm, n_gate), lambda i: (i, 0))]
        out_shape = [out_shape, jax.ShapeDtypeStruct((m, n_gate), F32)]
    return pl.pallas_call(
        functools.partial(_proj_kernel, n_x=len(xs), n_prompt_tiles=npt, n_rope_cols=n_rope_cols,
                          rope_shift=rope_shift, tn=tn, with_gates=gates is not None),
        grid=(m // tm,),
        in_specs=in_specs, out_specs=out_specs, out_shape=out_shape,
        compiler_params=_cparams("arbitrary"),
        name="mixer_in_proj_gated" if gates is not None else "mixer_in_proj",
    )(*args)


def _out_proj_kernel(*refs, n_parts, n_x, n_prompt_tiles, tn):
    parts_p, parts_s = refs[:n_parts], refs[n_parts:2 * n_parts]
    w_refs = refs[2 * n_parts:3 * n_parts]
    x_refs = refs[3 * n_parts:3 * n_parts + n_x]
    gate_ref, o_ref = refs[3 * n_parts + n_x:]
    i = pl.program_id(0)

    def run(parts, x_ref):
        for c in range(o_ref.shape[1] // tn):
            cols = slice(c * tn, (c + 1) * tn)
            acc = None
            for p_ref, w_ref in zip(parts, w_refs):
                t = jnp.dot(p_ref[...], w_ref[:, cols], preferred_element_type=F32)
                acc = t if acc is None else acc + t
            o_ref[:, cols] = x_ref[:, cols] + gate_ref[0, :, cols] * acc

    pl.when(i < n_prompt_tiles)(lambda: run(parts_p, x_refs[0]))
    pl.when(i >= n_prompt_tiles)(lambda: run(parts_s, x_refs[-1]))


def mixer_out_proj(parts_p, parts_s, w, e, xs, gate, rows_info):
    m, d = sum(x.shape[0] for x in xs), xs[0].shape[1]
    n_prompt_rows, lat_rows = rows_info
    tm = _tile(math.gcd(n_prompt_rows, lat_rows), 512, align=ROW_CHUNK)
    npt = n_prompt_rows // tm
    n_parts = len(parts_p)
    in_specs, w_specs, off = [], [], 0
    for p in parts_p:
        in_specs.append(pl.BlockSpec((tm, p.shape[1]), lambda i: (jnp.minimum(i, npt - 1), 0)))
    for p in parts_s:
        kp = p.shape[1]
        in_specs.append(pl.BlockSpec((tm, kp), lambda i: (jnp.maximum(i - npt, 0), 0)))
        assert off % kp == 0
        w_specs.append(pl.BlockSpec((None, kp, d), functools.partial(lambda i, rb: (e, rb, 0), rb=off // kp),
                                    pipeline_mode=pl.Buffered(1)))
        off += kp
    in_specs += w_specs
    in_specs += _stream_specs(xs, tm, npt)
    in_specs += [pl.BlockSpec((1, 1, d), lambda i: (_cond_row(i, tm, n_prompt_rows, lat_rows), 0, 0))]
    return pl.pallas_call(
        functools.partial(_out_proj_kernel, n_parts=n_parts, n_x=len(xs), n_prompt_tiles=npt, tn=_tile(d, 512)),
        grid=(m // tm,),
        in_specs=in_specs,
        out_specs=pl.BlockSpec((tm, d), lambda i: (i, 0)),
        out_shape=jax.ShapeDtypeStruct((m, d), F32),
        compiler_params=_cparams("arbitrary"),
        name="mixer_out_proj",
    )(*parts_p, *parts_s, *([w] * n_parts), *xs, gate)


def _softmax_parts(parts, extra_logit=None, weight=None, axis=-1):
    m = functools.reduce(jnp.maximum, [jnp.max(p, axis=axis, keepdims=True) for p in parts])
    if extra_logit is not None:
        m = jnp.maximum(m, extra_logit)
    es = [jnp.exp(p - m) for p in parts]
    den = functools.reduce(jnp.add, [jnp.sum(e, axis=axis, keepdims=True) for e in es])
    if extra_logit is not None:
        den = den + jnp.exp(extra_logit - m)
    inv = 1.0 / den if weight is None else weight / den
    return [e * inv for e in es]


def _diff_attn_kernel(*refs, hb, lam_init, has_ctx, emit_kv):
    it = iter(refs)
    lam_ref, g_ref, q_ref, k_ref, v_ref = (next(it) for _ in range(5))
    ck_ref, cv_ref = (next(it), next(it)) if has_ctx else (None, None)
    o_ref = next(it)
    nk_ref, nv_ref = (next(it), next(it)) if emit_kv else (None, None)

    lp = lam_ref[...]
    lam = (jnp.exp(jnp.sum(lp[0:1] * lp[1:2], axis=-1, keepdims=True))
           - jnp.exp(jnp.sum(lp[2:3] * lp[3:4], axis=-1, keepdims=True)) + lam_init)
    scale = A_DH ** -0.5
    assert math.frexp(scale)[0] == 0.5
    first = lax.broadcasted_iota(jnp.int32, (1, 2 * A_DH), 1) < A_DH
    tq = q_ref.shape[0]
    heads = [slice(hh * 128, (hh + 1) * 128) for hh in range(hb)]
    q1, q2, keys, vals = [], [], [], []
    for hh, cols in enumerate(heads):
        q, k, v = q_ref[:, cols] * scale, k_ref[:, cols], v_ref[:, cols]
        q1.append(jnp.where(first, q, 0.0).astype(BF16))
        q2.append(jnp.where(first, 0.0, q).astype(BF16))
        keys.append(([ck_ref[0, 0, hh].astype(BF16)] if has_ctx else []) + [k.astype(BF16)])
        vals.append(([cv_ref[0, 0, hh].astype(BF16)] if has_ctx else []) + [v.astype(BF16)])
        if emit_kv:
            nk_ref[0, 0, hh] = k
            nv_ref[0, 0, hh] = v
    parts = range(len(keys[0]))
    stack = lambda qs: [jnp.concatenate([_dot_nt(qs[hh], keys[hh][p]) for hh in range(hb)], axis=0) for p in parts]
    p1 = _softmax_parts(stack(q1))
    p2 = _softmax_parts(stack(q2), weight=lam)
    pd = [(a - b).astype(BF16) for a, b in zip(p1, p2)]
    outs = []
    for hh in range(hb):
        o = None
        for p in parts:
            t = jnp.dot(pd[p][hh * tq:(hh + 1) * tq], vals[hh][p], preferred_element_type=F32)
            o = t if o is None else o + t
        outs.append(o)
    y = ((_rms(jnp.concatenate(outs, axis=0)) * g_ref[...]) * (1.0 - lam_init)).astype(BF16)
    for hh, cols in enumerate(heads):
        o_ref[:, cols] = y[hh * tq:(hh + 1) * tq]


def diff_attention(proj, lam_p, subln_g, lam_init, *, row_off, batch, seq, tq, hb, ctx=None, emit_kv=False):
    width = hb * 128
    n_hg = A_HEADS // hb
    assert row_off % seq == 0 and seq % tq == 0
    qrow = lambda b, hg, i: ((row_off + b * seq) // tq + i, hg)
    krow = lambda b, hg, i: ((row_off + b * seq) // seq, A_QK_W // width + hg)
    vrow = lambda b, hg, i: ((row_off + b * seq) // seq, 2 * A_QK_W // width + hg)
    in_specs = [pl.BlockSpec(lam_p.shape, lambda b, hg, i: (0, 0)),
                pl.BlockSpec((1, A_DV), lambda b, hg, i: (0, 0)),
                pl.BlockSpec((tq, width), qrow),
                pl.BlockSpec((seq, width), krow),
                pl.BlockSpec((seq, width), vrow)]
    args = [lam_p, subln_g.reshape(1, A_DV), proj, proj, proj]
    if ctx is not None:
        ctx_k, ctx_v, e = ctx
        past = ctx_k.shape[3]
        cspec = lambda b, hg, i: (b, e, hg, 0, 0)
        in_specs += [pl.BlockSpec((1, 1, hb, past, 2 * A_DH), cspec), pl.BlockSpec((1, 1, hb, past, A_DV), cspec)]
        args += [ctx_k, ctx_v]
    out_specs = [pl.BlockSpec((tq, width), lambda b, hg, i: (b * (seq // tq) + i, hg))]
    out_shape = [jax.ShapeDtypeStruct((batch * seq, A_V_W), BF16)]
    if emit_kv:
        assert tq == seq
        kvspec = pl.BlockSpec((1, 1, hb, seq, 128), lambda b, hg, i: (b, 0, hg, 0, 0))
        out_specs += [kvspec, kvspec]
        out_shape += [jax.ShapeDtypeStruct((batch, 1, A_HEADS, seq, 128), F32)] * 2
    return pl.pallas_call(
        functools.partial(_diff_attn_kernel, hb=hb, lam_init=lam_init, has_ctx=ctx is not None, emit_kv=emit_kv),
        grid=(batch, n_hg, seq // tq),
        in_specs=in_specs, out_specs=out_specs, out_shape=out_shape,
        compiler_params=_cparams("arbitrary", "arbitrary", "arbitrary"),
        name="diff_attn_ctx" if ctx is not None else "diff_attn",
    )(*args)


GLA_FINE_HALF = 2


def _gla_level_matrices(chunk, fwd):
    t = np.arange(chunk)[:, None]
    i = np.arange(chunk)[None, :]
    mats = [i <= t] if fwd else [i >= t]
    h = GLA_FINE_HALF
    while h >= 1:
        p = t % (2 * h)
        base = t - p
        if fwd:
            m = base + h - 1
            a = np.where(p >= h, (i > m) & (i <= t), (i > t) & (i <= m))
        else:
            m = base + h
            a = np.where(p < h, (i >= t) & (i < m), (i >= m) & (i < t))
        mats.append(a)
        h //= 2
    return jnp.asarray(np.concatenate(mats, axis=0).astype(np.float32), dtype=BF16)


def _gla_kernel(*refs, chunk, has_state, emit_state):
    it = iter(refs)
    af_ref, ab_ref, q_ref, k_ref, v_ref, laf_ref, lab_ref, r_ref, g_ref = (next(it) for _ in range(9))
    sf_ref, sb_ref = (next(it), next(it)) if has_state else (None, None)
    o_ref = next(it)
    nsf_ref, nsb_ref = (next(it), next(it)) if emit_state else (None, None)
    accf_ref, accb_ref, stf_ref, stb_ref = (next(it) for _ in range(4))

    seq = q_ref.shape[0]
    n_chunks = seq // chunk
    n_lev = chunk.bit_length() - 1
    ti = lax.broadcasted_iota(jnp.int32, (chunk, chunk), 0)
    si = lax.broadcasted_iota(jnp.int32, (chunk, chunk), 1)
    split = ti ^ si
    scale = B_DK ** -0.5

    def chunk_step(c, fwd, a_ref, la_ref, acc_ref, st_ref):
        rows = c * chunk if isinstance(c, int) else pl.multiple_of(c * chunk, chunk)
        rows = pl.ds(rows, chunk)
        order = (ti > si) if fwd else (ti < si)
        q = q_ref[rows, :] * scale
        k = k_ref[rows, :]
        vb = v_ref[rows, :].astype(BF16)
        la = la_ref[rows, :]
        la_hi = la.astype(BF16)
        la_lo = (la - la_hi.astype(F32)).astype(BF16)
        r2 = jnp.dot(a_ref[...], jnp.concatenate([la_hi, la_lo], axis=1), preferred_element_type=F32)
        r = r2[:, :B_DK] + r2[:, B_DK:]
        c = r[0:chunk]
        e_in = jnp.exp(c)
        total = e_in[chunk - 1:chunk] if fwd else e_in[0:1]
        st = st_ref[...]
        inter = _dot_nt((q * e_in).astype(BF16), st.astype(BF16))
        ku = (k * jnp.exp((c[chunk - 1:chunk] if fwd else c[0:1]) - c)).astype(BF16)
        scores = jnp.where(ti == si, _dot_nt(q.astype(BF16), k.astype(BF16)), 0.0)
        for lev in range(n_lev):
            half = chunk >> (lev + 1)
            if half > GLA_FINE_HALF:
                c3 = c.reshape(chunk // (2 * half), 2 * half, B_DK)
                pivot = c3[:, half - 1:half, :] if fwd else c3[:, half:half + 1, :]
                f = jnp.exp(-jnp.abs(c3 - pivot)).reshape(chunk, B_DK)
            else:
                fine = (GLA_FINE_HALF // half).bit_length()
                f = jnp.exp(r[fine * chunk:(fine + 1) * chunk])
            sc = _dot_nt((q * f).astype(BF16), (k * f).astype(BF16))
            scores = jnp.where(order & (split >= half) & (split < 2 * half), sc, scores)
        acc_ref[rows, :] = inter + jnp.dot(scores.astype(BF16), vb, preferred_element_type=F32)
        st_ref[...] = st * total + _dot_tn(vb, ku)

    for s0_ref, st_ref in ((sf_ref, stf_ref), (sb_ref, stb_ref)):
        st_ref[...] = s0_ref[0, 0, 0].T if has_state else jnp.zeros(st_ref.shape, F32)

    def body(cc, carry):
        chunk_step(cc, True, af_ref, laf_ref, accf_ref, stf_ref)
        chunk_step(n_chunks - 1 - cc, False, ab_ref, lab_ref, accb_ref, stb_ref)
        return carry

    if n_chunks == 1:
        body(0, 0)
    else:
        lax.fori_loop(0, n_chunks, body, 0)
    if emit_state:
        nsf_ref[0, 0, 0] = stf_ref[...].T
        nsb_ref[0, 0, 0] = stb_ref[...].T

    def epilogue(r, carry):
        rows = pl.ds(pl.multiple_of(r * ROW_CHUNK, ROW_CHUNK), ROW_CHUNK)
        gate = r_ref[rows, :]
        o = accf_ref[rows, :] + accb_ref[rows, :]
        o_ref[rows, :] = ((_rms(o) * g_ref[...]) * (gate * jax.nn.sigmoid(gate))).astype(BF16)
        return carry

    lax.fori_loop(0, seq // ROW_CHUNK, epilogue, 0)


def gla_bidirectional(proj, gates, bnorm_g, *, row_off, batch, seq, states=None, emit_state=False):
    chunk = min(256, seq)
    assert row_off % seq == 0 and seq % chunk == 0 and chunk & (chunk - 1) == 0
    rb = lambda b: (row_off + b * seq) // seq
    q_off = (2 * A_QK_W + A_V_W) // B_DK
    k_off = q_off + B_HEADS
    v_off = (2 * A_QK_W + A_V_W + 2 * B_QK_W) // B_DV
    r_off = v_off + B_HEADS
    a_f, a_b = _gla_level_matrices(chunk, True), _gla_level_matrices(chunk, False)
    whole = lambda b, h: (0, 0)
    in_specs = [pl.BlockSpec(a_f.shape, whole), pl.BlockSpec(a_b.shape, whole),
                pl.BlockSpec((seq, B_DK), lambda b, h: (rb(b), q_off + h)),
                pl.BlockSpec((seq, B_DK), lambda b, h: (rb(b), k_off + h)),
                pl.BlockSpec((seq, B_DV), lambda b, h: (rb(b), v_off + h)),
                pl.BlockSpec((seq, B_DK), lambda b, h: (rb(b), h)),
                pl.BlockSpec((seq, B_DK), lambda b, h: (rb(b), B_HEADS + h)),
                pl.BlockSpec((seq, B_DV), lambda b, h: (rb(b), r_off + h)),
                pl.BlockSpec((1, B_DV), whole)]
    args = [a_f, a_b, proj, proj, proj, gates, gates, proj, bnorm_g.reshape(1, B_DV)]
    if states is not None:
        s_f, s_b, e = states
        sspec = pl.BlockSpec((1, 1, 1, B_DK, B_DV), lambda b, h: (b, e, h, 0, 0))
        in_specs += [sspec, sspec]
        args += [s_f, s_b]
    out_specs = [pl.BlockSpec((seq, B_DV), lambda b, h: (b, h))]
    out_shape = [jax.ShapeDtypeStruct((batch * seq, B_V_W), BF16)]
    if emit_state:
        nspec = pl.BlockSpec((1, 1, 1, B_DK, B_DV), lambda b, h: (b, 0, h, 0, 0))
        out_specs += [nspec, nspec]
        out_shape += [jax.ShapeDtypeStruct((batch, 1, B_HEADS, B_DK, B_DV), F32)] * 2
    return pl.pallas_call(
        functools.partial(_gla_kernel, chunk=chunk, has_state=states is not None, emit_state=emit_state),
        grid=(batch, B_HEADS),
        in_specs=in_specs, out_specs=out_specs, out_shape=out_shape,
        scratch_shapes=[pltpu.VMEM((seq, B_DV), F32)] * 2 + [pltpu.VMEM((B_DV, B_DK), F32)] * 2,
        compiler_params=_cparams("arbitrary", "arbitrary"),
        name="gla_state" if states is not None else "gla",
    )(*args)


def _sink_attn_kernel(sink_ref, q_ref, k_ref, v_ref, o_ref, nk_ref, nv_ref):
    seq = q_ref.shape[0]
    scale = C_DH ** -0.5
    scores, sinks, vals = [], [], []
    for hk in range(C_KV_HEADS):
        kcols = slice(hk * C_DH, (hk + 1) * C_DH)
        k, v = k_ref[:, kcols], v_ref[:, kcols]
        nk_ref[0, 0, hk] = k
        nv_ref[0, 0, hk] = v
        vals.append(v.T.astype(BF16))
        q4 = jnp.concatenate([q_ref[:, (hk * C_GROUP + g) * C_DH:(hk * C_GROUP + g + 1) * C_DH]
                              for g in range(C_GROUP)], axis=0).astype(BF16)
        scores.append(_dot_nt(k.astype(BF16), q4) * scale)
        sinks += [jnp.broadcast_to(sink_ref[hk, 0:1, g:g + 1], (1, seq)) for g in range(C_GROUP)]
    (p,) = _softmax_parts([jnp.concatenate(scores, axis=1)], extra_logit=jnp.concatenate(sinks, axis=1), axis=0)
    p = p.astype(BF16)
    cols_per_kv = C_GROUP * seq
    for hk in range(C_KV_HEADS):
        o = jnp.dot(vals[hk], p[:, hk * cols_per_kv:(hk + 1) * cols_per_kv], preferred_element_type=F32).T
        for g in range(C_GROUP):
            h = hk * C_GROUP + g
            o_ref[:, h * C_DH:(h + 1) * C_DH] = o[g * seq:(g + 1) * seq].astype(BF16)


def sink_attention_context(proj, sink, *, batch, seq):
    sink3 = sink.reshape(C_KV_HEADS, 1, C_GROUP)
    kvspec = pl.BlockSpec((1, 1, C_KV_HEADS, seq, C_DH), lambda b: (b, 0, 0, 0, 0))
    return pl.pallas_call(
        _sink_attn_kernel,
        grid=(batch,),
        in_specs=[pl.BlockSpec(sink3.shape, lambda b: (0, 0, 0)),
                  pl.BlockSpec((seq, C_Q_W), lambda b: (b, 0)),
                  pl.BlockSpec((seq, C_KV_W), lambda b: (b, C_Q_W // C_KV_W)),
                  pl.BlockSpec((seq, C_KV_W), lambda b: (b, C_Q_W // C_KV_W + 1))],
        out_specs=[pl.BlockSpec((seq, C_Q_W), lambda b: (b, 0)), kvspec, kvspec],
        out_shape=[jax.ShapeDtypeStruct((batch * seq, C_Q_W), BF16)]
        + [jax.ShapeDtypeStruct((batch, 1, C_KV_HEADS, seq, C_DH), F32)] * 2,
        compiler_params=_cparams("arbitrary"),
        name="sink_attn",
    )(sink3, proj, proj, proj)


def _window_attn_kernel(sink_ref, q_ref, k_ref, v_ref, ck_ref, cv_ref, o_ref, *, band, hkb):
    tq = q_ref.shape[0]
    seq = k_ref.shape[0]
    scale = C_DH ** -0.5
    i = pl.program_id(2)
    start = pl.multiple_of(jnp.clip(i * tq - C_WINDOW, 0, seq - band), C_WINDOW)
    kpos = start + lax.broadcasted_iota(jnp.int32, (band, tq), 0)
    qpos = i * tq + lax.broadcasted_iota(jnp.int32, (band, tq), 1)
    valid = jnp.abs(qpos - kpos) <= C_WINDOW
    valid4 = jnp.concatenate([valid] * C_GROUP, axis=1)
    gw = C_GROUP * C_DH
    s_ctx, s_band, sinks, vbs, cvs = [], [], [], [], []
    for hk in range(hkb):
        kcols = slice(hk * C_DH, (hk + 1) * C_DH)
        kb = k_ref[pl.ds(start, band), kcols].astype(BF16)
        vbs.append(v_ref[pl.ds(start, band), kcols].T.astype(BF16))
        cvs.append(cv_ref[0, 0, hk].T.astype(BF16))
        q4 = jnp.concatenate([q_ref[:, hk * gw + g * C_DH:hk * gw + (g + 1) * C_DH] for g in range(C_GROUP)],
                             axis=0).astype(BF16)
        sinks += [jnp.broadcast_to(sink_ref[hk, 0:1, g:g + 1], (1, tq)) for g in range(C_GROUP)]
        s_band.append(jnp.where(valid4, _dot_nt(kb, q4) * scale, -jnp.inf))
        s_ctx.append(_dot_nt(ck_ref[0, 0, hk].astype(BF16), q4) * scale)
    p_ctx, p_band = _softmax_parts([jnp.concatenate(s_ctx, axis=1), jnp.concatenate(s_band, axis=1)],
                                   extra_logit=jnp.concatenate(sinks, axis=1), axis=0)
    p_ctx, p_band = p_ctx.astype(BF16), p_band.astype(BF16)
    cols_per_kv = C_GROUP * tq
    for hk in range(hkb):
        cols = slice(hk * cols_per_kv, (hk + 1) * cols_per_kv)
        o = (jnp.dot(cvs[hk], p_ctx[:, cols], preferred_element_type=F32)
             + jnp.dot(vbs[hk], p_band[:, cols], preferred_element_type=F32)).T
        for g in range(C_GROUP):
            o_ref[:, hk * gw + g * C_DH:hk * gw + (g + 1) * C_DH] = o[g * tq:(g + 1) * tq].astype(BF16)


def window_attention_latent(proj, sink, ctx_k, ctx_v, e, *, row_off, batch, seq, tq=128, hkb=4):
    tq = min(tq, seq)
    band = min(tq + 2 * C_WINDOW, seq)
    assert row_off % seq == 0 and seq % tq == 0 and tq % C_WINDOW == 0 and C_KV_HEADS % hkb == 0
    past = ctx_k.shape[3]
    qw, kw = hkb * C_GROUP * C_DH, hkb * C_DH
    rb = lambda b: (row_off + b * seq) // seq
    cspec = pl.BlockSpec((1, 1, hkb, past, C_DH), lambda b, hg, i: (b, e, hg, 0, 0))
    sink3 = sink.reshape(C_KV_HEADS, 1, C_GROUP)
    return pl.pallas_call(
        functools.partial(_window_attn_kernel, band=band, hkb=hkb),
        grid=(batch, C_KV_HEADS // hkb, seq // tq),
        in_specs=[pl.BlockSpec((hkb, 1, C_GROUP), lambda b, hg, i: (hg, 0, 0)),
                  pl.BlockSpec((tq, qw), lambda b, hg, i: ((row_off + b * seq) // tq + i, hg)),
                  pl.BlockSpec((seq, kw), lambda b, hg, i: (rb(b), C_Q_W // kw + hg)),
                  pl.BlockSpec((seq, kw), lambda b, hg, i: (rb(b), (C_Q_W + C_KV_W) // kw + hg)),
                  cspec, cspec],
        out_specs=pl.BlockSpec((tq, qw), lambda b, hg, i: (b * (seq // tq) + i, hg)),
        out_shape=jax.ShapeDtypeStruct((batch * seq, C_Q_W), BF16),
        compiler_params=_cparams("arbitrary", "arbitrary", "arbitrary"),
        name="window_attn",
    )(sink3, proj, proj, proj, ctx_k, ctx_v)


def kernel(x_prompt, x_sample, cache_a_k, cache_a_v, state_b_fwd, state_b_bwd, cache_c_k, cache_c_v, c, c_ctx, ada_w, ada_b, norm_g, ffn_w_in, ffn_w_out, ab_w_in, ab_w_out, a_lambda, a_subln_g, b_alpha_w, b_alpha_b, b_norm_g, c_w_in, c_w_out, c_sink, final_g):
    bp, sp, d = x_prompt.shape
    bs, ss, _ = x_sample.shape
    depth = ada_w.shape[0]
    mp, ms = bp * sp, bs * ss
    rows_info = (mp, ss)

    cond = jnp.concatenate([c_ctx[None, :], c, jnp.zeros((COND_ROWS - 1 - bs, d), F32)], axis=0)
    mods = adaln(cond, ada_w, ada_b).reshape(depth, COND_ROWS, N_MOD, d)
    rope_a = _rope_tables(ss, A_DH, 2) + (2 * A_QK_W,)
    rope_c = _rope_tables(ss, C_DH, 1) + (C_Q_W + C_KV_W,)
    ab_w_in, ab_w_out, c_w_in, c_w_out = (w.astype(BF16) for w in (ab_w_in, ab_w_out, c_w_in, c_w_out))

    a_k, a_v, b_f, b_b, c_k, c_v = [], [], [], [], [], []
    xs = [x_prompt.reshape(mp, d), x_sample.reshape(ms, d)]
    for l in range(depth):
        mod = mods[l]
        ffn1 = functools.partial(ffn_half_step, mod3=mod[:, 0:3], g=norm_g[l, 0], w_in=ffn_w_in, w_out=ffn_w_out,
                                 widx=(l, 0), rows_info=rows_info)
        xs = [ffn1(xs[0])] if len(xs) == 1 else [ffn1(xs[0]), ffn1(xs[1], stream_row0=mp)]
        if l % 2 == 0:
            e = l // 2
            lam_init = 0.8 - 0.6 * math.exp(-0.3 * l)
            zeros = jnp.zeros((B_RANK, B_QK_W), F32)
            w_alpha = jnp.concatenate([jnp.concatenate([b_alpha_w[e, 0], zeros], axis=1),
                                       jnp.concatenate([zeros, b_alpha_w[e, 1]], axis=1)], axis=0)
            gate_w = (AB_MAIN, w_alpha, b_alpha_b[e].reshape(-1))
            proj, gates = mixer_in_proj(xs, mod[:, 3:5], norm_g[l, 1], ab_w_in, e, AB_MAIN, rows_info, rope_a,
                                        gates=gate_w)
            attn_p, ak, av = diff_attention(proj, a_lambda[e], a_subln_g[e], lam_init, row_off=0, batch=bp,
                                            seq=sp, tq=sp, hb=A_HEADS, emit_kv=True)
            (attn_s,) = diff_attention(proj, a_lambda[e], a_subln_g[e], lam_init, row_off=mp, batch=bs,
                                       seq=ss, tq=min(256, ss), hb=2, ctx=(cache_a_k, cache_a_v, e))
            gla_p, sf, sb = gla_bidirectional(proj, gates, b_norm_g[e], row_off=0, batch=bp, seq=sp,
                                              emit_state=True)
            (gla_s,) = gla_bidirectional(proj, gates, b_norm_g[e], row_off=mp, batch=bs, seq=ss,
                                         states=(state_b_fwd, state_b_bwd, e))
            a_k.append(ak), a_v.append(av), b_f.append(sf), b_b.append(sb)
            x = mixer_out_proj([attn_p, gla_p], [attn_s, gla_s], ab_w_out, e, xs, mod[:, 5:6], rows_info)
        else:
            o = l // 2
            proj = mixer_in_proj(xs, mod[:, 3:5], norm_g[l, 1], c_w_in, o, c_w_in.shape[2], rows_info, rope_c)
            mix_p, ck, cv = sink_attention_context(proj, c_sink[o], batch=bp, seq=sp)
            mix_s = window_attention_latent(proj, c_sink[o], cache_c_k, cache_c_v, o, row_off=mp, batch=bs, seq=ss)
            c_k.append(ck), c_v.append(cv)
            x = mixer_out_proj([mix_p], [mix_s], c_w_out, o, xs, mod[:, 5:6], rows_info)
        ffn2 = functools.partial(ffn_half_step, x, mod[:, 6:9], norm_g[l, 2], ffn_w_in, ffn_w_out, (l, 1), rows_info)
        if l < depth - 1:
            xs = [ffn2()]

    y_prompt = ffn2(n_rows=mp, final_g=final_g).reshape(bp, sp, d)
    y_sample = ffn2(stream_row0=mp, x_row0=mp, n_rows=ms, final_g=final_g).reshape(bs, ss, d)
    cat = lambda parts: parts[0] if len(parts) == 1 else jnp.concatenate(parts, axis=1)
    return (y_prompt, y_sample, cat(a_k), cat(a_v), cat(b_f), cat(b_b), cat(c_k), cat(c_v))
```

```python
import functools
import math

import numpy as np
import jax
import jax.numpy as jnp
from jax import lax
from jax.experimental import pallas as pl
from jax.experimental.pallas import tpu as pltpu

F32 = jnp.float32
BF16 = jnp.bfloat16

EPS = 1e-6
GRID_W = 64
ROPE_BASE = 10000.0
N_MOD = 9
A_HEADS, A_DH, A_DV = 8, 64, 128
B_HEADS, B_DK, B_DV, B_RANK, B_TAU = 4, 128, 256, 16, 16.0
C_HEADS, C_KV_HEADS, C_DH, C_WINDOW = 16, 4, 128, 128
A_QK_W = A_HEADS * 2 * A_DH
A_V_W = A_HEADS * A_DV
B_QK_W = B_HEADS * B_DK
B_V_W = B_HEADS * B_DV
AB_MAIN = 2 * A_QK_W + A_V_W + 2 * B_QK_W + 2 * B_V_W
C_GROUP = C_HEADS // C_KV_HEADS
C_Q_W = C_HEADS * C_DH
C_KV_W = C_KV_HEADS * C_DH

V7X_VMEM_BYTES = 64 * 1024 * 1024
VMEM_LIMIT_BYTES = V7X_VMEM_BYTES - 4 * 1024 * 1024
COND_ROWS = 8
ROW_CHUNK = 128


def _tile(n, pref, align=128):
    if n <= pref:
        return n
    t = (pref // align) * align
    while n % t:
        t -= align
    assert t > 0, (n, pref)
    return t


def _cparams(*sem):
    return pltpu.CompilerParams(dimension_semantics=sem, vmem_limit_bytes=VMEM_LIMIT_BYTES)


def _bdot(a, b):
    return jnp.dot(a.astype(BF16), b.astype(BF16), preferred_element_type=F32)


def _dot_nt(a, b):
    return lax.dot_general(a, b, (((1,), (1,)), ((), ())), preferred_element_type=F32)


def _dot_tn(a, b):
    return lax.dot_general(a, b, (((0,), (0,)), ((), ())), preferred_element_type=F32)


def _rms(x):
    return x * lax.rsqrt(jnp.mean(x * x, axis=-1, keepdims=True) + EPS)


def _adaln_kernel(cond_ref, w_ref, b_ref, o_ref):
    c = cond_ref[...]
    o_ref[0] = _bdot(c * jax.nn.sigmoid(c), w_ref[0]) + b_ref[0]


def adaln(cond, ada_w, ada_b):
    depth, d, n = ada_w.shape
    tn = _tile(n, 1024)
    return pl.pallas_call(
        _adaln_kernel,
        grid=(depth, n // tn),
        in_specs=[pl.BlockSpec((COND_ROWS, d), lambda l, j: (0, 0)),
                  pl.BlockSpec((1, d, tn), lambda l, j: (l, 0, j)),
                  pl.BlockSpec((1, 1, tn), lambda l, j: (l, 0, j))],
        out_specs=pl.BlockSpec((1, COND_ROWS, tn), lambda l, j: (l, 0, j)),
        out_shape=jax.ShapeDtypeStruct((depth, COND_ROWS, n), F32),
        compiler_params=_cparams("arbitrary", "arbitrary"),
        name="adaln",
    )(cond, ada_w, ada_b.reshape(depth, 1, n))


def _norm_modulate(x_ref, g_ref, mod_ref, h_ref):
    g = g_ref[...]
    shift = mod_ref[0, 0:1, :]
    scale1 = 1.0 + mod_ref[0, 1:2, :]

    def body(r, carry):
        rows = pl.ds(pl.multiple_of(r * ROW_CHUNK, ROW_CHUNK), ROW_CHUNK)
        h_ref[rows, :] = ((_rms(x_ref[rows, :]) * g) * scale1 + shift).astype(BF16)
        return carry

    lax.fori_loop(0, x_ref.shape[0] // ROW_CHUNK, body, 0)


def _row_tile(rows_info):
    n_prompt_rows, lat_rows = rows_info
    return _tile(math.gcd(n_prompt_rows, lat_rows), 1024, align=ROW_CHUNK)


def _cond_row(i, tm, n_prompt_rows, lat_rows):
    return jnp.maximum((i * tm - n_prompt_rows) // lat_rows + 1, 0)


FFN_W_SLOTS = 3


def _ffn_kernel(x_ref, mod_ref, g_ref, win_hbm, wout_hbm, *rest, widx, row_split, n_chunk, final):
    if final:
        fg_ref, rest = rest[0], rest[1:]
    o_ref, h_ref, wg_buf, wu_buf, wo_buf, sem = rest
    i, j = pl.program_id(0), pl.program_id(1)
    nf = pl.num_programs(1)
    step, n_steps = i * nf + j, pl.num_programs(0) * nf
    tf = wo_buf.shape[1]
    l, k = widx

    def tile_copies(s):
        jt, slot = s % nf, s % FFN_W_SLOTS
        gate = pl.multiple_of(jt * tf, tf)
        up = pl.multiple_of((nf + jt) * tf, tf)
        return (pltpu.make_async_copy(win_hbm.at[l, k, :, pl.ds(gate, tf)], wg_buf.at[slot], sem.at[0, slot]),
                pltpu.make_async_copy(win_hbm.at[l, k, :, pl.ds(up, tf)], wu_buf.at[slot], sem.at[1, slot]),
                pltpu.make_async_copy(wout_hbm.at[l, k, pl.ds(gate, tf), :], wo_buf.at[slot], sem.at[2, slot]))

    def start(s):
        for c in tile_copies(s):
            c.start()

    @pl.when(step == 0)
    def _():
        start(0)
        start(1)

    @pl.when(step + 2 < n_steps)
    def _():
        start(step + 2)

    @pl.when(j == 0)
    def _():
        _norm_modulate(x_ref, g_ref, mod_ref, h_ref)
        o_ref[...] = jnp.zeros(o_ref.shape, F32)

    for c in tile_copies(step):
        c.wait()
    slot = step % FFN_W_SLOTS
    tm, d = h_ref.shape
    rt = tm // row_split
    wg, wu, wo = (buf[slot].astype(BF16) for buf in (wg_buf, wu_buf, wo_buf))
    for r in range(row_split):
        rows = slice(r * rt, (r + 1) * rt)
        h = h_ref[rows, :]
        gate = jnp.dot(h, wg, preferred_element_type=F32)
        up = jnp.dot(h, wu, preferred_element_type=F32)
        act = (gate * jax.nn.sigmoid(gate) * up).astype(BF16)
        for c in range(d // n_chunk):
            cols = slice(c * n_chunk, (c + 1) * n_chunk)
            o_ref[rows, cols] += jnp.dot(act, wo[:, cols], preferred_element_type=F32)

    @pl.when(j == pl.num_programs(1) - 1)
    def _():
        half_gate = 0.5 * mod_ref[0, 2:3, :]

        def body(r, carry):
            rows = pl.ds(pl.multiple_of(r * ROW_CHUNK, ROW_CHUNK), ROW_CHUNK)
            xn = x_ref[rows, :] + half_gate * o_ref[rows, :]
            if final:
                xn = _rms(xn) * fg_ref[...]
            o_ref[rows, :] = xn
            return carry

        lax.fori_loop(0, tm // ROW_CHUNK, body, 0)


def ffn_half_step(x, mod3, g, w_in, w_out, widx, rows_info, *, stream_row0=0, x_row0=0, n_rows=None,
                  final_g=None):
    d = x.shape[1]
    f = w_out.shape[-2]
    n_prompt_rows, lat_rows = rows_info
    tm, tf, n_chunk = _row_tile(rows_info), _tile(f, 256), _tile(d, 512)
    m = x.shape[0] if n_rows is None else n_rows
    assert x_row0 % tm == 0 and stream_row0 % tm == 0 and m % tm == 0
    tx, ts = x_row0 // tm, stream_row0 // tm
    nf = f // tf
    final = final_g is not None
    l, k = widx
    in_specs = [
        pl.BlockSpec((tm, d), lambda i, j: (tx + i, 0)),
        pl.BlockSpec((1, 3, d), lambda i, j: (_cond_row(ts + i, tm, n_prompt_rows, lat_rows), 0, 0)),
        pl.BlockSpec((1, d), lambda i, j: (0, 0)),
        pl.BlockSpec(memory_space=pl.ANY),
        pl.BlockSpec(memory_space=pl.ANY)]
    assert (m // tm) * nf >= 2
    args = [x, mod3, g.reshape(1, d), w_in, w_out]
    if final:
        in_specs.append(pl.BlockSpec((1, d), lambda i, j: (0, 0)))
        args.append(final_g.reshape(1, d))
    return pl.pallas_call(
        functools.partial(_ffn_kernel, widx=widx, row_split=max(1, tm // 512), n_chunk=n_chunk, final=final),
        grid=(m // tm, nf),
        in_specs=in_specs,
        out_specs=pl.BlockSpec((tm, d), lambda i, j: (i, 0)),
        out_shape=jax.ShapeDtypeStruct((m, d), F32),
        scratch_shapes=[pltpu.VMEM((tm, d), BF16),
                        pltpu.VMEM((FFN_W_SLOTS, d, tf), F32), pltpu.VMEM((FFN_W_SLOTS, d, tf), F32),
                        pltpu.VMEM((FFN_W_SLOTS, tf, d), F32), pltpu.SemaphoreType.DMA((3, FFN_W_SLOTS))],
        compiler_params=_cparams("arbitrary", "arbitrary"),
        name="ffn_final" if final else "ffn",
    )(*args)


def _log_sigmoid(z):
    return jnp.minimum(z, 0.0) - jnp.log(1.0 + jnp.exp(-jnp.abs(z)))


def _proj_kernel(*refs, n_x, n_prompt_tiles, n_rope_cols, rope_shift, tn, with_gates):
    x_refs, refs = refs[:n_x], refs[n_x:]
    mod_ref, g_ref, w_ref, cos_ref, sa_ref, sb_ref = refs[:6]
    if with_gates:
        wl_ref, wa_ref, ba_ref, o_ref, gates_ref = refs[6:]
    else:
        (o_ref,) = refs[6:]

    def project(x_ref, rotary):
        x = x_ref[...]
        h = (((_rms(x) * g_ref[...]) * (1.0 + mod_ref[0, 1:2, :])) + mod_ref[0, 0:1, :]).astype(BF16)
        if with_gates:
            low = jnp.dot(h, wl_ref[...], preferred_element_type=F32)
            z = jnp.dot(low.astype(BF16), wa_ref[...].astype(BF16), preferred_element_type=F32) + ba_ref[...]
            gates_ref[...] = _log_sigmoid(z) * (1.0 / B_TAU)
        for c in range(o_ref.shape[1] // tn):
            acc = jnp.dot(h, w_ref[:, c * tn:(c + 1) * tn], preferred_element_type=F32)
            if rotary and c * tn < n_rope_cols:
                cos, sa, sb = cos_ref[...], sa_ref[...], sb_ref[...]
                for s in range(tn // 128):
                    a = acc[:, s * 128:(s + 1) * 128]
                    o_ref[:, c * tn + s * 128:c * tn + (s + 1) * 128] = (
                        a * cos + pltpu.roll(a, 128 - rope_shift, 1) * sa + pltpu.roll(a, rope_shift, 1) * sb)
            else:
                o_ref[:, c * tn:(c + 1) * tn] = acc

    i = pl.program_id(0)
    pl.when(i < n_prompt_tiles)(lambda: project(x_refs[0], False))
    pl.when(i >= n_prompt_tiles)(lambda: project(x_refs[-1], True))


def _rope_tables(n, head_dim, n_sub):
    assert head_dim * n_sub == 128
    rows = n // GRID_W
    row = jnp.repeat(jnp.arange(rows, dtype=F32), GRID_W)
    col = jnp.tile(jnp.arange(GRID_W, dtype=F32), rows)
    d_axis = head_dim // 2
    shift = d_axis // 2
    inv = ROPE_BASE ** (-jnp.arange(0, d_axis, 2, dtype=F32) / d_axis)
    lane = jnp.arange(128)
    sub = lane % head_dim
    is_col = (sub // d_axis) == 1
    within = sub % d_axis
    freq = within % shift
    second = (within // shift) == 1
    pos = jnp.where(is_col[None, :], col[:, None], row[:, None])
    ang = pos * inv[freq][None, :]
    cos, sin = jnp.cos(ang), jnp.sin(ang)
    sin_a = jnp.where(second[None, :], 0.0, -sin)
    sin_b = jnp.where(second[None, :], sin, 0.0)
    return cos, sin_a, sin_b, shift


def _stream_specs(xs, tm, npt):
    d = xs[0].shape[1]
    if len(xs) == 1:
        return [pl.BlockSpec((tm, d), lambda i: (i, 0))]
    return [pl.BlockSpec((tm, d), lambda i: (jnp.minimum(i, npt - 1), 0)),
            pl.BlockSpec((tm, d), lambda i: (jnp.maximum(i - npt, 0), 0))]


def mixer_in_proj(xs, mod2, g, w, e, n_cols, rows_info, rope, gates=None):
    m, d = sum(x.shape[0] for x in xs), xs[0].shape[1]
    n_prompt_rows, lat_rows = rows_info
    cos, sin_a, sin_b, rope_shift, n_rope_cols = rope
    tm = _tile(math.gcd(n_prompt_rows, lat_rows), 256, align=ROW_CHUNK)
    tn = _tile(math.gcd(n_cols, n_rope_cols), 512)
    npt, lat_tiles = n_prompt_rows // tm, lat_rows // tm
    tab = lambda i: (jnp.maximum(i - npt, 0) % lat_tiles, 0)
    once = dict(pipeline_mode=pl.Buffered(1))
    in_specs = _stream_specs(xs, tm, npt) + [
        pl.BlockSpec((1, 2, d), lambda i: (_cond_row(i, tm, n_prompt_rows, lat_rows), 0, 0)),
        pl.BlockSpec((1, d), lambda i: (0, 0)),
        pl.BlockSpec((None, d, n_cols), lambda i: (e, 0, 0), **once),
        pl.BlockSpec((tm, 128), tab), pl.BlockSpec((tm, 128), tab), pl.BlockSpec((tm, 128), tab)]
    args = list(xs) + [mod2, g.reshape(1, d), w, cos, sin_a, sin_b]
    out_specs = pl.BlockSpec((tm, n_cols), lambda i: (i, 0))
    out_shape = jax.ShapeDtypeStruct((m, n_cols), F32)
    if gates is not None:
        w_low, w_alpha, b_alpha = gates
        n_gate = w_alpha.shape[1]
        in_specs += [pl.BlockSpec(w_low.shape, lambda i: (0, 0)),
                     pl.BlockSpec(w_alpha.shape, lambda i: (0, 0)),
                     pl.BlockSpec((1, n_gate), lambda i: (0, 0))]
        args += [w_low, w_alpha, b_alpha.reshape(1, n_gate)]
        out_specs = [out_specs, pl.BlockSpec((tm, n_gate), lambda i: (i, 0))]
        out_shape = [out_shape, jax.ShapeDtypeStruct((m, n_gate), F32)]
    return pl.pallas_call(
        functools.partial(_proj_kernel, n_x=len(xs), n_prompt_tiles=npt, n_rope_cols=n_rope_cols,
                          rope_shift=rope_shift, tn=tn, with_gates=gates is not None),
        grid=(m // tm,),
        in_specs=in_specs, out_specs=out_specs, out_shape=out_shape,
        compiler_params=_cparams("arbitrary"),
        name="mixer_in_proj_gated" if gates is not None else "mixer_in_proj",
    )(*args)


def _out_proj_kernel(*refs, n_parts, n_x, n_prompt_tiles, tn):
    parts_p, parts_s = refs[:n_parts], refs[n_parts:2 * n_parts]
    w_refs = refs[2 * n_parts:3 * n_parts]
    x_refs = refs[3 * n_parts:3 * n_parts + n_x]
    gate_ref, o_ref = refs[3 * n_parts + n_x:]
    i = pl.program_id(0)

    def run(parts, x_ref):
        for c in range(o_ref.shape[1] // tn):
            cols = slice(c * tn, (c + 1) * tn)
            acc = None
            for p_ref, w_ref in zip(parts, w_refs):
                t = jnp.dot(p_ref[...], w_ref[:, cols], preferred_element_type=F32)
                acc = t if acc is None else acc + t
            o_ref[:, cols] = x_ref[:, cols] + gate_ref[0, :, cols] * acc

    pl.when(i < n_prompt_tiles)(lambda: run(parts_p, x_refs[0]))
    pl.when(i >= n_prompt_tiles)(lambda: run(parts_s, x_refs[-1]))


def mixer_out_proj(parts_p, parts_s, w, e, xs, gate, rows_info):
    m, d = sum(x.shape[0] for x in xs), xs[0].shape[1]
    n_prompt_rows, lat_rows = rows_info
    tm = _tile(math.gcd(n_prompt_rows, lat_rows), 512, align=ROW_CHUNK)
    npt = n_prompt_rows // tm
    n_parts = len(parts_p)
    in_specs, w_specs, off = [], [], 0
    for p in parts_p:
        in_specs.append(pl.BlockSpec((tm, p.shape[1]), lambda i: (jnp.minimum(i, npt - 1), 0)))
    for p in parts_s:
        kp = p.shape[1]
        in_specs.append(pl.BlockSpec((tm, kp), lambda i: (jnp.maximum(i - npt, 0), 0)))
        assert off % kp == 0
        w_specs.append(pl.BlockSpec((None, kp, d), functools.partial(lambda i, rb: (e, rb, 0), rb=off // kp),
                                    pipeline_mode=pl.Buffered(1)))
        off += kp
    in_specs += w_specs
    in_specs += _stream_specs(xs, tm, npt)
    in_specs += [pl.BlockSpec((1, 1, d), lambda i: (_cond_row(i, tm, n_prompt_rows, lat_rows), 0, 0))]
    return pl.pallas_call(
        functools.partial(_out_proj_kernel, n_parts=n_parts, n_x=len(xs), n_prompt_tiles=npt, tn=_tile(d, 512)),
        grid=(m // tm,),
        in_specs=in_specs,
        out_specs=pl.BlockSpec((tm, d), lambda i: (i, 0)),
        out_shape=jax.ShapeDtypeStruct((m, d), F32),
        compiler_params=_cparams("arbitrary"),
        name="mixer_out_proj",
    )(*parts_p, *parts_s, *([w] * n_parts), *xs, gate)


def _softmax_parts(parts, extra_logit=None, weight=None, axis=-1):
    m = functools.reduce(jnp.maximum, [jnp.max(p, axis=axis, keepdims=True) for p in parts])
    if extra_logit is not None:
        m = jnp.maximum(m, extra_logit)
    es = [jnp.exp(p - m) for p in parts]
    den = functools.reduce(jnp.add, [jnp.sum(e, axis=axis, keepdims=True) for e in es])
    if extra_logit is not None:
        den = den + jnp.exp(extra_logit - m)
    inv = 1.0 / den if weight is None else weight / den
    return [e * inv for e in es]


def _diff_attn_kernel(*refs, hb, lam_init, has_ctx, emit_kv):
    it = iter(refs)
    lam_ref, g_ref, q_ref, k_ref, v_ref = (next(it) for _ in range(5))
    ck_ref, cv_ref = (next(it), next(it)) if has_ctx else (None, None)
    o_ref = next(it)
    nk_ref, nv_ref = (next(it), next(it)) if emit_kv else (None, None)

    lp = lam_ref[...]
    lam = (jnp.exp(jnp.sum(lp[0:1] * lp[1:2], axis=-1, keepdims=True))
           - jnp.exp(jnp.sum(lp[2:3] * lp[3:4], axis=-1, keepdims=True)) + lam_init)
    scale = A_DH ** -0.5
    assert math.frexp(scale)[0] == 0.5
    first = lax.broadcasted_iota(jnp.int32, (1, 2 * A_DH), 1) < A_DH
    tq = q_ref.shape[0]
    heads = [slice(hh * 128, (hh + 1) * 128) for hh in range(hb)]
    q1, q2, keys, vals = [], [], [], []
    for hh, cols in enumerate(heads):
        q, k, v = q_ref[:, cols] * scale, k_ref[:, cols], v_ref[:, cols]
        q1.append(jnp.where(first, q, 0.0).astype(BF16))
        q2.append(jnp.where(first, 0.0, q).astype(BF16))
        keys.append(([ck_ref[0, 0, hh].astype(BF16)] if has_ctx else []) + [k.astype(BF16)])
        vals.append(([cv_ref[0, 0, hh].astype(BF16)] if has_ctx else []) + [v.astype(BF16)])
        if emit_kv:
            nk_ref[0, 0, hh] = k
            nv_ref[0, 0, hh] = v
    parts = range(len(keys[0]))
    stack = lambda qs: [jnp.concatenate([_dot_nt(qs[hh], keys[hh][p]) for hh in range(hb)], axis=0) for p in parts]
    p1 = _softmax_parts(stack(q1))
    p2 = _softmax_parts(stack(q2), weight=lam)
    pd = [(a - b).astype(BF16) for a, b in zip(p1, p2)]
    outs = []
    for hh in range(hb):
        o = None
        for p in parts:
            t = jnp.dot(pd[p][hh * tq:(hh + 1) * tq], vals[hh][p], preferred_element_type=F32)
            o = t if o is None else o + t
        outs.append(o)
    y = ((_rms(jnp.concatenate(outs, axis=0)) * g_ref[...]) * (1.0 - lam_init)).astype(BF16)
    for hh, cols in enumerate(heads):
        o_ref[:, cols] = y[hh * tq:(hh + 1) * tq]


def diff_attention(proj, lam_p, subln_g, lam_init, *, row_off, batch, seq, tq, hb, ctx=None, emit_kv=False):
    width = hb * 128
    n_hg = A_HEADS // hb
    assert row_off % seq == 0 and seq % tq == 0
    qrow = lambda b, hg, i: ((row_off + b * seq) // tq + i, hg)
    krow = lambda b, hg, i: ((row_off + b * seq) // seq, A_QK_W // width + hg)
    vrow = lambda b, hg, i: ((row_off + b * seq) // seq, 2 * A_QK_W // width + hg)
    in_specs = [pl.BlockSpec(lam_p.shape, lambda b, hg, i: (0, 0)),
                pl.BlockSpec((1, A_DV), lambda b, hg, i: (0, 0)),
                pl.BlockSpec((tq, width), qrow),
                pl.BlockSpec((seq, width), krow),
                pl.BlockSpec((seq, width), vrow)]
    args = [lam_p, subln_g.reshape(1, A_DV), proj, proj, proj]
    if ctx is not None:
        ctx_k, ctx_v, e = ctx
        past = ctx_k.shape[3]
        cspec = lambda b, hg, i: (b, e, hg, 0, 0)
        in_specs += [pl.BlockSpec((1, 1, hb, past, 2 * A_DH), cspec), pl.BlockSpec((1, 1, hb, past, A_DV), cspec)]
        args += [ctx_k, ctx_v]
    out_specs = [pl.BlockSpec((tq, width), lambda b, hg, i: (b * (seq // tq) + i, hg))]
    out_shape = [jax.ShapeDtypeStruct((batch * seq, A_V_W), BF16)]
    if emit_kv:
        assert tq == seq
        kvspec = pl.BlockSpec((1, 1, hb, seq, 128), lambda b, hg, i: (b, 0, hg, 0, 0))
        out_specs += [kvspec, kvspec]
        out_shape += [jax.ShapeDtypeStruct((batch, 1, A_HEADS, seq, 128), F32)] * 2
    return pl.pallas_call(
        functools.partial(_diff_attn_kernel, hb=hb, lam_init=lam_init, has_ctx=ctx is not None, emit_kv=emit_kv),
        grid=(batch, n_hg, seq // tq),
        in_specs=in_specs, out_specs=out_specs, out_shape=out_shape,
        compiler_params=_cparams("arbitrary", "arbitrary", "arbitrary"),
        name="diff_attn_ctx" if ctx is not None else "diff_attn",
    )(*args)


GLA_FINE_HALF = 2


def _gla_level_matrices(chunk, fwd):
    t = np.arange(chunk)[:, None]
    i = np.arange(chunk)[None, :]
    mats = [i <= t] if fwd else [i >= t]
    h = GLA_FINE_HALF
    while h >= 1:
        p = t % (2 * h)
        base = t - p
        if fwd:
            m = base + h - 1
            a = np.where(p >= h, (i > m) & (i <= t), (i > t) & (i <= m))
        else:
            m = base + h
            a = np.where(p < h, (i >= t) & (i < m), (i >= m) & (i < t))
        mats.append(a)
        h //= 2
    return jnp.asarray(np.concatenate(mats, axis=0).astype(np.float32), dtype=BF16)


def _gla_kernel(*refs, chunk, has_state, emit_state):
    it = iter(refs)
    af_ref, ab_ref, q_ref, k_ref, v_ref, laf_ref, lab_ref, r_ref, g_ref = (next(it) for _ in range(9))
    sf_ref, sb_ref = (next(it), next(it)) if has_state else (None, None)
    o_ref = next(it)
    nsf_ref, nsb_ref = (next(it), next(it)) if emit_state else (None, None)
    accf_ref, accb_ref, stf_ref, stb_ref = (next(it) for _ in range(4))

    seq = q_ref.shape[0]
    n_chunks = seq // chunk
    n_lev = chunk.bit_length() - 1
    ti = lax.broadcasted_iota(jnp.int32, (chunk, chunk), 0)
    si = lax.broadcasted_iota(jnp.int32, (chunk, chunk), 1)
    split = ti ^ si
    scale = B_DK ** -0.5

    def chunk_step(c, fwd, a_ref, la_ref, acc_ref, st_ref):
        rows = c * chunk if isinstance(c, int) else pl.multiple_of(c * chunk, chunk)
        rows = pl.ds(rows, chunk)
        order = (ti > si) if fwd else (ti < si)
        q = q_ref[rows, :] * scale
        k = k_ref[rows, :]
        vb = v_ref[rows, :].astype(BF16)
        la = la_ref[rows, :]
        la_hi = la.astype(BF16)
        la_lo = (la - la_hi.astype(F32)).astype(BF16)
        r2 = jnp.dot(a_ref[...], jnp.concatenate([la_hi, la_lo], axis=1), preferred_element_type=F32)
        r = r2[:, :B_DK] + r2[:, B_DK:]
        c = r[0:chunk]
        e_in = jnp.exp(c)
        total = e_in[chunk - 1:chunk] if fwd else e_in[0:1]
        st = st_ref[...]
        inter = _dot_nt((q * e_in).astype(BF16), st.astype(BF16))
        ku = (k * jnp.exp((c[chunk - 1:chunk] if fwd else c[0:1]) - c)).astype(BF16)
        scores = jnp.where(ti == si, _dot_nt(q.astype(BF16), k.astype(BF16)), 0.0)
        for lev in range(n_lev):
            half = chunk >> (lev + 1)
            if half > GLA_FINE_HALF:
                c3 = c.reshape(chunk // (2 * half), 2 * half, B_DK)
                pivot = c3[:, half - 1:half, :] if fwd else c3[:, half:half + 1, :]
                f = jnp.exp(-jnp.abs(c3 - pivot)).reshape(chunk, B_DK)
            else:
                fine = (GLA_FINE_HALF // half).bit_length()
                f = jnp.exp(r[fine * chunk:(fine + 1) * chunk])
            sc = _dot_nt((q * f).astype(BF16), (k * f).astype(BF16))
            scores = jnp.where(order & (split >= half) & (split < 2 * half), sc, scores)
        acc_ref[rows, :] = inter + jnp.dot(scores.astype(BF16), vb, preferred_element_type=F32)
        st_ref[...] = st * total + _dot_tn(vb, ku)

    for s0_ref, st_ref in ((sf_ref, stf_ref), (sb_ref, stb_ref)):
        st_ref[...] = s0_ref[0, 0, 0].T if has_state else jnp.zeros(st_ref.shape, F32)

    def body(cc, carry):
        chunk_step(cc, True, af_ref, laf_ref, accf_ref, stf_ref)
        chunk_step(n_chunks - 1 - cc, False, ab_ref, lab_ref, accb_ref, stb_ref)
        return carry

    if n_chunks == 1:
        body(0, 0)
    else:
        lax.fori_loop(0, n_chunks, body, 0)
    if emit_state:
        nsf_ref[0, 0, 0] = stf_ref[...].T
        nsb_ref[0, 0, 0] = stb_ref[...].T

    def epilogue(r, carry):
        rows = pl.ds(pl.multiple_of(r * ROW_CHUNK, ROW_CHUNK), ROW_CHUNK)
        gate = r_ref[rows, :]
        o = accf_ref[rows, :] + accb_ref[rows, :]
        o_ref[rows, :] = ((_rms(o) * g_ref[...]) * (gate * jax.nn.sigmoid(gate))).astype(BF16)
        return carry

    lax.fori_loop(0, seq // ROW_CHUNK, epilogue, 0)


def gla_bidirectional(proj, gates, bnorm_g, *, row_off, batch, seq, states=None, emit_state=False):
    chunk = min(256, seq)
    assert row_off % seq == 0 and seq % chunk == 0 and chunk & (chunk - 1) == 0
    rb = lambda b: (row_off + b * seq) // seq
    q_off = (2 * A_QK_W + A_V_W) // B_DK
    k_off = q_off + B_HEADS
    v_off = (2 * A_QK_W + A_V_W + 2 * B_QK_W) // B_DV
    r_off = v_off + B_HEADS
    a_f, a_b = _gla_level_matrices(chunk, True), _gla_level_matrices(chunk, False)
    whole = lambda b, h: (0, 0)
    in_specs = [pl.BlockSpec(a_f.shape, whole), pl.BlockSpec(a_b.shape, whole),
                pl.BlockSpec((seq, B_DK), lambda b, h: (rb(b), q_off + h)),
                pl.BlockSpec((seq, B_DK), lambda b, h: (rb(b), k_off + h)),
                pl.BlockSpec((seq, B_DV), lambda b, h: (rb(b), v_off + h)),
                pl.BlockSpec((seq, B_DK), lambda b, h: (rb(b), h)),
                pl.BlockSpec((seq, B_DK), lambda b, h: (rb(b), B_HEADS + h)),
                pl.BlockSpec((seq, B_DV), lambda b, h: (rb(b), r_off + h)),
                pl.BlockSpec((1, B_DV), whole)]
    args = [a_f, a_b, proj, proj, proj, gates, gates, proj, bnorm_g.reshape(1, B_DV)]
    if states is not None:
        s_f, s_b, e = states
        sspec = pl.BlockSpec((1, 1, 1, B_DK, B_DV), lambda b, h: (b, e, h, 0, 0))
        in_specs += [sspec, sspec]
        args += [s_f, s_b]
    out_specs = [pl.BlockSpec((seq, B_DV), lambda b, h: (b, h))]
    out_shape = [jax.ShapeDtypeStruct((batch * seq, B_V_W), BF16)]
    if emit_state:
        nspec = pl.BlockSpec((1, 1, 1, B_DK, B_DV), lambda b, h: (b, 0, h, 0, 0))
        out_specs += [nspec, nspec]
        out_shape += [jax.ShapeDtypeStruct((batch, 1, B_HEADS, B_DK, B_DV), F32)] * 2
    return pl.pallas_call(
        functools.partial(_gla_kernel, chunk=chunk, has_state=states is not None, emit_state=emit_state),
        grid=(batch, B_HEADS),
        in_specs=in_specs, out_specs=out_specs, out_shape=out_shape,
        scratch_shapes=[pltpu.VMEM((seq, B_DV), F32)] * 2 + [pltpu.VMEM((B_DV, B_DK), F32)] * 2,
        compiler_params=_cparams("arbitrary", "arbitrary"),
        name="gla_state" if states is not None else "gla",
    )(*args)


def _sink_attn_kernel(sink_ref, q_ref, k_ref, v_ref, o_ref, nk_ref, nv_ref):
    seq = q_ref.shape[0]
    scale = C_DH ** -0.5
    scores, sinks, vals = [], [], []
    for hk in range(C_KV_HEADS):
        kcols = slice(hk * C_DH, (hk + 1) * C_DH)
        k, v = k_ref[:, kcols], v_ref[:, kcols]
        nk_ref[0, 0, hk] = k
        nv_ref[0, 0, hk] = v
        vals.append(v.T.astype(BF16))
        q4 = jnp.concatenate([q_ref[:, (hk * C_GROUP + g) * C_DH:(hk * C_GROUP + g + 1) * C_DH]
                              for g in range(C_GROUP)], axis=0).astype(BF16)
        scores.append(_dot_nt(k.astype(BF16), q4) * scale)
        sinks += [jnp.broadcast_to(sink_ref[hk, 0:1, g:g + 1], (1, seq)) for g in range(C_GROUP)]
    (p,) = _softmax_parts([jnp.concatenate(scores, axis=1)], extra_logit=jnp.concatenate(sinks, axis=1), axis=0)
    p = p.astype(BF16)
    cols_per_kv = C_GROUP * seq
    for hk in range(C_KV_HEADS):
        o = jnp.dot(vals[hk], p[:, hk * cols_per_kv:(hk + 1) * cols_per_kv], preferred_element_type=F32).T
        for g in range(C_GROUP):
            h = hk * C_GROUP + g
            o_ref[:, h * C_DH:(h + 1) * C_DH] = o[g * seq:(g + 1) * seq].astype(BF16)


def sink_attention_context(proj, sink, *, batch, seq):
    sink3 = sink.reshape(C_KV_HEADS, 1, C_GROUP)
    kvspec = pl.BlockSpec((1, 1, C_KV_HEADS, seq, C_DH), lambda b: (b, 0, 0, 0, 0))
    return pl.pallas_call(
        _sink_attn_kernel,
        grid=(batch,),
        in_specs=[pl.BlockSpec(sink3.shape, lambda b: (0, 0, 0)),
                  pl.BlockSpec((seq, C_Q_W), lambda b: (b, 0)),
                  pl.BlockSpec((seq, C_KV_W), lambda b: (b, C_Q_W // C_KV_W)),
                  pl.BlockSpec((seq, C_KV_W), lambda b: (b, C_Q_W // C_KV_W + 1))],
        out_specs=[pl.BlockSpec((seq, C_Q_W), lambda b: (b, 0)), kvspec, kvspec],
        out_shape=[jax.ShapeDtypeStruct((batch * seq, C_Q_W), BF16)]
        + [jax.ShapeDtypeStruct((batch, 1, C_KV_HEADS, seq, C_DH), F32)] * 2,
        compiler_params=_cparams("arbitrary"),
        name="sink_attn",
    )(sink3, proj, proj, proj)


def _window_attn_kernel(sink_ref, q_ref, k_ref, v_ref, ck_ref, cv_ref, o_ref, *, band, hkb):
    tq = q_ref.shape[0]
    seq = k_ref.shape[0]
    scale = C_DH ** -0.5
    i = pl.program_id(2)
    start = pl.multiple_of(jnp.clip(i * tq - C_WINDOW, 0, seq - band), C_WINDOW)
    kpos = start + lax.broadcasted_iota(jnp.int32, (band, tq), 0)
    qpos = i * tq + lax.broadcasted_iota(jnp.int32, (band, tq), 1)
    valid = jnp.abs(qpos - kpos) <= C_WINDOW
    valid4 = jnp.concatenate([valid] * C_GROUP, axis=1)
    gw = C_GROUP * C_DH
    s_ctx, s_band, sinks, vbs, cvs = [], [], [], [], []
    for hk in range(hkb):
        kcols = slice(hk * C_DH, (hk + 1) * C_DH)
        kb = k_ref[pl.ds(start, band), kcols].astype(BF16)
        vbs.append(v_ref[pl.ds(start, band), kcols].T.astype(BF16))
        cvs.append(cv_ref[0, 0, hk].T.astype(BF16))
        q4 = jnp.concatenate([q_ref[:, hk * gw + g * C_DH:hk * gw + (g + 1) * C_DH] for g in range(C_GROUP)],
                             axis=0).astype(BF16)
        sinks += [jnp.broadcast_to(sink_ref[hk, 0:1, g:g + 1], (1, tq)) for g in range(C_GROUP)]
        s_band.append(jnp.where(valid4, _dot_nt(kb, q4) * scale, -jnp.inf))
        s_ctx.append(_dot_nt(ck_ref[0, 0, hk].astype(BF16), q4) * scale)
    p_ctx, p_band = _softmax_parts([jnp.concatenate(s_ctx, axis=1), jnp.concatenate(s_band, axis=1)],
                                   extra_logit=jnp.concatenate(sinks, axis=1), axis=0)
    p_ctx, p_band = p_ctx.astype(BF16), p_band.astype(BF16)
    cols_per_kv = C_GROUP * tq
    for hk in range(hkb):
        cols = slice(hk * cols_per_kv, (hk + 1) * cols_per_kv)
        o = (jnp.dot(cvs[hk], p_ctx[:, cols], preferred_element_type=F32)
             + jnp.dot(vbs[hk], p_band[:, cols], preferred_element_type=F32)).T
        for g in range(C_GROUP):
            o_ref[:, hk * gw + g * C_DH:hk * gw + (g + 1) * C_DH] = o[g * tq:(g + 1) * tq].astype(BF16)


def window_attention_latent(proj, sink, ctx_k, ctx_v, e, *, row_off, batch, seq, tq=128, hkb=4):
    tq = min(tq, seq)
    band = min(tq + 2 * C_WINDOW, seq)
    assert row_off % seq == 0 and seq % tq == 0 and tq % C_WINDOW == 0 and C_KV_HEADS % hkb == 0
    past = ctx_k.shape[3]
    qw, kw = hkb * C_GROUP * C_DH, hkb * C_DH
    rb = lambda b: (row_off + b * seq) // seq
    cspec = pl.BlockSpec((1, 1, hkb, past, C_DH), lambda b, hg, i: (b, e, hg, 0, 0))
    sink3 = sink.reshape(C_KV_HEADS, 1, C_GROUP)
    return pl.pallas_call(
        functools.partial(_window_attn_kernel, band=band, hkb=hkb),
        grid=(batch, C_KV_HEADS // hkb, seq // tq),
        in_specs=[pl.BlockSpec((hkb, 1, C_GROUP), lambda b, hg, i: (hg, 0, 0)),
                  pl.BlockSpec((tq, qw), lambda b, hg, i: ((row_off + b * seq) // tq + i, hg)),
                  pl.BlockSpec((seq, kw), lambda b, hg, i: (rb(b), C_Q_W // kw + hg)),
                  pl.BlockSpec((seq, kw), lambda b, hg, i: (rb(b), (C_Q_W + C_KV_W) // kw + hg)),
                  cspec, cspec],
        out_specs=pl.BlockSpec((tq, qw), lambda b, hg, i: (b * (seq // tq) + i, hg)),
        out_shape=jax.ShapeDtypeStruct((batch * seq, C_Q_W), BF16),
        compiler_params=_cparams("arbitrary", "arbitrary", "arbitrary"),
        name="window_attn",
    )(sink3, proj, proj, proj, ctx_k, ctx_v)


def kernel(x_prompt, x_sample, cache_a_k, cache_a_v, state_b_fwd, state_b_bwd, cache_c_k, cache_c_v, c, c_ctx, ada_w, ada_b, norm_g, ffn_w_in, ffn_w_out, ab_w_in, ab_w_out, a_lambda, a_subln_g, b_alpha_w, b_alpha_b, b_norm_g, c_w_in, c_w_out, c_sink, final_g):
    bp, sp, d = x_prompt.shape
    bs, ss, _ = x_sample.shape
    depth = ada_w.shape[0]
    mp, ms = bp * sp, bs * ss
    rows_info = (mp, ss)

    cond = jnp.concatenate([c_ctx[None, :], c, jnp.zeros((COND_ROWS - 1 - bs, d), F32)], axis=0)
    mods = adaln(cond, ada_w, ada_b).reshape(depth, COND_ROWS, N_MOD, d)
    rope_a = _rope_tables(ss, A_DH, 2) + (2 * A_QK_W,)
    rope_c = _rope_tables(ss, C_DH, 1) + (C_Q_W + C_KV_W,)
    ab_w_in, ab_w_out, c_w_in, c_w_out = (w.astype(BF16) for w in (ab_w_in, ab_w_out, c_w_in, c_w_out))

    a_k, a_v, b_f, b_b, c_k, c_v = [], [], [], [], [], []
    xs = [x_prompt.reshape(mp, d), x_sample.reshape(ms, d)]
    for l in range(depth):
        mod = mods[l]
        ffn1 = functools.partial(ffn_half_step, mod3=mod[:, 0:3], g=norm_g[l, 0], w_in=ffn_w_in, w_out=ffn_w_out,
                                 widx=(l, 0), rows_info=rows_info)
        xs = [ffn1(xs[0])] if len(xs) == 1 else [ffn1(xs[0]), ffn1(xs[1], stream_row0=mp)]
        if l % 2 == 0:
            e = l // 2
            lam_init = 0.8 - 0.6 * math.exp(-0.3 * l)
            zeros = jnp.zeros((B_RANK, B_QK_W), F32)
            w_alpha = jnp.concatenate([jnp.concatenate([b_alpha_w[e, 0], zeros], axis=1),
                                       jnp.concatenate([zeros, b_alpha_w[e, 1]], axis=1)], axis=0)
            gate_w = (ab_w_in[e][:, AB_MAIN:], w_alpha, b_alpha_b[e].reshape(-1))
            proj, gates = mixer_in_proj(xs, mod[:, 3:5], norm_g[l, 1], ab_w_in, e, AB_MAIN, rows_info, rope_a,
                                        gates=gate_w)
            attn_p, ak, av = diff_attention(proj, a_lambda[e], a_subln_g[e], lam_init, row_off=0, batch=bp,
                                            seq=sp, tq=sp, hb=A_HEADS, emit_kv=True)
            (attn_s,) = diff_attention(proj, a_lambda[e], a_subln_g[e], lam_init, row_off=mp, batch=bs,
                                       seq=ss, tq=min(256, ss), hb=2, ctx=(cache_a_k, cache_a_v, e))
            gla_p, sf, sb = gla_bidirectional(proj, gates, b_norm_g[e], row_off=0, batch=bp, seq=sp,
                                              emit_state=True)
            (gla_s,) = gla_bidirectional(proj, gates, b_norm_g[e], row_off=mp, batch=bs, seq=ss,
                                         states=(state_b_fwd, state_b_bwd, e))
            a_k.append(ak), a_v.append(av), b_f.append(sf), b_b.append(sb)
            x = mixer_out_proj([attn_p, gla_p], [attn_s, gla_s], ab_w_out, e, xs, mod[:, 5:6], rows_info)
        else:
            o = l // 2
            proj = mixer_in_proj(xs, mod[:, 3:5], norm_g[l, 1], c_w_in, o, c_w_in.shape[2], rows_info, rope_c)
            mix_p, ck, cv = sink_attention_context(proj, c_sink[o], batch=bp, seq=sp)
            mix_s = window_attention_latent(proj, c_sink[o], cache_c_k, cache_c_v, o, row_off=mp, batch=bs, seq=ss)
            c_k.append(ck), c_v.append(cv)
            x = mixer_out_proj([mix_p], [mix_s], c_w_out, o, xs, mod[:, 5:6], rows_info)
        ffn2 = functools.partial(ffn_half_step, x, mod[:, 6:9], norm_g[l, 2], ffn_w_in, ffn_w_out, (l, 1), rows_info)
        if l < depth - 1:
            xs = [ffn2()]

    y_prompt = ffn2(n_rows=mp, final_g=final_g).reshape(bp, sp, d)
    y_sample = ffn2(stream_row0=mp, x_row0=mp, n_rows=ms, final_g=final_g).reshape(bs, ss, d)
    cat = lambda parts: parts[0] if len(parts) == 1 else jnp.concatenate(parts, axis=1)
    return (y_prompt, y_sample, cat(a_k), cat(a_v), cat(b_f), cat(b_b), cat(c_k), cat(c_v))
```
